```python
import math
import jax
import jax.numpy as jnp
from jax import lax
import numpy as np

D_MODEL = 1024
BATCH = 8
SEQ = 4096
DEPTH = 2

N_MIXERS = 2
RMS_EPS = 1e-5

SSD_EXPAND = 2
SSD_D_INNER = SSD_EXPAND * D_MODEL
SSD_HEADDIM = 64
SSD_N_HEADS = SSD_D_INNER // SSD_HEADDIM
SSD_N_GROUPS = 4
SSD_HEADS_PER_GROUP = SSD_N_HEADS // SSD_N_GROUPS
SSD_D_STATE = 128
SSD_D_CONV = 4
SSD_CHUNK = 128
SSD_GN = SSD_N_GROUPS * SSD_D_STATE
SSD_CONV_DIM = SSD_D_INNER + 2 * SSD_GN
SSD_IN_DIM = SSD_D_INNER + SSD_CONV_DIM + SSD_N_HEADS

DA_HEAD_DIM = 64
DA_N_HEADS = D_MODEL // (2 * DA_HEAD_DIM)
DA_Q_BLOCK = 128

MEM_LEN = 256
XA_N_HEADS = 4
XA_HEAD_DIM = D_MODEL // XA_N_HEADS

N_EXPERTS = 32
TOP_K = 4
D_FF = D_MODEL
SWIGLU_LIMIT = 7.0
SWIGLU_ALPHA = 1.702
MOE_BLOCK = 256

N_SSD_LAYERS = (DEPTH + N_MIXERS - 1) // N_MIXERS
N_DA_LAYERS = DEPTH // N_MIXERS

kernel_name = 'hybrid_ssd_diffattn_memxattn_moe'


def rms_norm(x, g):
    xf = x.astype(jnp.float32)
    y = xf * lax.rsqrt(jnp.mean(xf * xf, axis=-1, keepdims=True) + RMS_EPS)
    return (y * g.astype(jnp.float32)).astype(x.dtype)


def causal_depthwise_conv(u, w, b):
    k = w.shape[0]
    out = lax.conv_general_dilated(u, w[:, None, :].astype(u.dtype), window_strides=(1,), padding=[(k - 1, 0)],
                                   dimension_numbers=('NWC', 'WIO', 'NWC'), feature_group_count=u.shape[-1])
    return out + b.astype(u.dtype)


def ssd_chunked_scan(x, dt, a, b_in, c_in):
    bsz, seq = x.shape[0], x.shape[1]
    nc = seq // SSD_CHUNK
    L, G, J = SSD_CHUNK, SSD_N_GROUPS, SSD_HEADS_PER_GROUP
    xd = (x * dt[..., None]).reshape(bsz, nc, L, G, J, SSD_HEADDIM)
    da = jnp.moveaxis((dt * a).reshape(bsz, nc, L, G, J), 2, -1)
    bc = b_in.reshape(bsz, nc, L, G, SSD_D_STATE)
    cc = c_in.reshape(bsz, nc, L, G, SSD_D_STATE)
    a_cum = jnp.cumsum(da, axis=-1)
    causal = jnp.tril(jnp.ones((L, L), dtype=bool))
    decay = jnp.exp(jnp.where(causal, a_cum[..., :, None] - a_cum[..., None, :], -jnp.inf))
    cb = jnp.einsum('bclgn,bcsgn->bcgls', cc, bc)
    y_diag = jnp.einsum('bcgls,bcgjls,bcsgjp->bclgjp', cb, decay, xd)
    decay_to_end = jnp.exp(a_cum[..., -1:] - a_cum)
    chunk_states = jnp.einsum('bclgn,bcgjl,bclgjp->bcgjpn', bc, decay_to_end, xd)
    chunk_decay = jnp.exp(a_cum[..., -1])

    def step(state, inp):
        st, dec = inp
        return state * dec[..., None, None] + st, state

    init = jnp.zeros((bsz, G, J, SSD_HEADDIM, SSD_D_STATE), jnp.float32)
    _, prev = lax.scan(step, init, (jnp.moveaxis(chunk_states, 1, 0), jnp.moveaxis(chunk_decay, 1, 0)))
    prev = jnp.moveaxis(prev, 0, 1)
    y_off = jnp.einsum('bclgn,bcgjpn,bcgjl->bclgjp', cc, prev, jnp.exp(a_cum))
    return (y_diag + y_off).reshape(bsz, seq, SSD_N_HEADS, SSD_HEADDIM)


def ssd_mixer(h, w_in, conv_w, conv_b, dt_bias, a_log, d_skip, norm_g, w_out):
    bsz, seq, _ = h.shape
    proj = h @ w_in
    z = proj[..., :SSD_D_INNER]
    xbc = proj[..., SSD_D_INNER:SSD_D_INNER + SSD_CONV_DIM]
    dt_raw = proj[..., SSD_D_INNER + SSD_CONV_DIM:]
    xbc = jax.nn.silu(causal_depthwise_conv(xbc, conv_w, conv_b)).astype(jnp.float32)
    xs = xbc[..., :SSD_D_INNER].reshape(bsz, seq, SSD_N_HEADS, SSD_HEADDIM)
    b_in = xbc[..., SSD_D_INNER:SSD_D_INNER + SSD_GN].reshape(bsz, seq, SSD_N_GROUPS, SSD_D_STATE)
    c_in = xbc[..., SSD_D_INNER + SSD_GN:].reshape(bsz, seq, SSD_N_GROUPS, SSD_D_STATE)
    dt = jax.nn.softplus(dt_raw.astype(jnp.float32) + dt_bias.astype(jnp.float32))
    a = -jnp.exp(a_log.astype(jnp.float32))
    y = ssd_chunked_scan(xs, dt, a, b_in, c_in) + d_skip.astype(jnp.float32)[:, None] * xs
    u = y.reshape(bsz, seq, SSD_D_INNER) * jax.nn.silu(z.astype(jnp.float32))
    u = u.reshape(bsz, seq, SSD_N_GROUPS, SSD_D_INNER // SSD_N_GROUPS)
    u = u * lax.rsqrt(jnp.mean(u * u, axis=-1, keepdims=True) + RMS_EPS)
    u = u.reshape(bsz, seq, SSD_D_INNER) * norm_g.astype(jnp.float32)
    return u.astype(h.dtype) @ w_out


def diff_attention(h, w_qkv, lq1, lk1, lq2, lk2, subln_g, w_o, layer_idx):
    bsz, seq, _ = h.shape
    qkv = h @ w_qkv
    q = qkv[..., :D_MODEL].reshape(bsz, seq, DA_N_HEADS, 2, DA_HEAD_DIM)
    k = qkv[..., D_MODEL:2 * D_MODEL].reshape(bsz, seq, DA_N_HEADS, 2, DA_HEAD_DIM)
    v = qkv[..., 2 * D_MODEL:].reshape(bsz, seq, DA_N_HEADS, 2 * DA_HEAD_DIM)
    lambda_init = 0.8 - 0.6 * math.exp(-0.3 * layer_idx)
    lam = (jnp.exp(jnp.sum(lq1.astype(jnp.float32) * lk1.astype(jnp.float32)))
           - jnp.exp(jnp.sum(lq2.astype(jnp.float32) * lk2.astype(jnp.float32))) + lambda_init)
    scale = DA_HEAD_DIM ** -0.5
    nqb = seq // DA_Q_BLOCK
    q_blocks = jnp.swapaxes(q.reshape(bsz, nqb, DA_Q_BLOCK, DA_N_HEADS, 2, DA_HEAD_DIM), 0, 1)
    kpos = jnp.arange(seq)

    def one_block(args):
        q_blk, start = args
        s = jnp.einsum('bqhmd,bkhmd->bhmqk', q_blk, k).astype(jnp.float32) * scale
        qpos = start + jnp.arange(DA_Q_BLOCK)
        mask = kpos[None, :] <= qpos[:, None]
        p = jax.nn.softmax(jnp.where(mask, s, -jnp.inf), axis=-1)
        attn = p[:, :, 0] - lam * p[:, :, 1]
        return jnp.einsum('bhqk,bkhe->bqhe', attn.astype(v.dtype), v)

    o = lax.map(one_block, (q_blocks, jnp.arange(nqb) * DA_Q_BLOCK))
    o = jnp.swapaxes(o, 0, 1).reshape(bsz, seq, DA_N_HEADS, 2 * DA_HEAD_DIM)
    o = rms_norm(o, subln_g) * (1.0 - lambda_init)
    return o.reshape(bsz, seq, D_MODEL) @ w_o


def memory_cross_attention(h, mem_n, w_q, w_kv, w_o):
    bsz, seq, _ = h.shape
    m = mem_n.shape[1]
    q = (h @ w_q).reshape(bsz, seq, XA_N_HEADS, XA_HEAD_DIM)
    kv = mem_n @ w_kv
    k = kv[..., :D_MODEL].reshape(bsz, m, XA_N_HEADS, XA_HEAD_DIM)
    v = kv[..., D_MODEL:].reshape(bsz, m, XA_N_HEADS, XA_HEAD_DIM)
    s = jnp.einsum('bshd,bmhd->bhsm', q, k).astype(jnp.float32) * (XA_HEAD_DIM ** -0.5)
    p = jax.nn.softmax(s, axis=-1)
    o = jnp.einsum('bhsm,bmhd->bshd', p.astype(v.dtype), v).reshape(bsz, seq, D_MODEL)
    return o @ w_o


def moe_ffn(h, w_router, b_router, w_gate_up, b_gate_up, w_down, b_down):
    bsz, seq, d = h.shape
    t = bsz * seq
    ht = h.reshape(t, d)
    logits = (ht @ w_router).astype(jnp.float32) + b_router.astype(jnp.float32)
    top_logits, top_idx = lax.top_k(logits, TOP_K)
    gates = jax.nn.softmax(top_logits, axis=-1)
    n = t * TOP_K
    flat_e = top_idx.reshape(n)
    flat_tok = jnp.repeat(jnp.arange(t, dtype=jnp.int32), TOP_K)
    flat_g = gates.reshape(n)
    order = jnp.argsort(flat_e)
    sorted_e = flat_e[order]
    counts = jax.ops.segment_sum(jnp.ones((n,), jnp.int32), flat_e, num_segments=N_EXPERTS)
    padded = (counts + MOE_BLOCK - 1) // MOE_BLOCK * MOE_BLOCK
    start = jnp.cumsum(counts) - counts
    pend = jnp.cumsum(padded)
    pstart = pend - padded
    dest = pstart[sorted_e] + (jnp.arange(n, dtype=jnp.int32) - start[sorted_e])
    p_rows = -(-(n + N_EXPERTS * MOE_BLOCK) // MOE_BLOCK) * MOE_BLOCK
    nb = p_rows // MOE_BLOCK
    row_tok = jnp.full((p_rows,), t, jnp.int32).at[dest].set(flat_tok[order])
    row_gate = jnp.zeros((p_rows,), jnp.float32).at[dest].set(flat_g[order])
    blk_e = jnp.minimum(jnp.searchsorted(pend, jnp.arange(nb, dtype=jnp.int32) * MOE_BLOCK, side='right'), N_EXPERTS - 1)
    xs = ht[jnp.minimum(row_tok, t - 1)].reshape(nb, MOE_BLOCK, d)

    def expert_block(args):
        xb, e = args
        gu = xb @ w_gate_up[e] + b_gate_up[e]
        gate = jnp.minimum(gu[..., :D_FF], SWIGLU_LIMIT)
        up = jnp.clip(gu[..., D_FF:], -SWIGLU_LIMIT, SWIGLU_LIMIT)
        act = (up + 1.0) * (gate * jax.nn.sigmoid(gate * SWIGLU_ALPHA))
        return act @ w_down[e] + b_down[e]

    ys = lax.map(expert_block, (xs, blk_e)).reshape(p_rows, d)
    out = jnp.zeros((t, d), h.dtype).at[row_tok].add(ys * row_gate[:, None].astype(ys.dtype), mode='drop')
    return out.reshape(bsz, seq, d)


def setup_inputs(seed: int = 0) -> dict:
    key = jax.random.key(seed)
    ks = iter(jax.random.split(key, 48))
    f32 = jnp.float32

    def nrm(shape, scale):
        return jax.random.normal(next(ks), shape, f32) * scale

    def gain(shape):
        return 1.0 + nrm(shape, 0.02)

    ns, na, L = N_SSD_LAYERS, N_DA_LAYERS, DEPTH
    u = jax.random.uniform(next(ks), (ns, SSD_N_HEADS), f32)
    dt0 = jnp.exp(u * (math.log(0.1) - math.log(0.001)) + math.log(0.001))
    dt_bias = dt0 + jnp.log(-jnp.expm1(-dt0))
    a_log = jnp.log(jax.random.uniform(next(ks), (ns, SSD_N_HEADS), f32, 1.0, 16.0))
    return {
        'x': nrm((BATCH, SEQ, D_MODEL), 1.0),
        'mem': nrm((BATCH, MEM_LEN, D_MODEL), 1.0),
        'mixer_norm': gain((L, D_MODEL)),
        'xattn_norm': gain((L, D_MODEL)),
        'mem_norm': gain((L, D_MODEL)),
        'ffn_norm': gain((L, D_MODEL)),
        'ssd_w_in': nrm((ns, D_MODEL, SSD_IN_DIM), D_MODEL ** -0.5),
        'ssd_conv_w': nrm((ns, SSD_D_CONV, SSD_CONV_DIM), SSD_D_CONV ** -0.5),
        'ssd_conv_b': nrm((ns, SSD_CONV_DIM), 0.02),
        'ssd_dt_bias': dt_bias,
        'ssd_a_log': a_log,
        'ssd_d': gain((ns, SSD_N_HEADS)),
        'ssd_norm': gain((ns, SSD_D_INNER)),
        'ssd_w_out': nrm((ns, SSD_D_INNER, D_MODEL), SSD_D_INNER ** -0.5),
        'da_w_qkv': nrm((na, D_MODEL, 3 * D_MODEL), D_MODEL ** -0.5),
        'da_lambda_q1': nrm((na, DA_HEAD_DIM), 0.1),
        'da_lambda_k1': nrm((na, DA_HEAD_DIM), 0.1),
        'da_lambda_q2': nrm((na, DA_HEAD_DIM), 0.1),
        'da_lambda_k2': nrm((na, DA_HEAD_DIM), 0.1),
        'da_subln': gain((na, 2 * DA_HEAD_DIM)),
        'da_w_o': nrm((na, D_MODEL, D_MODEL), D_MODEL ** -0.5),
        'xa_w_q': nrm((L, D_MODEL, D_MODEL), D_MODEL ** -0.5),
        'xa_w_kv': nrm((L, D_MODEL, 2 * D_MODEL), D_MODEL ** -0.5),
        'xa_w_o': nrm((L, D_MODEL, D_MODEL), D_MODEL ** -0.5),
        'moe_w_router': nrm((L, D_MODEL, N_EXPERTS), D_MODEL ** -0.5),
        'moe_b_router': nrm((L, N_EXPERTS), 0.01),
        'moe_w_gate_up': nrm((L, N_EXPERTS, D_MODEL, 2 * D_FF), D_MODEL ** -0.5),
        'moe_b_gate_up': nrm((L, N_EXPERTS, 2 * D_FF), 0.02),
        'moe_w_down': nrm((L, N_EXPERTS, D_FF, D_MODEL), D_FF ** -0.5),
        'moe_b_down': nrm((L, N_EXPERTS, D_MODEL), 0.02),
        'final_norm': gain((D_MODEL,)),
    }


def reference(x, mem, mixer_norm, xattn_norm, mem_norm, ffn_norm,
              ssd_w_in, ssd_conv_w, ssd_conv_b, ssd_dt_bias, ssd_a_log, ssd_d, ssd_norm, ssd_w_out,
              da_w_qkv, da_lambda_q1, da_lambda_k1, da_lambda_q2, da_lambda_k2, da_subln, da_w_o,
              xa_w_q, xa_w_kv, xa_w_o,
              moe_w_router, moe_b_router, moe_w_gate_up, moe_b_gate_up, moe_w_down, moe_b_down,
              final_norm):
    h = x
    for i in range(DEPTH):
        hn = rms_norm(h, mixer_norm[i])
        j = i // N_MIXERS
        if i % N_MIXERS == 0:
            h = h + ssd_mixer(hn, ssd_w_in[j], ssd_conv_w[j], ssd_conv_b[j], ssd_dt_bias[j], ssd_a_log[j],
                              ssd_d[j], ssd_norm[j], ssd_w_out[j])
        else:
            h = h + diff_attention(hn, da_w_qkv[j], da_lambda_q1[j], da_lambda_k1[j], da_lambda_q2[j],
                                   da_lambda_k2[j], da_subln[j], da_w_o[j], i)
        h = h + memory_cross_attention(rms_norm(h, xattn_norm[i]), rms_norm(mem, mem_norm[i]),
                                       xa_w_q[i], xa_w_kv[i], xa_w_o[i])
        h = h + moe_ffn(rms_norm(h, ffn_norm[i]), moe_w_router[i], moe_b_router[i], moe_w_gate_up[i],
                        moe_b_gate_up[i], moe_w_down[i], moe_b_down[i])
    return rms_norm(h, final_norm)
```

```python
import functools
import math

import jax
import jax.numpy as jnp
from jax import lax
from jax.experimental import pallas as pl
from jax.experimental.pallas import tpu as pltpu

F32 = jnp.float32
BF16 = jnp.bfloat16

D_MODEL = 1024
RMS_EPS = 1e-5
N_MIXERS = 2

SSD_D_INNER = 2048
SSD_HEADDIM = 64
SSD_N_HEADS = 32
SSD_N_GROUPS = 4
SSD_HEADS_PER_GROUP = 8
SSD_D_STATE = 128
SSD_D_CONV = 4
SSD_CHUNK = 128
SSD_GN = 512
SSD_CONV_DIM = 3072
SSD_GROUP_WIDTH = SSD_D_INNER // SSD_N_GROUPS

DA_HEAD_DIM = 64
DA_N_HEADS = 8

XA_N_HEADS = 4
XA_HEAD_DIM = 256

N_EXPERTS = 32
TOP_K = 4
D_FF = 1024
SWIGLU_LIMIT = 7.0
SWIGLU_ALPHA = 1.702

LANES = 128
SUBLANES = 8
VMEM_LIMIT_BYTES = 56 * 1024 * 1024

MOE_ROWS_PER_BLOCK = 256
ROUTER_TILE = 512
DISPATCH_TILE = 256
COMBINE_TILE = 256


def _cparams(*sem):
    return pltpu.CompilerParams(dimension_semantics=sem, vmem_limit_bytes=VMEM_LIMIT_BYTES)


def _rms(x, g):
    ms = jnp.mean(x * x, axis=-1, keepdims=True)
    return x * lax.rsqrt(ms + RMS_EPS) * g


def _dot(a, b):
    return jnp.dot(a, b, preferred_element_type=F32)


def _dot_nt(a, b):
    return lax.dot_general(a, b, (((1,), (1,)), ((), ())), preferred_element_type=F32)


def _split2(x):
    hi = x.astype(BF16)
    lo = (x - hi.astype(F32)).astype(BF16)
    return hi, lo


def _split3(x):
    hi = x.astype(BF16)
    r = x - hi.astype(F32)
    mid = r.astype(BF16)
    lo = (r - mid.astype(F32)).astype(BF16)
    return hi, mid, lo


def _norm_mm_kernel(x_ref, g_ref, w_ref, o_ref, xn_ref):
    @pl.when(pl.program_id(1) == 0)
    def _():
        xn_ref[...] = _rms(x_ref[...], g_ref[...]).astype(BF16)

    o_ref[...] = _dot(xn_ref[...], w_ref[...]).astype(o_ref.dtype)


def norm_matmul(x, g, w, out_dtype, tm, tn):
    m, k = x.shape
    n = w.shape[1]
    tm = min(tm, m)
    tn = min(tn, n)
    return pl.pallas_call(
        _norm_mm_kernel,
        grid=(m // tm, n // tn),
        in_specs=[
            pl.BlockSpec((tm, k), lambda i, j: (i, 0)),
            pl.BlockSpec((1, k), lambda i, j: (0, 0)),
            pl.BlockSpec((k, tn), lambda i, j: (0, j)),
        ],
        out_specs=pl.BlockSpec((tm, tn), lambda i, j: (i, j)),
        out_shape=jax.ShapeDtypeStruct((m, n), out_dtype),
        scratch_shapes=[pltpu.VMEM((tm, k), BF16)],
        compiler_params=_cparams("parallel", "arbitrary"),
        name="norm_matmul",
    )(x, g.reshape(1, k), w)


def _mm_res_kernel(x_ref, w_ref, r_ref, o_ref):
    o_ref[...] = r_ref[...] + _dot(x_ref[...], w_ref[...])


def matmul_residual(x, w, res, tm=512):
    m, k = x.shape
    n = w.shape[1]
    tm = min(tm, m)
    return pl.pallas_call(
        _mm_res_kernel,
        grid=(m // tm,),
        in_specs=[
            pl.BlockSpec((tm, k), lambda i: (i, 0)),
            pl.BlockSpec((k, n), lambda i: (0, 0)),
            pl.BlockSpec((tm, n), lambda i: (i, 0)),
        ],
        out_specs=pl.BlockSpec((tm, n), lambda i: (i, 0)),
        out_shape=jax.ShapeDtypeStruct((m, n), F32),
        compiler_params=_cparams("parallel"),
        name="matmul_residual",
    )(x, w, res)


def _norm_kernel(x_ref, g_ref, o_ref):
    o_ref[...] = _rms(x_ref[...], g_ref[...])


def final_norm_call(x, g, tm=1024):
    m, k = x.shape
    tm = min(tm, m)
    return pl.pallas_call(
        _norm_kernel,
        grid=(m // tm,),
        in_specs=[pl.BlockSpec((tm, k), lambda i: (i, 0)), pl.BlockSpec((1, k), lambda i: (0, 0))],
        out_specs=pl.BlockSpec((tm, k), lambda i: (i, 0)),
        out_shape=jax.ShapeDtypeStruct((m, k), F32),
        compiler_params=_cparams("parallel"),
        name="final_norm",
    )(x, g.reshape(1, k))


def _ssd_kernel(z0_ref, z1_ref, x0_ref, x1_ref, bc_ref, dtr_ref, h_ref,
                convw_ref, convb_ref, dtb_ref, alog_ref, dexp_ref, ng_ref, expand_ref, wout_ref,
                o_ref, state_ref, ext_ref):
    L = SSD_CHUNK
    c = pl.program_id(1)

    @pl.when(c == 0)
    def _():
        state_ref[...] = jnp.zeros_like(state_ref)
        ext_ref[0:SUBLANES, :] = jnp.zeros((SUBLANES, SSD_CONV_DIM), F32)

    pieces = []
    for blk, ref in enumerate((x0_ref, x1_ref, bc_ref)):
        cols = slice(blk * 1024, (blk + 1) * 1024)
        ext_ref[SUBLANES:SUBLANES + L, cols] = ref[...].astype(F32)
        acc = convb_ref[:, cols]
        for k in range(SSD_D_CONV):
            start = SUBLANES - (SSD_D_CONV - 1) + k
            acc = acc + convw_ref[k:k + 1, cols] * ext_ref[start:start + L, cols]
        pieces.append(acc * jax.nn.sigmoid(acc))
        ext_ref[0:SUBLANES, cols] = ext_ref[L:L + SUBLANES, cols]
    xs = jnp.concatenate(pieces[:2], axis=-1)
    b_all = pieces[2][:, :SSD_GN]
    c_all = pieces[2][:, SSD_GN:]

    dtr = dtr_ref[...] + dtb_ref[...]
    dt = jnp.maximum(dtr, 0.0) + jnp.log1p(jnp.exp(-jnp.abs(dtr)))
    a = -jnp.exp(alog_ref[...])
    da = dt * a
    row = lax.broadcasted_iota(jnp.int32, (L, L), 0)
    col = lax.broadcasted_iota(jnp.int32, (L, L), 1)
    causal = col <= row
    tril = jnp.where(causal, 1.0, 0.0).astype(BF16)
    d_hi, d_mid, d_lo = _split3(da)
    a_cum = _dot(tril, d_hi) + _dot(tril, d_mid) + _dot(tril, d_lo)
    a_cum_t = a_cum.T

    expand = expand_ref[...]
    t_hi, t_lo = _split2(dt)
    dt_e = _dot(t_hi, expand) + _dot(t_lo, expand)
    c_hi, c_mid, c_lo = _split3(a_cum)
    acum_e = _dot(c_hi, expand) + _dot(c_mid, expand) + _dot(c_lo, expand)
    alast_e = acum_e[L - 1:L, :]
    exp_acum_e = jnp.exp(acum_e)
    dte_e = jnp.exp(alast_e - acum_e)
    cd_e = jnp.exp(alast_e)

    xd = xs * dt_e
    xd_b = xd.astype(BF16)
    xdw_b = (xd * dte_e).astype(BF16)

    lane = lax.broadcasted_iota(jnp.int32, (L, LANES), 1)
    first_half = lane < SSD_HEADDIM

    y_parts = []
    for g in range(SSD_N_GROUPS):
        gs = slice(g * SSD_D_STATE, (g + 1) * SSD_D_STATE)
        gw = slice(g * SSD_GROUP_WIDTH, (g + 1) * SSD_GROUP_WIDTH)
        b_g = b_all[:, gs]
        c_g = c_all[:, gs].astype(BF16)
        cb = _dot_nt(c_g, b_g.astype(BF16))
        y_pairs = []
        for jp in range(SSD_HEADS_PER_GROUP // 2):
            res = []
            pair_col = g * SSD_GROUP_WIDTH + jp * LANES
            xd_pair = xd_b[:, pair_col:pair_col + LANES]
            for sub in range(2):
                hd = g * SSD_HEADS_PER_GROUP + jp * 2 + sub
                diff = a_cum[:, hd:hd + 1] - a_cum_t[hd:hd + 1, :]
                dec = jnp.exp(jnp.where(causal, diff, -jnp.inf))
                res.append(_dot((cb * dec).astype(BF16), xd_pair))
            y_pairs.append(jnp.where(first_half, res[0], res[1]))
        y_diag = jnp.concatenate(y_pairs, axis=-1)
        st = state_ref[g]
        y_off = _dot(c_g, st.astype(BF16)) * exp_acum_e[:, gw]
        state_ref[g] = st * cd_e[:, gw] + _dot(b_g.T.astype(BF16), xdw_b[:, gw])
        y_parts.append(y_diag + y_off)
    y = jnp.concatenate(y_parts, axis=-1) + dexp_ref[...] * xs

    z = jnp.concatenate([z0_ref[...], z1_ref[...]], axis=-1).astype(F32)
    u = y * (z * jax.nn.sigmoid(z))
    u_parts = []
    for g in range(SSD_N_GROUPS):
        gw = slice(g * SSD_GROUP_WIDTH, (g + 1) * SSD_GROUP_WIDTH)
        ug = u[:, gw]
        ms = jnp.mean(ug * ug, axis=-1, keepdims=True)
        u_parts.append(ug * lax.rsqrt(ms + RMS_EPS) * ng_ref[:, gw])
    un = jnp.concatenate(u_parts, axis=-1).astype(BF16)
    o_ref[...] = h_ref[...] + _dot(un, wout_ref[...])


def ssd_core(zxbc, dt_raw, h, conv_w, conv_b, dt_bias, a_log, d_skip, norm_g, w_out_b):
    bsz, seq, _ = h.shape
    L = SSD_CHUNK
    nc = seq // L
    pad_heads = LANES - SSD_N_HEADS
    dtb = jnp.pad(dt_bias.astype(F32), (0, pad_heads)).reshape(1, LANES)
    alog = jnp.pad(a_log.astype(F32), (0, pad_heads)).reshape(1, LANES)
    dexp = jnp.repeat(d_skip.astype(F32), SSD_HEADDIM).reshape(1, SSD_D_INNER)
    head_of_col = jnp.arange(SSD_D_INNER, dtype=jnp.int32) // SSD_HEADDIM
    expand = (jnp.arange(LANES, dtype=jnp.int32)[:, None] == head_of_col[None, :]).astype(BF16)

    def zx_spec(k):
        return pl.BlockSpec((None, L, 1024), lambda b, c, k=k: (b, c, k))

    def const_spec(shape):
        return pl.BlockSpec(shape, lambda b, c: (0,) * len(shape))

    return pl.pallas_call(
        _ssd_kernel,
        grid=(bsz, nc),
        in_specs=[
            zx_spec(0), zx_spec(1), zx_spec(2), zx_spec(3), zx_spec(4),
            pl.BlockSpec((None, L, LANES), lambda b, c: (b, c, 0)),
            pl.BlockSpec((None, L, D_MODEL), lambda b, c: (b, c, 0)),
            const_spec((SSD_D_CONV, SSD_CONV_DIM)),
            const_spec((1, SSD_CONV_DIM)),
            const_spec((1, LANES)),
            const_spec((1, LANES)),
            const_spec((1, SSD_D_INNER)),
            const_spec((1, SSD_D_INNER)),
            const_spec((LANES, SSD_D_INNER)),
            const_spec((SSD_D_INNER, D_MODEL)),
        ],
        out_specs=pl.BlockSpec((None, L, D_MODEL), lambda b, c: (b, c, 0)),
        out_shape=jax.ShapeDtypeStruct((bsz, seq, D_MODEL), F32),
        scratch_shapes=[
            pltpu.VMEM((SSD_N_GROUPS, SSD_D_STATE, SSD_GROUP_WIDTH), F32),
            pltpu.VMEM((SUBLANES + L, SSD_CONV_DIM), F32),
        ],
        compiler_params=_cparams("parallel", "arbitrary"),
        name="ssd_core",
    )(zxbc, zxbc, zxbc, zxbc, zxbc, dt_raw, h,
      conv_w.astype(F32), conv_b.astype(F32).reshape(1, SSD_CONV_DIM), dtb, alog, dexp,
      norm_g.astype(F32).reshape(1, SSD_D_INNER), expand, w_out_b)


def ssd_layer(h, norm_g_in, w_in, conv_w, conv_b, dt_bias, a_log, d_skip, norm_g, w_out):
    bsz, seq, d = h.shape
    h2 = h.reshape(bsz * seq, d)
    n_main = SSD_D_INNER + SSD_CONV_DIM
    w_main = w_in[:, :n_main].astype(BF16)
    w_dt = jnp.pad(w_in[:, n_main:], ((0, 0), (0, LANES - SSD_N_HEADS))).astype(BF16)
    zxbc = norm_matmul(h2, norm_g_in, w_main, BF16, tm=1024, tn=1024)
    dt_raw = norm_matmul(h2, norm_g_in, w_dt, F32, tm=1024, tn=LANES)
    return ssd_core(zxbc.reshape(bsz, seq, n_main), dt_raw.reshape(bsz, seq, LANES), h,
                    conv_w, conv_b, dt_bias, a_log, d_skip, norm_g, w_out.astype(BF16))


def _da_kernel(lq1_ref, lk1_ref, lq2_ref, lk2_ref, sub_ref, q_ref, k_ref, v_ref, o_ref,
               acc1_ref, acc2_ref, m1_ref, l1_ref, m2_ref, l2_ref, *, tq, lambda_init):
    i = pl.program_id(2)
    q = q_ref[...]
    lane = lax.broadcasted_iota(jnp.int32, (tq, LANES), 1)
    scale = DA_HEAD_DIM ** -0.5
    qs = (q.astype(F32) * scale).astype(BF16)
    zero = jnp.zeros_like(qs)
    q_maps = (jnp.where(lane < DA_HEAD_DIM, qs, zero), jnp.where(lane >= DA_HEAD_DIM, qs, zero))
    states = ((m1_ref, l1_ref, acc1_ref), (m2_ref, l2_ref, acc2_ref))

    for m_ref, l_ref, acc_ref in states:
        m_ref[...] = jnp.full((tq, 1), -jnp.inf, F32)
        l_ref[...] = jnp.zeros((tq, 1), F32)
        acc_ref[...] = jnp.zeros((tq, LANES), F32)

    def step(j, masked):
        start = pl.multiple_of(j * tq, tq)
        kt = k_ref[pl.ds(start, tq), :]
        vt = v_ref[pl.ds(start, tq), :]
        for qm, (m_ref, l_ref, acc_ref) in zip(q_maps, states):
            s = _dot_nt(qm, kt)
            if masked:
                r = lax.broadcasted_iota(jnp.int32, (tq, tq), 0)
                cidx = lax.broadcasted_iota(jnp.int32, (tq, tq), 1)
                s = jnp.where(cidx <= r, s, -jnp.inf)
            m_old = m_ref[...]
            m_new = jnp.maximum(m_old, jnp.max(s, axis=-1, keepdims=True))
            alpha = jnp.exp(m_old - m_new)
            p = jnp.exp(s - m_new)
            l_ref[...] = alpha * l_ref[...] + jnp.sum(p, axis=-1, keepdims=True)
            acc_ref[...] = alpha * acc_ref[...] + _dot(p.astype(BF16), vt)
            m_ref[...] = m_new

    def body(j, carry):
        step(j, False)
        return carry

    lax.fori_loop(0, i, body, 0)
    step(i, True)

    lam = (jnp.exp(jnp.sum(lq1_ref[...] * lk1_ref[...], axis=-1, keepdims=True))
           - jnp.exp(jnp.sum(lq2_ref[...] * lk2_ref[...], axis=-1, keepdims=True)) + lambda_init)
    o = acc1_ref[...] / l1_ref[...] - lam * (acc2_ref[...] / l2_ref[...])
    o = _rms(o, sub_ref[...]) * (1.0 - lambda_init)
    o_ref[...] = o.astype(o_ref.dtype)


def diff_attention_core(qkv, lq1, lk1, lq2, lk2, subln_g, layer_idx, tq=512):
    bsz, seq, _ = qkv.shape
    tq = min(tq, seq)
    lambda_init = 0.8 - 0.6 * math.exp(-0.3 * layer_idx)
    nh = DA_N_HEADS

    def vec_spec(n):
        return pl.BlockSpec((1, n), lambda b, h, i: (0, 0))

    return pl.pallas_call(
        functools.partial(_da_kernel, tq=tq, lambda_init=lambda_init),
        grid=(bsz, nh, seq // tq),
        in_specs=[
            vec_spec(DA_HEAD_DIM), vec_spec(DA_HEAD_DIM), vec_spec(DA_HEAD_DIM), vec_spec(DA_HEAD_DIM),
            vec_spec(LANES),
            pl.BlockSpec((None, tq, LANES), lambda b, h, i: (b, i, h)),
            pl.BlockSpec((None, seq, LANES), lambda b, h, i: (b, 0, nh + h)),
            pl.BlockSpec((None, seq, LANES), lambda b, h, i: (b, 0, 2 * nh + h)),
        ],
        out_specs=pl.BlockSpec((None, tq, LANES), lambda b, h, i: (b, i, h)),
        out_shape=jax.ShapeDtypeStruct((bsz, seq, D_MODEL), BF16),
        scratch_shapes=[
            pltpu.VMEM((tq, LANES), F32), pltpu.VMEM((tq, LANES), F32),
            pltpu.VMEM((tq, 1), F32), pltpu.VMEM((tq, 1), F32),
            pltpu.VMEM((tq, 1), F32), pltpu.VMEM((tq, 1), F32),
        ],
        compiler_params=_cparams("parallel", "parallel", "arbitrary"),
        name="diff_attention",
    )(lq1.astype(F32).reshape(1, -1), lk1.astype(F32).reshape(1, -1),
      lq2.astype(F32).reshape(1, -1), lk2.astype(F32).reshape(1, -1),
      subln_g.astype(F32).reshape(1, -1), qkv, qkv, qkv)


def da_layer(h, norm_g_in, w_qkv, lq1, lk1, lq2, lk2, subln_g, w_o, layer_idx):
    bsz, seq, d = h.shape
    h2 = h.reshape(bsz * seq, d)
    qkv = norm_matmul(h2, norm_g_in, w_qkv.astype(BF16), BF16, tm=1024, tn=1024)
    o = diff_attention_core(qkv.reshape(bsz, seq, 3 * d), lq1, lk1, lq2, lk2, subln_g, layer_idx)
    return matmul_residual(o.reshape(bsz * seq, d), w_o.astype(BF16), h2).reshape(bsz, seq, d)


def _xattn_kernel(h_ref, g_ref, wq_ref, kv_ref, wo_ref, o_ref):
    h = h_ref[...]
    hn = _rms(h, g_ref[...]).astype(BF16)
    scale = XA_HEAD_DIM ** -0.5
    q = (_dot(hn, wq_ref[...]) * scale).astype(BF16)
    outs = []
    for hd in range(XA_N_HEADS):
        cs = slice(hd * XA_HEAD_DIM, (hd + 1) * XA_HEAD_DIM)
        vs = slice(D_MODEL + hd * XA_HEAD_DIM, D_MODEL + (hd + 1) * XA_HEAD_DIM)
        s = _dot_nt(q[:, cs], kv_ref[:, cs])
        m = jnp.max(s, axis=-1, keepdims=True)
        p = jnp.exp(s - m)
        l = jnp.sum(p, axis=-1, keepdims=True)
        outs.append((_dot(p.astype(BF16), kv_ref[:, vs]) / l).astype(BF16))
    o = jnp.concatenate(outs, axis=-1)
    o_ref[...] = h + _dot(o, wo_ref[...])


def xattn_layer(h, mem, norm_g, mem_norm_g, w_q, w_kv, w_o, tq=512):
    bsz, seq, d = h.shape
    mlen = mem.shape[1]
    tq = min(tq, seq)
    kv = norm_matmul(mem.reshape(bsz * mlen, d), mem_norm_g, w_kv.astype(BF16), BF16, tm=512, tn=1024)
    kv = kv.reshape(bsz, mlen, 2 * d)
    return pl.pallas_call(
        _xattn_kernel,
        grid=(bsz, seq // tq),
        in_specs=[
            pl.BlockSpec((None, tq, d), lambda b, i: (b, i, 0)),
            pl.BlockSpec((1, d), lambda b, i: (0, 0)),
            pl.BlockSpec((d, d), lambda b, i: (0, 0)),
            pl.BlockSpec((None, mlen, 2 * d), lambda b, i: (b, 0, 0)),
            pl.BlockSpec((d, d), lambda b, i: (0, 0)),
        ],
        out_specs=pl.BlockSpec((None, tq, d), lambda b, i: (b, i, 0)),
        out_shape=jax.ShapeDtypeStruct((bsz, seq, d), F32),
        compiler_params=_cparams("parallel", "parallel"),
        name="mem_xattn",
    )(h, norm_g.astype(F32).reshape(1, d), w_q.astype(BF16), kv, w_o.astype(BF16))


SLAB = D_MODEL // LANES


def _router_kernel(h_ref, g_ref, wh_ref, wl_ref, br_ref, su_ref,
                   slab_ref, idx_ref, gate_ref, rank_ref, cnt_ref, run_ref, *, tm):
    i = pl.program_id(0)

    @pl.when(i == 0)
    def _():
        run_ref[...] = jnp.zeros_like(run_ref)

    hn = _rms(h_ref[...], g_ref[...])
    for s in range(SLAB):
        slab_ref[pl.ds(s, tm, stride=SLAB), :] = hn[:, s * LANES:(s + 1) * LANES]

    x_hi, x_lo = _split2(hn)
    wh = wh_ref[...]
    logits = _dot_nt(wh, x_hi) + _dot_nt(wh, x_lo) + _dot_nt(wl_ref[...], x_hi) + br_ref[...]

    rows = lax.broadcasted_iota(jnp.int32, (N_EXPERTS, tm), 0).astype(F32)
    tops, idxs, onehots = [], [], []
    cur = logits
    for _ in range(TOP_K):
        m = jnp.max(cur, axis=0, keepdims=True)
        idx = jnp.min(jnp.where(cur == m, rows, float(N_EXPERTS)), axis=0, keepdims=True)
        oh = rows == idx
        cur = jnp.where(oh, -jnp.inf, cur)
        tops.append(m)
        idxs.append(idx)
        onehots.append(oh)
    exps = [jnp.exp(t - tops[0]) for t in tops]
    denom = exps[0] + exps[1] + exps[2] + exps[3]
    gate_ref[...] = jnp.concatenate([e / denom for e in exps], axis=0)
    idx_ref[...] = jnp.concatenate(idxs, axis=0).astype(jnp.int32)

    oh_sum = jnp.zeros((N_EXPERTS, tm), F32)
    for oh in onehots:
        oh_sum = oh_sum + jnp.where(oh, 1.0, 0.0)
    run = run_ref[...]
    prefix = _dot(oh_sum.astype(BF16), su_ref[...]) + run[:, 0:1]
    ranks = [jnp.sum(jnp.where(oh, prefix, 0.0), axis=0, keepdims=True) for oh in onehots]
    rank_ref[...] = jnp.concatenate(ranks, axis=0).astype(jnp.int32)
    run_new = run + jnp.sum(oh_sum, axis=1, keepdims=True)
    run_ref[...] = run_new
    cnt_ref[...] = run_new


def moe_router(h2, norm_g, w_router, b_router, tm=ROUTER_TILE):
    t, d = h2.shape
    tm = min(tm, t)
    wt = w_router.astype(F32).T
    wh = wt.astype(BF16)
    wl = (wt - wh.astype(F32)).astype(BF16)
    su = (jnp.arange(tm)[:, None] < jnp.arange(tm)[None, :]).astype(BF16)
    return pl.pallas_call(
        functools.partial(_router_kernel, tm=tm),
        grid=(t // tm,),
        in_specs=[
            pl.BlockSpec((tm, d), lambda i: (i, 0)),
            pl.BlockSpec((1, d), lambda i: (0, 0)),
            pl.BlockSpec((N_EXPERTS, d), lambda i: (0, 0)),
            pl.BlockSpec((N_EXPERTS, d), lambda i: (0, 0)),
            pl.BlockSpec((N_EXPERTS, 1), lambda i: (0, 0)),
            pl.BlockSpec((tm, tm), lambda i: (0, 0)),
        ],
        out_specs=[
            pl.BlockSpec((tm * SLAB, LANES), lambda i: (i, 0)),
            pl.BlockSpec((TOP_K, tm), lambda i: (0, i)),
            pl.BlockSpec((TOP_K, tm), lambda i: (0, i)),
            pl.BlockSpec((TOP_K, tm), lambda i: (0, i)),
            pl.BlockSpec((N_EXPERTS, LANES), lambda i: (0, 0)),
        ],
        out_shape=[
            jax.ShapeDtypeStruct((t * SLAB, LANES), F32),
            jax.ShapeDtypeStruct((TOP_K, t), jnp.int32),
            jax.ShapeDtypeStruct((TOP_K, t), F32),
            jax.ShapeDtypeStruct((TOP_K, t), jnp.int32),
            jax.ShapeDtypeStruct((N_EXPERTS, LANES), F32),
        ],
        scratch_shapes=[pltpu.VMEM((N_EXPERTS, LANES), F32)],
        compiler_params=_cparams("arbitrary"),
        name="moe_router",
    )(h2, norm_g.astype(F32).reshape(1, d), wh, wl, b_router.astype(F32).reshape(N_EXPERTS, 1), su)


def _slab_rows(r):
    return pl.ds(pl.multiple_of(r * SLAB, SLAB), SLAB)


def _dispatch_kernel(dest_ref, slab_ref, xs_in_ref, xs_ref, sem, *, tm):
    del xs_in_ref

    def row_copy(t, d):
        return pltpu.make_async_copy(slab_ref.at[_slab_rows(t), :], xs_ref.at[_slab_rows(d), :], sem)

    for k in range(TOP_K):
        def issue(t, carry, k=k):
            row_copy(t, dest_ref[0, 0, k * tm + t]).start()
            return carry
        lax.fori_loop(0, tm, issue, 0)

    for k in range(TOP_K):
        def drain(t, carry, k=k):
            row_copy(t, dest_ref[0, 0, k * tm + t]).wait()
            return carry
        lax.fori_loop(0, tm, drain, 0)


def moe_dispatch(dest_tiles, slab, p_rows, tm):
    t = slab.shape[0] // SLAB
    xs0 = jnp.zeros((p_rows * SLAB, LANES), F32)
    return pl.pallas_call(
        functools.partial(_dispatch_kernel, tm=tm),
        grid=(t // tm,),
        in_specs=[
            pl.BlockSpec((1, 1, TOP_K * tm), lambda i: (i, 0, 0), memory_space=pltpu.SMEM),
            pl.BlockSpec((tm * SLAB, LANES), lambda i: (i, 0)),
            pl.BlockSpec(memory_space=pl.ANY),
        ],
        out_specs=pl.BlockSpec(memory_space=pl.ANY),
        out_shape=jax.ShapeDtypeStruct((p_rows * SLAB, LANES), F32),
        scratch_shapes=[pltpu.SemaphoreType.DMA(())],
        input_output_aliases={2: 0},
        compiler_params=_cparams("arbitrary"),
        name="moe_dispatch",
    )(dest_tiles, slab, xs0)


def _expert_kernel(blk_e_ref, xs_ref, wgu_ref, bgu_ref, wd_ref, bd_ref, ys_ref, *, bm):
    del blk_e_ref
    x = jnp.concatenate([xs_ref[pl.ds(s, bm, stride=SLAB), :] for s in range(SLAB)], axis=-1)
    gu = _dot(x.astype(BF16), wgu_ref[...]) + bgu_ref[...]
    gate = jnp.minimum(gu[:, :D_FF], SWIGLU_LIMIT)
    up = jnp.clip(gu[:, D_FF:], -SWIGLU_LIMIT, SWIGLU_LIMIT)
    act = (up + 1.0) * (gate * jax.nn.sigmoid(gate * SWIGLU_ALPHA))
    y = _dot(act.astype(BF16), wd_ref[...]) + bd_ref[...]
    for s in range(SLAB):
        ys_ref[pl.ds(s, bm, stride=SLAB), :] = y[:, s * LANES:(s + 1) * LANES]


def moe_experts(blk_e, xs, w_gate_up_b, b_gate_up, w_down_b, b_down, bm):
    nb = blk_e.shape[0]
    d = D_MODEL
    grid_spec = pltpu.PrefetchScalarGridSpec(
        num_scalar_prefetch=1,
        grid=(nb,),
        in_specs=[
            pl.BlockSpec((bm * SLAB, LANES), lambda i, be: (i, 0)),
            pl.BlockSpec((None, d, 2 * D_FF), lambda i, be: (be[i], 0, 0)),
            pl.BlockSpec((None, 1, 2 * D_FF), lambda i, be: (be[i], 0, 0)),
            pl.BlockSpec((None, D_FF, d), lambda i, be: (be[i], 0, 0)),
            pl.BlockSpec((None, 1, d), lambda i, be: (be[i], 0, 0)),
        ],
        out_specs=pl.BlockSpec((bm * SLAB, LANES), lambda i, be: (i, 0)),
    )
    return pl.pallas_call(
        functools.partial(_expert_kernel, bm=bm),
        grid_spec=grid_spec,
        out_shape=jax.ShapeDtypeStruct((nb * bm * SLAB, LANES), F32),
        compiler_params=_cparams("arbitrary"),
        name="moe_experts",
    )(blk_e, xs, w_gate_up_b, b_gate_up.astype(F32).reshape(N_EXPERTS, 1, 2 * D_FF),
      w_down_b, b_down.astype(F32).reshape(N_EXPERTS, 1, d))


def _combine_kernel(dest_ref, gate_ref, h_ref, ys_ref, o_ref, buf_ref, sem, *, tm):
    def row_copy(j):
        return pltpu.make_async_copy(ys_ref.at[_slab_rows(dest_ref[0, 0, j]), :],
                                     buf_ref.at[_slab_rows(j), :], sem)

    def issue(j, carry):
        row_copy(j).start()
        return carry

    def drain(j, carry):
        row_copy(j).wait()
        return carry

    lax.fori_loop(0, TOP_K * tm, issue, 0)
    lax.fori_loop(0, TOP_K * tm, drain, 0)

    gates = gate_ref[...]
    for s in range(SLAB):
        cols = slice(s * LANES, (s + 1) * LANES)
        acc = h_ref[:, cols]
        for k in range(TOP_K):
            acc = acc + gates[:, k:k + 1] * buf_ref[pl.ds(k * tm * SLAB + s, tm, stride=SLAB), :]
        o_ref[:, cols] = acc


def moe_combine(dest_tiles, gates_col, h2, ys, tm):
    t, d = h2.shape
    return pl.pallas_call(
        functools.partial(_combine_kernel, tm=tm),
        grid=(t // tm,),
        in_specs=[
            pl.BlockSpec((1, 1, TOP_K * tm), lambda i: (i, 0, 0), memory_space=pltpu.SMEM),
            pl.BlockSpec((tm, TOP_K), lambda i: (i, 0)),
            pl.BlockSpec((tm, d), lambda i: (i, 0)),
            pl.BlockSpec(memory_space=pl.ANY),
        ],
        out_specs=pl.BlockSpec((tm, d), lambda i: (i, 0)),
        out_shape=jax.ShapeDtypeStruct((t, d), F32),
        scratch_shapes=[pltpu.VMEM((TOP_K * tm * SLAB, LANES), F32), pltpu.SemaphoreType.DMA(())],
        compiler_params=_cparams("arbitrary"),
        name="moe_combine",
    )(dest_tiles, gates_col, h2, ys)


def _tile_major(a, tm):
    t = a.shape[1]
    return a.reshape(TOP_K, t // tm, tm).transpose(1, 0, 2).reshape(t // tm, 1, TOP_K * tm)


def moe_layer(h, norm_g, w_router, b_router, w_gate_up, b_gate_up, w_down, b_down):
    bsz, seq, d = h.shape
    t = bsz * seq
    h2 = h.reshape(t, d)
    bm = MOE_ROWS_PER_BLOCK
    slab, idx, gates, rank, cnt = moe_router(h2, norm_g, w_router, b_router)

    counts = cnt[:, 0].astype(jnp.int32)
    padded = (counts + bm - 1) // bm * bm
    pend = jnp.cumsum(padded)
    pstart = pend - padded
    dest = pstart[idx] + rank
    n = t * TOP_K
    p_rows = -(-(n + N_EXPERTS * bm) // bm) * bm
    nb = p_rows // bm
    blk_e = jnp.minimum(jnp.searchsorted(pend, jnp.arange(nb, dtype=jnp.int32) * bm, side='right'),
                        N_EXPERTS - 1).astype(jnp.int32)

    td = min(DISPATCH_TILE, t)
    xs = moe_dispatch(_tile_major(dest, td), slab, p_rows, td)
    ys = moe_experts(blk_e, xs, w_gate_up.astype(BF16), b_gate_up, w_down.astype(BF16), b_down, bm)
    tc = min(COMBINE_TILE, t)
    out = moe_combine(_tile_major(dest, tc), gates.T, h2, ys, tc)
    return out.reshape(bsz, seq, d)


def kernel(x, mem, mixer_norm, xattn_norm, mem_norm, ffn_norm, ssd_w_in, ssd_conv_w, ssd_conv_b, ssd_dt_bias, ssd_a_log, ssd_d, ssd_norm, ssd_w_out, da_w_qkv, da_lambda_q1, da_lambda_k1, da_lambda_q2, da_lambda_k2, da_subln, da_w_o, xa_w_q, xa_w_kv, xa_w_o, moe_w_router, moe_b_router, moe_w_gate_up, moe_b_gate_up, moe_w_down, moe_b_down, final_norm):
    depth = mixer_norm.shape[0]
    bsz, seq, d = x.shape
    h = x
    for i in range(depth):
        j = i // N_MIXERS
        if i % N_MIXERS == 0:
            h = ssd_layer(h, mixer_norm[i], ssd_w_in[j], ssd_conv_w[j], ssd_conv_b[j], ssd_dt_bias[j],
                          ssd_a_log[j], ssd_d[j], ssd_norm[j], ssd_w_out[j])
        else:
            h = da_layer(h, mixer_norm[i], da_w_qkv[j], da_lambda_q1[j], da_lambda_k1[j], da_lambda_q2[j],
                         da_lambda_k2[j], da_subln[j], da_w_o[j], i)
        h = xattn_layer(h, mem, xattn_norm[i], mem_norm[i], xa_w_q[i], xa_w_kv[i], xa_w_o[i])
        h = moe_layer(h, ffn_norm[i], moe_w_router[i], moe_b_router[i], moe_w_gate_up[i],
                      moe_b_gate_up[i], moe_w_down[i], moe_b_down[i])
    return final_norm_call(h.reshape(bsz * seq, d), final_norm).reshape(bsz, seq, d)
```

```python
import functools
import math

import jax
import jax.numpy as jnp
from jax import lax
from jax.experimental import pallas as pl
from jax.experimental.pallas import tpu as pltpu

F32 = jnp.float32
BF16 = jnp.bfloat16

D_MODEL = 1024
RMS_EPS = 1e-5
LOG2_E = 1.4426950408889634
N_MIXERS = 2

SSD_D_INNER = 2048
SSD_HEADDIM = 64
SSD_N_HEADS = 32
SSD_N_GROUPS = 4
SSD_HEADS_PER_GROUP = 8
SSD_D_STATE = 128
SSD_D_CONV = 4
SSD_CHUNK = 128
SSD_GN = 512
SSD_CONV_DIM = 3072
SSD_GROUP_WIDTH = SSD_D_INNER // SSD_N_GROUPS

DA_HEAD_DIM = 64
DA_N_HEADS = 8

XA_N_HEADS = 4
XA_HEAD_DIM = 256

N_EXPERTS = 32
TOP_K = 4
D_FF = 1024
SWIGLU_LIMIT = 7.0
SWIGLU_ALPHA = 1.702

LANES = 128
SUBLANES = 8
VMEM_LIMIT_BYTES = 56 * 1024 * 1024

MOE_ROWS_PER_BLOCK = 256
ROUTER_TILE = 512
DISPATCH_TILE = 256
COMBINE_TILE = 256
DMA_ISSUE_UNROLL = 8


def _cparams(*sem):
    return pltpu.CompilerParams(dimension_semantics=sem, vmem_limit_bytes=VMEM_LIMIT_BYTES)


def _rms(x, g):
    ms = jnp.mean(x * x, axis=-1, keepdims=True)
    return x * lax.rsqrt(ms + RMS_EPS) * g


def _dot(a, b):
    return jnp.dot(a, b, preferred_element_type=F32)


def _dot_nt(a, b):
    return lax.dot_general(a, b, (((1,), (1,)), ((), ())), preferred_element_type=F32)


def _split2(x):
    hi = x.astype(BF16)
    lo = (x - hi.astype(F32)).astype(BF16)
    return hi, lo


def _split3(x):
    hi = x.astype(BF16)
    r = x - hi.astype(F32)
    mid = r.astype(BF16)
    lo = (r - mid.astype(F32)).astype(BF16)
    return hi, mid, lo


def _norm_mm_kernel(x_ref, g_ref, w_ref, o_ref, xn_ref):
    @pl.when(pl.program_id(1) == 0)
    def _():
        xn_ref[...] = _rms(x_ref[...], g_ref[...]).astype(BF16)

    o_ref[...] = _dot(xn_ref[...], w_ref[...]).astype(o_ref.dtype)


def norm_matmul(x, g, w, out_dtype, tm, tn):
    m, k = x.shape
    n = w.shape[1]
    tm = min(tm, m)
    tn = min(tn, n)
    return pl.pallas_call(
        _norm_mm_kernel,
        grid=(m // tm, n // tn),
        in_specs=[
            pl.BlockSpec((tm, k), lambda i, j: (i, 0)),
            pl.BlockSpec((1, k), lambda i, j: (0, 0)),
            pl.BlockSpec((k, tn), lambda i, j: (0, j)),
        ],
        out_specs=pl.BlockSpec((tm, tn), lambda i, j: (i, j)),
        out_shape=jax.ShapeDtypeStruct((m, n), out_dtype),
        scratch_shapes=[pltpu.VMEM((tm, k), BF16)],
        compiler_params=_cparams("parallel", "arbitrary"),
        name="norm_matmul",
    )(x, g.reshape(1, k), w)


def _mm_res_kernel(x_ref, w_ref, r_ref, o_ref):
    o_ref[...] = r_ref[...] + _dot(x_ref[...], w_ref[...])


def matmul_residual(x, w, res, tm=512):
    m, k = x.shape
    n = w.shape[1]
    tm = min(tm, m)
    return pl.pallas_call(
        _mm_res_kernel,
        grid=(m // tm,),
        in_specs=[
            pl.BlockSpec((tm, k), lambda i: (i, 0)),
            pl.BlockSpec((k, n), lambda i: (0, 0)),
            pl.BlockSpec((tm, n), lambda i: (i, 0)),
        ],
        out_specs=pl.BlockSpec((tm, n), lambda i: (i, 0)),
        out_shape=jax.ShapeDtypeStruct((m, n), F32),
        compiler_params=_cparams("parallel"),
        name="matmul_residual",
    )(x, w, res)


def _norm_kernel(x_ref, g_ref, o_ref):
    o_ref[...] = _rms(x_ref[...], g_ref[...])


def final_norm_call(x, g, tm=1024):
    m, k = x.shape
    tm = min(tm, m)
    return pl.pallas_call(
        _norm_kernel,
        grid=(m // tm,),
        in_specs=[pl.BlockSpec((tm, k), lambda i: (i, 0)), pl.BlockSpec((1, k), lambda i: (0, 0))],
        out_specs=pl.BlockSpec((tm, k), lambda i: (i, 0)),
        out_shape=jax.ShapeDtypeStruct((m, k), F32),
        compiler_params=_cparams("parallel"),
        name="final_norm",
    )(x, g.reshape(1, k))


def _ssd_kernel(z0_ref, z1_ref, x0_ref, x1_ref, bc_ref, dtr_ref, h_ref,
                convw_ref, convb_ref, dtb_ref, alog_ref, dexp_ref, ng_ref, expand_ref, wout_ref,
                o_ref, state_ref, ext_ref):
    L = SSD_CHUNK
    c = pl.program_id(1)

    @pl.when(c == 0)
    def _():
        state_ref[...] = jnp.zeros_like(state_ref)
        ext_ref[0:SUBLANES, :] = jnp.zeros((SUBLANES, SSD_CONV_DIM), F32)

    pieces = []
    for blk, ref in enumerate((x0_ref, x1_ref, bc_ref)):
        cols = slice(blk * 1024, (blk + 1) * 1024)
        ext_ref[SUBLANES:SUBLANES + L, cols] = ref[...].astype(F32)
        acc = convb_ref[:, cols]
        for k in range(SSD_D_CONV):
            start = SUBLANES - (SSD_D_CONV - 1) + k
            acc = acc + convw_ref[k:k + 1, cols] * ext_ref[start:start + L, cols]
        pieces.append(acc * jax.nn.sigmoid(acc))
        ext_ref[0:SUBLANES, cols] = ext_ref[L:L + SUBLANES, cols]
    xs = jnp.concatenate(pieces[:2], axis=-1)
    b_all = pieces[2][:, :SSD_GN]
    c_all = pieces[2][:, SSD_GN:]

    dtr = dtr_ref[...] + dtb_ref[...]
    dt = jnp.maximum(dtr, 0.0) + jnp.log1p(jnp.exp(-jnp.abs(dtr)))
    a = -jnp.exp(alog_ref[...])
    da = dt * a
    row = lax.broadcasted_iota(jnp.int32, (L, L), 0)
    col = lax.broadcasted_iota(jnp.int32, (L, L), 1)
    causal = col <= row
    tril = jnp.where(causal, 1.0, 0.0).astype(BF16)
    d_hi, d_mid, d_lo = _split3(da)
    a_cum = _dot(tril, d_hi) + _dot(tril, d_mid) + _dot(tril, d_lo)
    a_cum_t = a_cum.T

    expand = expand_ref[...]
    t_hi, t_lo = _split2(dt)
    dt_e = _dot(t_hi, expand) + _dot(t_lo, expand)
    c_hi, c_mid, c_lo = _split3(a_cum)
    acum_e = _dot(c_hi, expand) + _dot(c_mid, expand) + _dot(c_lo, expand)
    alast_e = acum_e[L - 1:L, :]
    exp_acum_e = jnp.exp(acum_e)
    dte_e = jnp.exp(alast_e - acum_e)
    cd_e = jnp.exp(alast_e)

    xd = xs * dt_e
    xd_b = xd.astype(BF16)
    xdw_b = (xd * dte_e).astype(BF16)

    lane = lax.broadcasted_iota(jnp.int32, (L, LANES), 1)
    first_half = lane < SSD_HEADDIM

    y_parts = []
    for g in range(SSD_N_GROUPS):
        gs = slice(g * SSD_D_STATE, (g + 1) * SSD_D_STATE)
        gw = slice(g * SSD_GROUP_WIDTH, (g + 1) * SSD_GROUP_WIDTH)
        b_g = b_all[:, gs]
        c_g = c_all[:, gs].astype(BF16)
        cb = _dot_nt(c_g, b_g.astype(BF16))
        y_pairs = []
        for jp in range(SSD_HEADS_PER_GROUP // 2):
            res = []
            pair_col = g * SSD_GROUP_WIDTH + jp * LANES
            xd_pair = xd_b[:, pair_col:pair_col + LANES]
            for sub in range(2):
                hd = g * SSD_HEADS_PER_GROUP + jp * 2 + sub
                diff = a_cum[:, hd:hd + 1] - a_cum_t[hd:hd + 1, :]
                dec = jnp.exp(jnp.where(causal, diff, -jnp.inf))
                res.append(_dot((cb * dec).astype(BF16), xd_pair))
            y_pairs.append(jnp.where(first_half, res[0], res[1]))
        y_diag = jnp.concatenate(y_pairs, axis=-1)
        st = state_ref[g]
        y_off = _dot(c_g, st.astype(BF16)) * exp_acum_e[:, gw]
        state_ref[g] = st * cd_e[:, gw] + _dot(b_g.T.astype(BF16), xdw_b[:, gw])
        y_parts.append(y_diag + y_off)
    y = jnp.concatenate(y_parts, axis=-1) + dexp_ref[...] * xs

    z = jnp.concatenate([z0_ref[...], z1_ref[...]], axis=-1).astype(F32)
    u = y * (z * jax.nn.sigmoid(z))
    u_parts = []
    for g in range(SSD_N_GROUPS):
        gw = slice(g * SSD_GROUP_WIDTH, (g + 1) * SSD_GROUP_WIDTH)
        ug = u[:, gw]
        ms = jnp.mean(ug * ug, axis=-1, keepdims=True)
        u_parts.append(ug * lax.rsqrt(ms + RMS_EPS) * ng_ref[:, gw])
    un = jnp.concatenate(u_parts, axis=-1).astype(BF16)
    o_ref[...] = h_ref[...] + _dot(un, wout_ref[...])


def ssd_core(zxbc, dt_raw, h, conv_w, conv_b, dt_bias, a_log, d_skip, norm_g, w_out_b):
    bsz, seq, _ = h.shape
    L = SSD_CHUNK
    nc = seq // L
    pad_heads = LANES - SSD_N_HEADS
    dtb = jnp.pad(dt_bias.astype(F32), (0, pad_heads)).reshape(1, LANES)
    alog = jnp.pad(a_log.astype(F32), (0, pad_heads)).reshape(1, LANES)
    dexp = jnp.repeat(d_skip.astype(F32), SSD_HEADDIM).reshape(1, SSD_D_INNER)
    head_of_col = jnp.arange(SSD_D_INNER, dtype=jnp.int32) // SSD_HEADDIM
    expand = (jnp.arange(LANES, dtype=jnp.int32)[:, None] == head_of_col[None, :]).astype(BF16)

    def zx_spec(k):
        return pl.BlockSpec((None, L, 1024), lambda b, c, k=k: (b, c, k))

    def const_spec(shape):
        return pl.BlockSpec(shape, lambda b, c: (0,) * len(shape))

    return pl.pallas_call(
        _ssd_kernel,
        grid=(bsz, nc),
        in_specs=[
            zx_spec(0), zx_spec(1), zx_spec(2), zx_spec(3), zx_spec(4),
            pl.BlockSpec((None, L, LANES), lambda b, c: (b, c, 0)),
            pl.BlockSpec((None, L, D_MODEL), lambda b, c: (b, c, 0)),
            const_spec((SSD_D_CONV, SSD_CONV_DIM)),
            const_spec((1, SSD_CONV_DIM)),
            const_spec((1, LANES)),
            const_spec((1, LANES)),
            const_spec((1, SSD_D_INNER)),
            const_spec((1, SSD_D_INNER)),
            const_spec((LANES, SSD_D_INNER)),
            const_spec((SSD_D_INNER, D_MODEL)),
        ],
        out_specs=pl.BlockSpec((None, L, D_MODEL), lambda b, c: (b, c, 0)),
        out_shape=jax.ShapeDtypeStruct((bsz, seq, D_MODEL), F32),
        scratch_shapes=[
            pltpu.VMEM((SSD_N_GROUPS, SSD_D_STATE, SSD_GROUP_WIDTH), F32),
            pltpu.VMEM((SUBLANES + L, SSD_CONV_DIM), F32),
        ],
        compiler_params=_cparams("parallel", "arbitrary"),
        name="ssd_core",
    )(zxbc, zxbc, zxbc, zxbc, zxbc, dt_raw, h,
      conv_w.astype(F32), conv_b.astype(F32).reshape(1, SSD_CONV_DIM), dtb, alog, dexp,
      norm_g.astype(F32).reshape(1, SSD_D_INNER), expand, w_out_b)


def ssd_layer(h, norm_g_in, w_in, conv_w, conv_b, dt_bias, a_log, d_skip, norm_g, w_out):
    bsz, seq, d = h.shape
    h2 = h.reshape(bsz * seq, d)
    n_main = SSD_D_INNER + SSD_CONV_DIM
    w_main = w_in[:, :n_main].astype(BF16)
    w_dt = jnp.pad(w_in[:, n_main:], ((0, 0), (0, LANES - SSD_N_HEADS))).astype(BF16)
    zxbc = norm_matmul(h2, norm_g_in, w_main, BF16, tm=1024, tn=1024)
    dt_raw = norm_matmul(h2, norm_g_in, w_dt, F32, tm=1024, tn=LANES)
    return ssd_core(zxbc.reshape(bsz, seq, n_main), dt_raw.reshape(bsz, seq, LANES), h,
                    conv_w, conv_b, dt_bias, a_log, d_skip, norm_g, w_out.astype(BF16))


def _da_kernel(lq1_ref, lk1_ref, lq2_ref, lk2_ref, sub_ref, q_ref, k_ref, v_ref, o_ref,
               acc1_ref, acc2_ref, m1_ref, m2_ref, *, tq, lambda_init):
    i = pl.program_id(2)
    q = q_ref[...]
    lane = lax.broadcasted_iota(jnp.int32, (tq, LANES), 1)
    qs = (q.astype(F32) * (DA_HEAD_DIM ** -0.5 * LOG2_E)).astype(BF16)
    zero = jnp.zeros_like(qs)
    q_maps = (jnp.where(lane < DA_HEAD_DIM, qs, zero), jnp.where(lane >= DA_HEAD_DIM, qs, zero))
    states = ((m1_ref, acc1_ref), (m2_ref, acc2_ref))

    for m_ref, acc_ref in states:
        m_ref[...] = jnp.full((tq, LANES), -jnp.inf, F32)
        acc_ref[...] = jnp.zeros((tq, 2 * LANES), F32)

    ones = jnp.ones((tq, LANES), BF16)

    def step(j, masked):
        start = pl.multiple_of(j * tq, tq)
        kt = k_ref[pl.ds(start, tq), :]
        v_aug = jnp.concatenate([v_ref[pl.ds(start, tq), :], ones], axis=1)
        for qm, (m_ref, acc_ref) in zip(q_maps, states):
            s = _dot_nt(qm, kt)
            if masked:
                r = lax.broadcasted_iota(jnp.int32, (tq, tq), 0)
                cidx = lax.broadcasted_iota(jnp.int32, (tq, tq), 1)
                s = jnp.where(cidx <= r, s, -jnp.inf)
            m_old = m_ref[...]
            m_new = jnp.maximum(m_old, jnp.max(s, axis=-1, keepdims=True))
            alpha = jnp.exp2(m_old - m_new)
            p = jnp.exp2(s - jnp.concatenate([m_new] * (tq // LANES), axis=1))
            acc_ref[...] = (jnp.concatenate([alpha, alpha], axis=1) * acc_ref[...]
                            + _dot(p.astype(BF16), v_aug))
            m_ref[...] = m_new

    def body(j, carry):
        step(j, False)
        return carry

    lax.fori_loop(0, i, body, 0)
    step(i, True)

    lam = (jnp.exp(jnp.sum(lq1_ref[...] * lk1_ref[...], axis=-1, keepdims=True))
           - jnp.exp(jnp.sum(lq2_ref[...] * lk2_ref[...], axis=-1, keepdims=True)) + lambda_init)
    o1 = acc1_ref[:, :LANES] / acc1_ref[:, LANES:]
    o2 = acc2_ref[:, :LANES] / acc2_ref[:, LANES:]
    o = _rms(o1 - lam * o2, sub_ref[...]) * (1.0 - lambda_init)
    o_ref[...] = o.astype(o_ref.dtype)


def diff_attention_core(qkv, lq1, lk1, lq2, lk2, subln_g, layer_idx, tq=512):
    bsz, seq, _ = qkv.shape
    tq = min(tq, seq)
    lambda_init = 0.8 - 0.6 * math.exp(-0.3 * layer_idx)
    nh = DA_N_HEADS

    def vec_spec(n):
        return pl.BlockSpec((1, n), lambda b, h, i: (0, 0))

    return pl.pallas_call(
        functools.partial(_da_kernel, tq=tq, lambda_init=lambda_init),
        grid=(bsz, nh, seq // tq),
        in_specs=[
            vec_spec(DA_HEAD_DIM), vec_spec(DA_HEAD_DIM), vec_spec(DA_HEAD_DIM), vec_spec(DA_HEAD_DIM),
            vec_spec(LANES),
            pl.BlockSpec((None, tq, LANES), lambda b, h, i: (b, i, h)),
            pl.BlockSpec((None, seq, LANES), lambda b, h, i: (b, 0, nh + h)),
            pl.BlockSpec((None, seq, LANES), lambda b, h, i: (b, 0, 2 * nh + h)),
        ],
        out_specs=pl.BlockSpec((None, tq, LANES), lambda b, h, i: (b, i, h)),
        out_shape=jax.ShapeDtypeStruct((bsz, seq, D_MODEL), BF16),
        scratch_shapes=[
            pltpu.VMEM((tq, 2 * LANES), F32), pltpu.VMEM((tq, 2 * LANES), F32),
            pltpu.VMEM((tq, LANES), F32), pltpu.VMEM((tq, LANES), F32),
        ],
        compiler_params=_cparams("parallel", "parallel", "arbitrary"),
        name="diff_attention",
    )(lq1.astype(F32).reshape(1, -1), lk1.astype(F32).reshape(1, -1),
      lq2.astype(F32).reshape(1, -1), lk2.astype(F32).reshape(1, -1),
      subln_g.astype(F32).reshape(1, -1), qkv, qkv, qkv)


def da_layer(h, norm_g_in, w_qkv, lq1, lk1, lq2, lk2, subln_g, w_o, layer_idx):
    bsz, seq, d = h.shape
    h2 = h.reshape(bsz * seq, d)
    qkv = norm_matmul(h2, norm_g_in, w_qkv.astype(BF16), BF16, tm=1024, tn=1024)
    o = diff_attention_core(qkv.reshape(bsz, seq, 3 * d), lq1, lk1, lq2, lk2, subln_g, layer_idx)
    return matmul_residual(o.reshape(bsz * seq, d), w_o.astype(BF16), h2).reshape(bsz, seq, d)


def _xattn_kernel(h_ref, g_ref, wq_ref, kv_ref, wo_ref, o_ref):
    h = h_ref[...]
    hn = _rms(h, g_ref[...]).astype(BF16)
    scale = XA_HEAD_DIM ** -0.5
    q = (_dot(hn, wq_ref[...]) * scale).astype(BF16)
    outs = []
    for hd in range(XA_N_HEADS):
        cs = slice(hd * XA_HEAD_DIM, (hd + 1) * XA_HEAD_DIM)
        vs = slice(D_MODEL + hd * XA_HEAD_DIM, D_MODEL + (hd + 1) * XA_HEAD_DIM)
        s = _dot_nt(q[:, cs], kv_ref[:, cs])
        m = jnp.max(s, axis=-1, keepdims=True)
        p = jnp.exp(s - m)
        l = jnp.sum(p, axis=-1, keepdims=True)
        outs.append((_dot(p.astype(BF16), kv_ref[:, vs]) / l).astype(BF16))
    o = jnp.concatenate(outs, axis=-1)
    o_ref[...] = h + _dot(o, wo_ref[...])


def xattn_layer(h, mem, norm_g, mem_norm_g, w_q, w_kv, w_o, tq=512):
    bsz, seq, d = h.shape
    mlen = mem.shape[1]
    tq = min(tq, seq)
    kv = norm_matmul(mem.reshape(bsz * mlen, d), mem_norm_g, w_kv.astype(BF16), BF16, tm=512, tn=1024)
    kv = kv.reshape(bsz, mlen, 2 * d)
    return pl.pallas_call(
        _xattn_kernel,
        grid=(bsz, seq // tq),
        in_specs=[
            pl.BlockSpec((None, tq, d), lambda b, i: (b, i, 0)),
            pl.BlockSpec((1, d), lambda b, i: (0, 0)),
            pl.BlockSpec((d, d), lambda b, i: (0, 0)),
            pl.BlockSpec((None, mlen, 2 * d), lambda b, i: (b, 0, 0)),
            pl.BlockSpec((d, d), lambda b, i: (0, 0)),
        ],
        out_specs=pl.BlockSpec((None, tq, d), lambda b, i: (b, i, 0)),
        out_shape=jax.ShapeDtypeStruct((bsz, seq, d), F32),
        compiler_params=_cparams("parallel", "parallel"),
        name="mem_xattn",
    )(h, norm_g.astype(F32).reshape(1, d), w_q.astype(BF16), kv, w_o.astype(BF16))


SLAB = D_MODEL // LANES


def _router_kernel(h_ref, g_ref, wh_ref, wl_ref, br_ref, su_ref,
                   slab_ref, idx_ref, gate_ref, rank_ref, cnt_ref, run_ref, *, tm):
    i = pl.program_id(0)

    @pl.when(i == 0)
    def _():
        run_ref[...] = jnp.zeros_like(run_ref)

    hn = _rms(h_ref[...], g_ref[...])
    for s in range(SLAB):
        slab_ref[pl.ds(s, tm, stride=SLAB), :] = hn[:, s * LANES:(s + 1) * LANES]

    x_hi, x_lo = _split2(hn)
    wh = wh_ref[...]
    logits = _dot_nt(wh, x_hi) + _dot_nt(wh, x_lo) + _dot_nt(wl_ref[...], x_hi) + br_ref[...]

    rows = lax.broadcasted_iota(jnp.int32, (N_EXPERTS, tm), 0).astype(F32)
    tops, idxs, onehots = [], [], []
    cur = logits
    for _ in range(TOP_K):
        m = jnp.max(cur, axis=0, keepdims=True)
        idx = jnp.min(jnp.where(cur == m, rows, float(N_EXPERTS)), axis=0, keepdims=True)
        oh = rows == idx
        cur = jnp.where(oh, -jnp.inf, cur)
        tops.append(m)
        idxs.append(idx)
        onehots.append(oh)
    exps = [jnp.exp(t - tops[0]) for t in tops]
    denom = exps[0] + exps[1] + exps[2] + exps[3]
    gate_ref[...] = jnp.concatenate([e / denom for e in exps], axis=0)
    idx_ref[...] = jnp.concatenate(idxs, axis=0).astype(jnp.int32)

    oh_sum = jnp.zeros((N_EXPERTS, tm), F32)
    for oh in onehots:
        oh_sum = oh_sum + jnp.where(oh, 1.0, 0.0)
    run = run_ref[...]
    prefix = _dot(oh_sum.astype(BF16), su_ref[...]) + run[:, 0:1]
    ranks = [jnp.sum(jnp.where(oh, prefix, 0.0), axis=0, keepdims=True) for oh in onehots]
    rank_ref[...] = jnp.concatenate(ranks, axis=0).astype(jnp.int32)
    run_new = run + jnp.sum(oh_sum, axis=1, keepdims=True)
    run_ref[...] = run_new
    cnt_ref[...] = run_new


def moe_router(h2, norm_g, w_router, b_router, tm=ROUTER_TILE):
    t, d = h2.shape
    tm = min(tm, t)
    wt = w_router.astype(F32).T
    wh = wt.astype(BF16)
    wl = (wt - wh.astype(F32)).astype(BF16)
    su = (jnp.arange(tm)[:, None] < jnp.arange(tm)[None, :]).astype(BF16)
    return pl.pallas_call(
        functools.partial(_router_kernel, tm=tm),
        grid=(t // tm,),
        in_specs=[
            pl.BlockSpec((tm, d), lambda i: (i, 0)),
            pl.BlockSpec((1, d), lambda i: (0, 0)),
            pl.BlockSpec((N_EXPERTS, d), lambda i: (0, 0)),
            pl.BlockSpec((N_EXPERTS, d), lambda i: (0, 0)),
            pl.BlockSpec((N_EXPERTS, 1), lambda i: (0, 0)),
            pl.BlockSpec((tm, tm), lambda i: (0, 0)),
        ],
        out_specs=[
            pl.BlockSpec((tm * SLAB, LANES), lambda i: (i, 0)),
            pl.BlockSpec((TOP_K, tm), lambda i: (0, i)),
            pl.BlockSpec((TOP_K, tm), lambda i: (0, i)),
            pl.BlockSpec((TOP_K, tm), lambda i: (0, i)),
            pl.BlockSpec((N_EXPERTS, LANES), lambda i: (0, 0)),
        ],
        out_shape=[
            jax.ShapeDtypeStruct((t * SLAB, LANES), F32),
            jax.ShapeDtypeStruct((TOP_K, t), jnp.int32),
            jax.ShapeDtypeStruct((TOP_K, t), F32),
            jax.ShapeDtypeStruct((TOP_K, t), jnp.int32),
            jax.ShapeDtypeStruct((N_EXPERTS, LANES), F32),
        ],
        scratch_shapes=[pltpu.VMEM((N_EXPERTS, LANES), F32)],
        compiler_params=_cparams("arbitrary"),
        name="moe_router",
    )(h2, norm_g.astype(F32).reshape(1, d), wh, wl, b_router.astype(F32).reshape(N_EXPERTS, 1), su)


def _slab_rows(r):
    return pl.ds(pl.multiple_of(r * SLAB, SLAB), SLAB)


def _dispatch_kernel(dest_ref, slab_ref, xs_in_ref, xs_ref, sem, *, tm):
    del xs_in_ref

    def row_copy(t, d):
        return pltpu.make_async_copy(slab_ref.at[_slab_rows(t), :], xs_ref.at[_slab_rows(d), :], sem)

    def issue(g, carry):
        for u in range(DMA_ISSUE_UNROLL):
            t = g * DMA_ISSUE_UNROLL + u
            for k in range(TOP_K):
                row_copy(t, dest_ref[0, 0, k * tm + t]).start()
        return carry

    lax.fori_loop(0, tm // DMA_ISSUE_UNROLL, issue, 0)

    for k in range(TOP_K):
        pltpu.make_async_copy(slab_ref, xs_ref.at[pl.ds(0, tm * SLAB), :], sem).wait()


def moe_dispatch(dest_tiles, slab, p_rows, tm):
    t = slab.shape[0] // SLAB
    xs0 = jnp.zeros((p_rows * SLAB, LANES), F32)
    return pl.pallas_call(
        functools.partial(_dispatch_kernel, tm=tm),
        grid=(t // tm,),
        in_specs=[
            pl.BlockSpec((1, 1, TOP_K * tm), lambda i: (i, 0, 0), memory_space=pltpu.SMEM),
            pl.BlockSpec((tm * SLAB, LANES), lambda i: (i, 0)),
            pl.BlockSpec(memory_space=pl.ANY),
        ],
        out_specs=pl.BlockSpec(memory_space=pl.ANY),
        out_shape=jax.ShapeDtypeStruct((p_rows * SLAB, LANES), F32),
        scratch_shapes=[pltpu.SemaphoreType.DMA(())],
        input_output_aliases={2: 0},
        compiler_params=_cparams("arbitrary"),
        name="moe_dispatch",
    )(dest_tiles, slab, xs0)


def _expert_kernel(blk_e_ref, xs_ref, wgu_ref, bgu_ref, wd_ref, bd_ref, ys_ref, wgu_b_ref, wd_b_ref, *, bm):
    i = pl.program_id(0)
    prev_e = blk_e_ref[jnp.maximum(i - 1, 0)]

    @pl.when((i == 0) | (blk_e_ref[i] != prev_e))
    def _():
        wgu_b_ref[...] = wgu_ref[...].astype(BF16)
        wd_b_ref[...] = wd_ref[...].astype(BF16)

    x = jnp.concatenate([xs_ref[pl.ds(s, bm, stride=SLAB), :] for s in range(SLAB)], axis=-1)
    gu = _dot(x.astype(BF16), wgu_b_ref[...]) + bgu_ref[...]
    gate = jnp.minimum(gu[:, :D_FF], SWIGLU_LIMIT)
    up = jnp.clip(gu[:, D_FF:], -SWIGLU_LIMIT, SWIGLU_LIMIT)
    act = (up + 1.0) * (gate * jax.nn.sigmoid(gate * SWIGLU_ALPHA))
    y = _dot(act.astype(BF16), wd_b_ref[...]) + bd_ref[...]
    for s in range(SLAB):
        ys_ref[pl.ds(s, bm, stride=SLAB), :] = y[:, s * LANES:(s + 1) * LANES]


def moe_experts(blk_e, xs, w_gate_up, b_gate_up, w_down, b_down, bm):
    nb = blk_e.shape[0]
    d = D_MODEL
    grid_spec = pltpu.PrefetchScalarGridSpec(
        num_scalar_prefetch=1,
        grid=(nb,),
        in_specs=[
            pl.BlockSpec((bm * SLAB, LANES), lambda i, be: (i, 0)),
            pl.BlockSpec((None, d, 2 * D_FF), lambda i, be: (be[i], 0, 0)),
            pl.BlockSpec((None, 1, 2 * D_FF), lambda i, be: (be[i], 0, 0)),
            pl.BlockSpec((None, D_FF, d), lambda i, be: (be[i], 0, 0)),
            pl.BlockSpec((None, 1, d), lambda i, be: (be[i], 0, 0)),
        ],
        out_specs=pl.BlockSpec((bm * SLAB, LANES), lambda i, be: (i, 0)),
        scratch_shapes=[pltpu.VMEM((d, 2 * D_FF), BF16), pltpu.VMEM((D_FF, d), BF16)],
    )
    return pl.pallas_call(
        functools.partial(_expert_kernel, bm=bm),
        grid_spec=grid_spec,
        out_shape=jax.ShapeDtypeStruct((nb * bm * SLAB, LANES), F32),
        compiler_params=_cparams("arbitrary"),
        name="moe_experts",
    )(blk_e, xs, w_gate_up, b_gate_up.astype(F32).reshape(N_EXPERTS, 1, 2 * D_FF),
      w_down, b_down.astype(F32).reshape(N_EXPERTS, 1, d))


def _combine_kernel(dest_ref, gate_ref, h_ref, ys_ref, o_ref, buf_ref, sem, *, tm):
    def row_copy(j):
        return pltpu.make_async_copy(ys_ref.at[_slab_rows(dest_ref[0, 0, j]), :],
                                     buf_ref.at[_slab_rows(j), :], sem)

    def issue(g, carry):
        for u in range(DMA_ISSUE_UNROLL):
            row_copy(g * DMA_ISSUE_UNROLL + u).start()
        return carry

    lax.fori_loop(0, TOP_K * tm // DMA_ISSUE_UNROLL, issue, 0)
    pltpu.make_async_copy(ys_ref.at[pl.ds(0, TOP_K * tm * SLAB), :], buf_ref, sem).wait()

    gates = gate_ref[...]
    for s in range(SLAB):
        cols = slice(s * LANES, (s + 1) * LANES)
        acc = h_ref[:, cols]
        for k in range(TOP_K):
            acc = acc + gates[:, k:k + 1] * buf_ref[pl.ds(k * tm * SLAB + s, tm, stride=SLAB), :]
        o_ref[:, cols] = acc


def moe_combine(dest_tiles, gates_col, h2, ys, tm):
    t, d = h2.shape
    return pl.pallas_call(
        functools.partial(_combine_kernel, tm=tm),
        grid=(t // tm,),
        in_specs=[
            pl.BlockSpec((1, 1, TOP_K * tm), lambda i: (i, 0, 0), memory_space=pltpu.SMEM),
            pl.BlockSpec((tm, TOP_K), lambda i: (i, 0)),
            pl.BlockSpec((tm, d), lambda i: (i, 0)),
            pl.BlockSpec(memory_space=pl.ANY),
        ],
        out_specs=pl.BlockSpec((tm, d), lambda i: (i, 0)),
        out_shape=jax.ShapeDtypeStruct((t, d), F32),
        scratch_shapes=[pltpu.VMEM((TOP_K * tm * SLAB, LANES), F32), pltpu.SemaphoreType.DMA(())],
        compiler_params=_cparams("arbitrary"),
        name="moe_combine",
    )(dest_tiles, gates_col, h2, ys)


def _tile_major(a, tm):
    t = a.shape[1]
    return a.reshape(TOP_K, t // tm, tm).transpose(1, 0, 2).reshape(t // tm, 1, TOP_K * tm)


def moe_layer(h, norm_g, w_router, b_router, w_gate_up, b_gate_up, w_down, b_down):
    bsz, seq, d = h.shape
    t = bsz * seq
    h2 = h.reshape(t, d)
    bm = MOE_ROWS_PER_BLOCK
    slab, idx, gates, rank, cnt = moe_router(h2, norm_g, w_router, b_router)

    counts = cnt[:, 0].astype(jnp.int32)
    padded = (counts + bm - 1) // bm * bm
    pend = jnp.cumsum(padded)
    pstart = pend - padded
    experts = jnp.arange(N_EXPERTS, dtype=jnp.int32)
    dest = rank + jnp.sum(jnp.where(idx[..., None] == experts, pstart, 0), axis=-1)
    n = t * TOP_K
    p_rows = -(-(n + N_EXPERTS * bm) // bm) * bm
    nb = p_rows // bm
    blk_start = jnp.arange(nb, dtype=jnp.int32) * bm
    blk_e = jnp.minimum(jnp.sum((pend[None, :] <= blk_start[:, None]).astype(jnp.int32), axis=1),
                        N_EXPERTS - 1)

    td = min(DISPATCH_TILE, t)
    xs = moe_dispatch(_tile_major(dest, td), slab, p_rows, td)
    ys = moe_experts(blk_e, xs, w_gate_up.astype(F32), b_gate_up, w_down.astype(F32), b_down, bm)
    tc = min(COMBINE_TILE, t)
    out = moe_combine(_tile_major(dest, tc), gates.T, h2, ys, tc)
    return out.reshape(bsz, seq, d)


def kernel(x, mem, mixer_norm, xattn_norm, mem_norm, ffn_norm, ssd_w_in, ssd_conv_w, ssd_conv_b, ssd_dt_bias, ssd_a_log, ssd_d, ssd_norm, ssd_w_out, da_w_qkv, da_lambda_q1, da_lambda_k1, da_lambda_q2, da_lambda_k2, da_subln, da_w_o, xa_w_q, xa_w_kv, xa_w_o, moe_w_router, moe_b_router, moe_w_gate_up, moe_b_gate_up, moe_w_down, moe_b_down, final_norm):
    depth = mixer_norm.shape[0]
    bsz, seq, d = x.shape
    h = x
    for i in range(depth):
        j = i // N_MIXERS
        if i % N_MIXERS == 0:
            h = ssd_layer(h, mixer_norm[i], ssd_w_in[j], ssd_conv_w[j], ssd_conv_b[j], ssd_dt_bias[j],
                          ssd_a_log[j], ssd_d[j], ssd_norm[j], ssd_w_out[j])
        else:
            h = da_layer(h, mixer_norm[i], da_w_qkv[j], da_lambda_q1[j], da_lambda_k1[j], da_lambda_q2[j],
                         da_lambda_k2[j], da_subln[j], da_w_o[j], i)
        h = xattn_layer(h, mem, xattn_norm[i], mem_norm[i], xa_w_q[i], xa_w_kv[i], xa_w_o[i])
        h = moe_layer(h, ffn_norm[i], moe_w_router[i], moe_b_router[i], moe_w_gate_up[i],
                      moe_b_gate_up[i], moe_w_down[i], moe_b_down[i])
    return final_norm_call(h.reshape(bsz * seq, d), final_norm).reshape(bsz, seq, d)
```

```python
import functools
import math

import jax
import jax.numpy as jnp
from jax import lax
from jax.experimental import pallas as pl
from jax.experimental.pallas import tpu as pltpu

F32 = jnp.float32
BF16 = jnp.bfloat16

D_MODEL = 1024
RMS_EPS = 1e-5
LOG2_E = 1.4426950408889634
N_MIXERS = 2

SSD_D_INNER = 2048
SSD_HEADDIM = 64
SSD_N_HEADS = 32
SSD_N_GROUPS = 4
SSD_HEADS_PER_GROUP = 8
SSD_D_STATE = 128
SSD_D_CONV = 4
SSD_CHUNK = 128
SSD_GN = 512
SSD_CONV_DIM = 3072
SSD_GROUP_WIDTH = SSD_D_INNER // SSD_N_GROUPS

DA_HEAD_DIM = 64
DA_N_HEADS = 8

XA_N_HEADS = 4
XA_HEAD_DIM = 256

N_EXPERTS = 32
TOP_K = 4
D_FF = 1024
SWIGLU_LIMIT = 7.0
SWIGLU_ALPHA = 1.702

LANES = 128
SUBLANES = 8
VMEM_LIMIT_BYTES = 56 * 1024 * 1024

MOE_ROWS_PER_BLOCK = 512
ROUTER_TILE = 512
DISPATCH_TILE = 256
COMBINE_TILE = 256
DMA_ISSUE_UNROLL = 8


def _cparams(*sem):
    return pltpu.CompilerParams(dimension_semantics=sem, vmem_limit_bytes=VMEM_LIMIT_BYTES)


def _rms(x, g):
    ms = jnp.mean(x * x, axis=-1, keepdims=True)
    return x * lax.rsqrt(ms + RMS_EPS) * g


def _dot(a, b):
    return jnp.dot(a, b, preferred_element_type=F32)


def _dot_nt(a, b):
    return lax.dot_general(a, b, (((1,), (1,)), ((), ())), preferred_element_type=F32)


def _split2(x):
    hi = x.astype(BF16)
    lo = (x - hi.astype(F32)).astype(BF16)
    return hi, lo


def _split3(x):
    hi = x.astype(BF16)
    r = x - hi.astype(F32)
    mid = r.astype(BF16)
    lo = (r - mid.astype(F32)).astype(BF16)
    return hi, mid, lo


def _norm_mm_kernel(x_ref, g_ref, w_ref, o_ref, xn_ref):
    @pl.when(pl.program_id(1) == 0)
    def _():
        xn_ref[...] = _rms(x_ref[...], g_ref[...]).astype(BF16)

    o_ref[...] = _dot(xn_ref[...], w_ref[...]).astype(o_ref.dtype)


def norm_matmul(x, g, w, out_dtype, tm, tn):
    m, k = x.shape
    n = w.shape[1]
    tm = min(tm, m)
    tn = min(tn, n)
    return pl.pallas_call(
        _norm_mm_kernel,
        grid=(m // tm, n // tn),
        in_specs=[
            pl.BlockSpec((tm, k), lambda i, j: (i, 0)),
            pl.BlockSpec((1, k), lambda i, j: (0, 0)),
            pl.BlockSpec((k, tn), lambda i, j: (0, j)),
        ],
        out_specs=pl.BlockSpec((tm, tn), lambda i, j: (i, j)),
        out_shape=jax.ShapeDtypeStruct((m, n), out_dtype),
        scratch_shapes=[pltpu.VMEM((tm, k), BF16)],
        compiler_params=_cparams("parallel", "arbitrary"),
        name="norm_matmul",
    )(x, g.reshape(1, k), w)


def _mm_res_kernel(x_ref, w_ref, r_ref, o_ref):
    o_ref[...] = r_ref[...] + _dot(x_ref[...], w_ref[...])


def matmul_residual(x, w, res, tm=512):
    m, k = x.shape
    n = w.shape[1]
    tm = min(tm, m)
    return pl.pallas_call(
        _mm_res_kernel,
        grid=(m // tm,),
        in_specs=[
            pl.BlockSpec((tm, k), lambda i: (i, 0)),
            pl.BlockSpec((k, n), lambda i: (0, 0)),
            pl.BlockSpec((tm, n), lambda i: (i, 0)),
        ],
        out_specs=pl.BlockSpec((tm, n), lambda i: (i, 0)),
        out_shape=jax.ShapeDtypeStruct((m, n), F32),
        compiler_params=_cparams("parallel"),
        name="matmul_residual",
    )(x, w, res)


def _norm_kernel(x_ref, g_ref, o_ref):
    o_ref[...] = _rms(x_ref[...], g_ref[...])


def final_norm_call(x, g, tm=1024):
    m, k = x.shape
    tm = min(tm, m)
    return pl.pallas_call(
        _norm_kernel,
        grid=(m // tm,),
        in_specs=[pl.BlockSpec((tm, k), lambda i: (i, 0)), pl.BlockSpec((1, k), lambda i: (0, 0))],
        out_specs=pl.BlockSpec((tm, k), lambda i: (i, 0)),
        out_shape=jax.ShapeDtypeStruct((m, k), F32),
        compiler_params=_cparams("parallel"),
        name="final_norm",
    )(x, g.reshape(1, k))


def _ssd_kernel(z0_ref, z1_ref, x0_ref, x1_ref, bc_ref, dtr_ref, h_ref,
                convw_ref, convb_ref, dtb_ref, alog_ref, dexp_ref, ng_ref, expand_ref, wout_ref,
                o_ref, state_ref, ext_ref):
    L = SSD_CHUNK
    c = pl.program_id(1)

    @pl.when(c == 0)
    def _():
        state_ref[...] = jnp.zeros_like(state_ref)
        ext_ref[0:SUBLANES, :] = jnp.zeros((SUBLANES, SSD_CONV_DIM), F32)

    pieces = []
    for blk, ref in enumerate((x0_ref, x1_ref, bc_ref)):
        cols = slice(blk * 1024, (blk + 1) * 1024)
        ext_ref[SUBLANES:SUBLANES + L, cols] = ref[...].astype(F32)
        acc = convb_ref[:, cols]
        for k in range(SSD_D_CONV):
            start = SUBLANES - (SSD_D_CONV - 1) + k
            acc = acc + convw_ref[k:k + 1, cols] * ext_ref[start:start + L, cols]
        pieces.append(acc * jax.nn.sigmoid(acc))
        ext_ref[0:SUBLANES, cols] = ext_ref[L:L + SUBLANES, cols]
    xs = jnp.concatenate(pieces[:2], axis=-1)
    b_all = pieces[2][:, :SSD_GN]
    c_all = pieces[2][:, SSD_GN:]

    dtr = dtr_ref[...] + dtb_ref[...]
    dt = jnp.maximum(dtr, 0.0) + jnp.log1p(jnp.exp(-jnp.abs(dtr)))
    a = -jnp.exp(alog_ref[...])
    da = dt * a
    row = lax.broadcasted_iota(jnp.int32, (L, L), 0)
    col = lax.broadcasted_iota(jnp.int32, (L, L), 1)
    causal = col <= row
    tril = jnp.where(causal, 1.0, 0.0).astype(BF16)
    d_hi, d_mid, d_lo = _split3(da)
    a_cum = _dot(tril, d_hi) + _dot(tril, d_mid) + _dot(tril, d_lo)
    a_cum_t = a_cum.T

    expand = expand_ref[...]
    t_hi, t_lo = _split2(dt)
    dt_e = _dot(t_hi, expand) + _dot(t_lo, expand)
    c_hi, c_mid, c_lo = _split3(a_cum)
    acum_e = _dot(c_hi, expand) + _dot(c_mid, expand) + _dot(c_lo, expand)
    alast_e = acum_e[L - 1:L, :]
    exp_acum_e = jnp.exp(acum_e)
    dte_e = jnp.exp(alast_e - acum_e)
    cd_e = jnp.exp(alast_e)

    xd = xs * dt_e
    xd_b = xd.astype(BF16)
    xdw_b = (xd * dte_e).astype(BF16)

    lane = lax.broadcasted_iota(jnp.int32, (L, LANES), 1)
    first_half = lane < SSD_HEADDIM

    y_parts = []
    for g in range(SSD_N_GROUPS):
        gs = slice(g * SSD_D_STATE, (g + 1) * SSD_D_STATE)
        gw = slice(g * SSD_GROUP_WIDTH, (g + 1) * SSD_GROUP_WIDTH)
        b_g = b_all[:, gs]
        c_g = c_all[:, gs].astype(BF16)
        cb = _dot_nt(c_g, b_g.astype(BF16))
        y_pairs = []
        for jp in range(SSD_HEADS_PER_GROUP // 2):
            res = []
            pair_col = g * SSD_GROUP_WIDTH + jp * LANES
            xd_pair = xd_b[:, pair_col:pair_col + LANES]
            for sub in range(2):
                hd = g * SSD_HEADS_PER_GROUP + jp * 2 + sub
                diff = a_cum[:, hd:hd + 1] - a_cum_t[hd:hd + 1, :]
                dec = jnp.exp(jnp.where(causal, diff, -jnp.inf))
                res.append(_dot((cb * dec).astype(BF16), xd_pair))
            y_pairs.append(jnp.where(first_half, res[0], res[1]))
        y_diag = jnp.concatenate(y_pairs, axis=-1)
        st = state_ref[g]
        y_off = _dot(c_g, st.astype(BF16)) * exp_acum_e[:, gw]
        state_ref[g] = st * cd_e[:, gw] + _dot(b_g.T.astype(BF16), xdw_b[:, gw])
        y_parts.append(y_diag + y_off)
    y = jnp.concatenate(y_parts, axis=-1) + dexp_ref[...] * xs

    z = jnp.concatenate([z0_ref[...], z1_ref[...]], axis=-1).astype(F32)
    u = y * (z * jax.nn.sigmoid(z))
    u_parts = []
    for g in range(SSD_N_GROUPS):
        gw = slice(g * SSD_GROUP_WIDTH, (g + 1) * SSD_GROUP_WIDTH)
        ug = u[:, gw]
        ms = jnp.mean(ug * ug, axis=-1, keepdims=True)
        u_parts.append(ug * lax.rsqrt(ms + RMS_EPS) * ng_ref[:, gw])
    un = jnp.concatenate(u_parts, axis=-1).astype(BF16)
    o_ref[...] = h_ref[...] + _dot(un, wout_ref[...])


def ssd_core(zxbc, dt_raw, h, conv_w, conv_b, dt_bias, a_log, d_skip, norm_g, w_out_b):
    bsz, seq, _ = h.shape
    L = SSD_CHUNK
    nc = seq // L
    pad_heads = LANES - SSD_N_HEADS
    dtb = jnp.pad(dt_bias.astype(F32), (0, pad_heads)).reshape(1, LANES)
    alog = jnp.pad(a_log.astype(F32), (0, pad_heads)).reshape(1, LANES)
    dexp = jnp.repeat(d_skip.astype(F32), SSD_HEADDIM).reshape(1, SSD_D_INNER)
    head_of_col = jnp.arange(SSD_D_INNER, dtype=jnp.int32) // SSD_HEADDIM
    expand = (jnp.arange(LANES, dtype=jnp.int32)[:, None] == head_of_col[None, :]).astype(BF16)

    def zx_spec(k):
        return pl.BlockSpec((None, L, 1024), lambda b, c, k=k: (b, c, k))

    def const_spec(shape):
        return pl.BlockSpec(shape, lambda b, c: (0,) * len(shape))

    return pl.pallas_call(
        _ssd_kernel,
        grid=(bsz, nc),
        in_specs=[
            zx_spec(0), zx_spec(1), zx_spec(2), zx_spec(3), zx_spec(4),
            pl.BlockSpec((None, L, LANES), lambda b, c: (b, c, 0)),
            pl.BlockSpec((None, L, D_MODEL), lambda b, c: (b, c, 0)),
            const_spec((SSD_D_CONV, SSD_CONV_DIM)),
            const_spec((1, SSD_CONV_DIM)),
            const_spec((1, LANES)),
            const_spec((1, LANES)),
            const_spec((1, SSD_D_INNER)),
            const_spec((1, SSD_D_INNER)),
            const_spec((LANES, SSD_D_INNER)),
            const_spec((SSD_D_INNER, D_MODEL)),
        ],
        out_specs=pl.BlockSpec((None, L, D_MODEL), lambda b, c: (b, c, 0)),
        out_shape=jax.ShapeDtypeStruct((bsz, seq, D_MODEL), F32),
        scratch_shapes=[
            pltpu.VMEM((SSD_N_GROUPS, SSD_D_STATE, SSD_GROUP_WIDTH), F32),
            pltpu.VMEM((SUBLANES + L, SSD_CONV_DIM), F32),
        ],
        compiler_params=_cparams("parallel", "arbitrary"),
        name="ssd_core",
    )(zxbc, zxbc, zxbc, zxbc, zxbc, dt_raw, h,
      conv_w.astype(F32), conv_b.astype(F32).reshape(1, SSD_CONV_DIM), dtb, alog, dexp,
      norm_g.astype(F32).reshape(1, SSD_D_INNER), expand, w_out_b)


def ssd_layer(h, norm_g_in, w_in, conv_w, conv_b, dt_bias, a_log, d_skip, norm_g, w_out):
    bsz, seq, d = h.shape
    h2 = h.reshape(bsz * seq, d)
    n_main = SSD_D_INNER + SSD_CONV_DIM
    w_main = w_in[:, :n_main].astype(BF16)
    w_dt = jnp.pad(w_in[:, n_main:], ((0, 0), (0, LANES - SSD_N_HEADS))).astype(BF16)
    zxbc = norm_matmul(h2, norm_g_in, w_main, BF16, tm=1024, tn=1024)
    dt_raw = norm_matmul(h2, norm_g_in, w_dt, F32, tm=1024, tn=LANES)
    return ssd_core(zxbc.reshape(bsz, seq, n_main), dt_raw.reshape(bsz, seq, LANES), h,
                    conv_w, conv_b, dt_bias, a_log, d_skip, norm_g, w_out.astype(BF16))


def _da_kernel(lq1_ref, lk1_ref, lq2_ref, lk2_ref, sub_ref, q_ref, k_ref, v_ref, o_ref,
               acc1_ref, acc2_ref, m1_ref, m2_ref, *, tq, lambda_init):
    i = pl.program_id(2)
    q = q_ref[...]
    lane = lax.broadcasted_iota(jnp.int32, (tq, LANES), 1)
    qs = (q.astype(F32) * (DA_HEAD_DIM ** -0.5 * LOG2_E)).astype(BF16)
    zero = jnp.zeros_like(qs)
    q_maps = (jnp.where(lane < DA_HEAD_DIM, qs, zero), jnp.where(lane >= DA_HEAD_DIM, qs, zero))
    states = ((m1_ref, acc1_ref), (m2_ref, acc2_ref))

    for m_ref, acc_ref in states:
        m_ref[...] = jnp.full((tq, LANES), -jnp.inf, F32)
        acc_ref[...] = jnp.zeros((tq, 2 * LANES), F32)

    ones = jnp.ones((tq, LANES), BF16)

    def step(j, masked):
        start = pl.multiple_of(j * tq, tq)
        kt = k_ref[pl.ds(start, tq), :]
        v_aug = jnp.concatenate([v_ref[pl.ds(start, tq), :], ones], axis=1)
        for qm, (m_ref, acc_ref) in zip(q_maps, states):
            s = _dot_nt(qm, kt)
            if masked:
                r = lax.broadcasted_iota(jnp.int32, (tq, tq), 0)
                cidx = lax.broadcasted_iota(jnp.int32, (tq, tq), 1)
                s = jnp.where(cidx <= r, s, -jnp.inf)
            m_old = m_ref[...]
            m_new = jnp.maximum(m_old, jnp.max(s, axis=-1, keepdims=True))
            alpha = jnp.exp2(m_old - m_new)
            p = jnp.exp2(s - jnp.concatenate([m_new] * (tq // LANES), axis=1))
            acc_ref[...] = (jnp.concatenate([alpha, alpha], axis=1) * acc_ref[...]
                            + _dot(p.astype(BF16), v_aug))
            m_ref[...] = m_new

    def body(j, carry):
        step(j, False)
        return carry

    lax.fori_loop(0, i, body, 0)
    step(i, True)

    lam = (jnp.exp(jnp.sum(lq1_ref[...] * lk1_ref[...], axis=-1, keepdims=True))
           - jnp.exp(jnp.sum(lq2_ref[...] * lk2_ref[...], axis=-1, keepdims=True)) + lambda_init)
    o1 = acc1_ref[:, :LANES] / acc1_ref[:, LANES:]
    o2 = acc2_ref[:, :LANES] / acc2_ref[:, LANES:]
    o = _rms(o1 - lam * o2, sub_ref[...]) * (1.0 - lambda_init)
    o_ref[...] = o.astype(o_ref.dtype)


def diff_attention_core(qkv, lq1, lk1, lq2, lk2, subln_g, layer_idx, tq=512):
    bsz, seq, _ = qkv.shape
    tq = min(tq, seq)
    lambda_init = 0.8 - 0.6 * math.exp(-0.3 * layer_idx)
    nh = DA_N_HEADS

    def vec_spec(n):
        return pl.BlockSpec((1, n), lambda b, h, i: (0, 0))

    return pl.pallas_call(
        functools.partial(_da_kernel, tq=tq, lambda_init=lambda_init),
        grid=(bsz, nh, seq // tq),
        in_specs=[
            vec_spec(DA_HEAD_DIM), vec_spec(DA_HEAD_DIM), vec_spec(DA_HEAD_DIM), vec_spec(DA_HEAD_DIM),
            vec_spec(LANES),
            pl.BlockSpec((None, tq, LANES), lambda b, h, i: (b, i, h)),
            pl.BlockSpec((None, seq, LANES), lambda b, h, i: (b, 0, nh + h)),
            pl.BlockSpec((None, seq, LANES), lambda b, h, i: (b, 0, 2 * nh + h)),
        ],
        out_specs=pl.BlockSpec((None, tq, LANES), lambda b, h, i: (b, i, h)),
        out_shape=jax.ShapeDtypeStruct((bsz, seq, D_MODEL), BF16),
        scratch_shapes=[
            pltpu.VMEM((tq, 2 * LANES), F32), pltpu.VMEM((tq, 2 * LANES), F32),
            pltpu.VMEM((tq, LANES), F32), pltpu.VMEM((tq, LANES), F32),
        ],
        compiler_params=_cparams("parallel", "parallel", "arbitrary"),
        name="diff_attention",
    )(lq1.astype(F32).reshape(1, -1), lk1.astype(F32).reshape(1, -1),
      lq2.astype(F32).reshape(1, -1), lk2.astype(F32).reshape(1, -1),
      subln_g.astype(F32).reshape(1, -1), qkv, qkv, qkv)


def da_layer(h, norm_g_in, w_qkv, lq1, lk1, lq2, lk2, subln_g, w_o, layer_idx):
    bsz, seq, d = h.shape
    h2 = h.reshape(bsz * seq, d)
    qkv = norm_matmul(h2, norm_g_in, w_qkv.astype(BF16), BF16, tm=1024, tn=1024)
    o = diff_attention_core(qkv.reshape(bsz, seq, 3 * d), lq1, lk1, lq2, lk2, subln_g, layer_idx)
    return matmul_residual(o.reshape(bsz * seq, d), w_o.astype(BF16), h2).reshape(bsz, seq, d)


def _xattn_kernel(h_ref, g_ref, wq_ref, kv_ref, wo_ref, o_ref):
    h = h_ref[...]
    hn = _rms(h, g_ref[...]).astype(BF16)
    scale = XA_HEAD_DIM ** -0.5
    q = (_dot(hn, wq_ref[...]) * scale).astype(BF16)
    outs = []
    for hd in range(XA_N_HEADS):
        cs = slice(hd * XA_HEAD_DIM, (hd + 1) * XA_HEAD_DIM)
        vs = slice(D_MODEL + hd * XA_HEAD_DIM, D_MODEL + (hd + 1) * XA_HEAD_DIM)
        s = _dot_nt(q[:, cs], kv_ref[:, cs])
        m = jnp.max(s, axis=-1, keepdims=True)
        p = jnp.exp(s - m)
        l = jnp.sum(p, axis=-1, keepdims=True)
        outs.append((_dot(p.astype(BF16), kv_ref[:, vs]) / l).astype(BF16))
    o = jnp.concatenate(outs, axis=-1)
    o_ref[...] = h + _dot(o, wo_ref[...])


def xattn_layer(h, mem, norm_g, mem_norm_g, w_q, w_kv, w_o, tq=512):
    bsz, seq, d = h.shape
    mlen = mem.shape[1]
    tq = min(tq, seq)
    kv = norm_matmul(mem.reshape(bsz * mlen, d), mem_norm_g, w_kv.astype(BF16), BF16, tm=512, tn=1024)
    kv = kv.reshape(bsz, mlen, 2 * d)
    return pl.pallas_call(
        _xattn_kernel,
        grid=(bsz, seq // tq),
        in_specs=[
            pl.BlockSpec((None, tq, d), lambda b, i: (b, i, 0)),
            pl.BlockSpec((1, d), lambda b, i: (0, 0)),
            pl.BlockSpec((d, d), lambda b, i: (0, 0)),
            pl.BlockSpec((None, mlen, 2 * d), lambda b, i: (b, 0, 0)),
            pl.BlockSpec((d, d), lambda b, i: (0, 0)),
        ],
        out_specs=pl.BlockSpec((None, tq, d), lambda b, i: (b, i, 0)),
        out_shape=jax.ShapeDtypeStruct((bsz, seq, d), F32),
        compiler_params=_cparams("parallel", "parallel"),
        name="mem_xattn",
    )(h, norm_g.astype(F32).reshape(1, d), w_q.astype(BF16), kv, w_o.astype(BF16))


SLAB = D_MODEL // LANES


def _router_kernel(h_ref, g_ref, wh_ref, wl_ref, br_ref, su_ref,
                   slab_ref, idx_ref, gate_ref, rank_ref, cnt_ref, run_ref, *, tm):
    i = pl.program_id(0)

    @pl.when(i == 0)
    def _():
        run_ref[...] = jnp.zeros_like(run_ref)

    hn = _rms(h_ref[...], g_ref[...])
    for s in range(SLAB):
        slab_ref[pl.ds(s, tm, stride=SLAB), :] = hn[:, s * LANES:(s + 1) * LANES]

    x_hi, x_lo = _split2(hn)
    wh = wh_ref[...]
    logits = _dot_nt(wh, x_hi) + _dot_nt(wh, x_lo) + _dot_nt(wl_ref[...], x_hi) + br_ref[...]

    rows = lax.broadcasted_iota(jnp.int32, (N_EXPERTS, tm), 0).astype(F32)
    tops, idxs, onehots = [], [], []
    cur = logits
    for _ in range(TOP_K):
        m = jnp.max(cur, axis=0, keepdims=True)
        idx = jnp.min(jnp.where(cur == m, rows, float(N_EXPERTS)), axis=0, keepdims=True)
        oh = rows == idx
        cur = jnp.where(oh, -jnp.inf, cur)
        tops.append(m)
        idxs.append(idx)
        onehots.append(oh)
    exps = [jnp.exp(t - tops[0]) for t in tops]
    denom = exps[0] + exps[1] + exps[2] + exps[3]
    gate_ref[...] = jnp.concatenate([e / denom for e in exps], axis=0)
    idx_ref[...] = jnp.concatenate(idxs, axis=0).astype(jnp.int32)

    oh_sum = jnp.zeros((N_EXPERTS, tm), F32)
    for oh in onehots:
        oh_sum = oh_sum + jnp.where(oh, 1.0, 0.0)
    run = run_ref[...]
    prefix = _dot(oh_sum.astype(BF16), su_ref[...]) + run[:, 0:1]
    ranks = [jnp.sum(jnp.where(oh, prefix, 0.0), axis=0, keepdims=True) for oh in onehots]
    rank_ref[...] = jnp.concatenate(ranks, axis=0).astype(jnp.int32)
    run_new = run + jnp.sum(oh_sum, axis=1, keepdims=True)
    run_ref[...] = run_new
    cnt_ref[...] = run_new


def moe_router(h2, norm_g, w_router, b_router, tm=ROUTER_TILE):
    t, d = h2.shape
    tm = min(tm, t)
    wt = w_router.astype(F32).T
    wh = wt.astype(BF16)
    wl = (wt - wh.astype(F32)).astype(BF16)
    su = (jnp.arange(tm)[:, None] < jnp.arange(tm)[None, :]).astype(BF16)
    return pl.pallas_call(
        functools.partial(_router_kernel, tm=tm),
        grid=(t // tm,),
        in_specs=[
            pl.BlockSpec((tm, d), lambda i: (i, 0)),
            pl.BlockSpec((1, d), lambda i: (0, 0)),
            pl.BlockSpec((N_EXPERTS, d), lambda i: (0, 0)),
            pl.BlockSpec((N_EXPERTS, d), lambda i: (0, 0)),
            pl.BlockSpec((N_EXPERTS, 1), lambda i: (0, 0)),
            pl.BlockSpec((tm, tm), lambda i: (0, 0)),
        ],
        out_specs=[
            pl.BlockSpec((tm * SLAB, LANES), lambda i: (i, 0)),
            pl.BlockSpec((TOP_K, tm), lambda i: (0, i)),
            pl.BlockSpec((TOP_K, tm), lambda i: (0, i)),
            pl.BlockSpec((TOP_K, tm), lambda i: (0, i)),
            pl.BlockSpec((N_EXPERTS, LANES), lambda i: (0, 0)),
        ],
        out_shape=[
            jax.ShapeDtypeStruct((t * SLAB, LANES), F32),
            jax.ShapeDtypeStruct((TOP_K, t), jnp.int32),
            jax.ShapeDtypeStruct((TOP_K, t), F32),
            jax.ShapeDtypeStruct((TOP_K, t), jnp.int32),
            jax.ShapeDtypeStruct((N_EXPERTS, LANES), F32),
        ],
        scratch_shapes=[pltpu.VMEM((N_EXPERTS, LANES), F32)],
        compiler_params=_cparams("arbitrary"),
        name="moe_router",
    )(h2, norm_g.astype(F32).reshape(1, d), wh, wl, b_router.astype(F32).reshape(N_EXPERTS, 1), su)


def _slab_rows(r):
    return pl.ds(pl.multiple_of(r * SLAB, SLAB), SLAB)


def _dispatch_kernel(dest_ref, slab_ref, xs_in_ref, xs_ref, sem, *, tm):
    del xs_in_ref

    def row_copy(t, d):
        return pltpu.make_async_copy(slab_ref.at[_slab_rows(t), :], xs_ref.at[_slab_rows(d), :], sem)

    def issue(g, carry):
        for u in range(DMA_ISSUE_UNROLL):
            t = g * DMA_ISSUE_UNROLL + u
            for k in range(TOP_K):
                row_copy(t, dest_ref[0, 0, k * tm + t]).start(priority=k % 2)
        return carry

    lax.fori_loop(0, tm // DMA_ISSUE_UNROLL, issue, 0)

    for k in range(TOP_K):
        pltpu.make_async_copy(slab_ref, xs_ref.at[pl.ds(0, tm * SLAB), :], sem).wait()


def moe_dispatch(dest_tiles, slab, p_rows, tm):
    t = slab.shape[0] // SLAB
    xs0 = jnp.zeros((p_rows * SLAB, LANES), F32)
    return pl.pallas_call(
        functools.partial(_dispatch_kernel, tm=tm),
        grid=(t // tm,),
        in_specs=[
            pl.BlockSpec((1, 1, TOP_K * tm), lambda i: (i, 0, 0), memory_space=pltpu.SMEM),
            pl.BlockSpec((tm * SLAB, LANES), lambda i: (i, 0)),
            pl.BlockSpec(memory_space=pl.ANY),
        ],
        out_specs=pl.BlockSpec(memory_space=pl.ANY),
        out_shape=jax.ShapeDtypeStruct((p_rows * SLAB, LANES), F32),
        scratch_shapes=[pltpu.SemaphoreType.DMA(())],
        input_output_aliases={2: 0},
        compiler_params=_cparams("arbitrary"),
        name="moe_dispatch",
    )(dest_tiles, slab, xs0)


def _expert_kernel(blk_e_ref, xs_ref, wgu_ref, bgu_ref, wd_ref, bd_ref, ys_ref, wgu_b_ref, wd_b_ref, *, bm):
    i = pl.program_id(0)
    prev_e = blk_e_ref[jnp.maximum(i - 1, 0)]

    @pl.when((i == 0) | (blk_e_ref[i] != prev_e))
    def _():
        wgu_b_ref[...] = wgu_ref[...].astype(BF16)
        wd_b_ref[...] = wd_ref[...].astype(BF16)

    x = jnp.concatenate([xs_ref[pl.ds(s, bm, stride=SLAB), :] for s in range(SLAB)], axis=-1)
    gu = _dot(x.astype(BF16), wgu_b_ref[...]) + bgu_ref[...]
    gate = jnp.minimum(gu[:, :D_FF], SWIGLU_LIMIT)
    up = jnp.clip(gu[:, D_FF:], -SWIGLU_LIMIT, SWIGLU_LIMIT)
    act = (up + 1.0) * (gate * jax.nn.sigmoid(gate * SWIGLU_ALPHA))
    y = _dot(act.astype(BF16), wd_b_ref[...]) + bd_ref[...]
    for s in range(SLAB):
        ys_ref[pl.ds(s, bm, stride=SLAB), :] = y[:, s * LANES:(s + 1) * LANES]


def moe_experts(blk_e, xs, w_gate_up, b_gate_up, w_down, b_down, layer, bm):
    nb = blk_e.shape[0]
    d = D_MODEL
    grid_spec = pltpu.PrefetchScalarGridSpec(
        num_scalar_prefetch=1,
        grid=(nb,),
        in_specs=[
            pl.BlockSpec((bm * SLAB, LANES), lambda i, be: (i, 0)),
            pl.BlockSpec((None, None, d, 2 * D_FF), lambda i, be: (layer, be[i], 0, 0)),
            pl.BlockSpec((None, 1, 2 * D_FF), lambda i, be: (be[i], 0, 0)),
            pl.BlockSpec((None, None, D_FF, d), lambda i, be: (layer, be[i], 0, 0)),
            pl.BlockSpec((None, 1, d), lambda i, be: (be[i], 0, 0)),
        ],
        out_specs=pl.BlockSpec((bm * SLAB, LANES), lambda i, be: (i, 0)),
        scratch_shapes=[pltpu.VMEM((d, 2 * D_FF), BF16), pltpu.VMEM((D_FF, d), BF16)],
    )
    return pl.pallas_call(
        functools.partial(_expert_kernel, bm=bm),
        grid_spec=grid_spec,
        out_shape=jax.ShapeDtypeStruct((nb * bm * SLAB, LANES), F32),
        compiler_params=_cparams("arbitrary"),
        name="moe_experts",
    )(blk_e, xs, w_gate_up, b_gate_up.astype(F32).reshape(N_EXPERTS, 1, 2 * D_FF),
      w_down, b_down.astype(F32).reshape(N_EXPERTS, 1, d))


def _combine_kernel(dest_ref, dest_next_ref, gate_ref, h_ref, fg_ref, ys_ref, o_ref, buf_ref, sem,
                    *, tm, final_norm):
    i = pl.program_id(0)
    n = pl.num_programs(0)
    rows = TOP_K * tm * SLAB

    def issue_tile(d_ref, slot):
        base = slot * rows

        def issue(g, carry):
            for u in range(DMA_ISSUE_UNROLL):
                j = g * DMA_ISSUE_UNROLL + u
                pltpu.make_async_copy(
                    ys_ref.at[_slab_rows(d_ref[0, 0, j]), :],
                    buf_ref.at[pl.ds(pl.multiple_of(base + j * SLAB, SLAB), SLAB), :],
                    sem.at[slot]).start(priority=u % 2)
            return carry

        lax.fori_loop(0, TOP_K * tm // DMA_ISSUE_UNROLL, issue, 0)

    slot = i % 2

    @pl.when(i == 0)
    def _():
        issue_tile(dest_ref, 0)

    @pl.when(i + 1 < n)
    def _():
        issue_tile(dest_next_ref, 1 - slot)

    base = pl.multiple_of(slot * rows, rows)
    pltpu.make_async_copy(ys_ref.at[pl.ds(0, rows), :], buf_ref.at[pl.ds(base, rows), :], sem.at[slot]).wait()

    gates = gate_ref[...]
    for s in range(SLAB):
        cols = slice(s * LANES, (s + 1) * LANES)
        acc = h_ref[:, cols]
        for k in range(TOP_K):
            acc = acc + gates[:, k:k + 1] * buf_ref[pl.ds(base + k * tm * SLAB + s, tm, stride=SLAB), :]
        o_ref[:, cols] = acc
    if final_norm:
        o_ref[...] = _rms(o_ref[...], fg_ref[...])


def moe_combine(dest_tiles, gates_col, h2, ys, tm, final_g=None):
    t, d = h2.shape
    nt = t // tm
    fg = jnp.ones((1, d), F32) if final_g is None else final_g.astype(F32).reshape(1, d)
    return pl.pallas_call(
        functools.partial(_combine_kernel, tm=tm, final_norm=final_g is not None),
        grid=(nt,),
        in_specs=[
            pl.BlockSpec((1, 1, TOP_K * tm), lambda i: (i, 0, 0), memory_space=pltpu.SMEM),
            pl.BlockSpec((1, 1, TOP_K * tm), lambda i: (jnp.minimum(i + 1, nt - 1), 0, 0),
                         memory_space=pltpu.SMEM),
            pl.BlockSpec((tm, TOP_K), lambda i: (i, 0)),
            pl.BlockSpec((tm, d), lambda i: (i, 0)),
            pl.BlockSpec((1, d), lambda i: (0, 0)),
            pl.BlockSpec(memory_space=pl.ANY),
        ],
        out_specs=pl.BlockSpec((tm, d), lambda i: (i, 0)),
        out_shape=jax.ShapeDtypeStruct((t, d), F32),
        scratch_shapes=[pltpu.VMEM((2 * TOP_K * tm * SLAB, LANES), F32), pltpu.SemaphoreType.DMA((2,))],
        compiler_params=_cparams("arbitrary"),
        name="moe_combine",
    )(dest_tiles, dest_tiles, gates_col, h2, fg, ys)


def _tile_major(a, tm):
    t = a.shape[1]
    return a.reshape(TOP_K, t // tm, tm).transpose(1, 0, 2).reshape(t // tm, 1, TOP_K * tm)


def moe_layer(h, norm_g, w_router, b_router, w_gate_up, b_gate_up, w_down, b_down, layer, final_g=None):
    bsz, seq, d = h.shape
    t = bsz * seq
    h2 = h.reshape(t, d)
    bm = MOE_ROWS_PER_BLOCK
    slab, idx, gates, rank, cnt = moe_router(h2, norm_g, w_router, b_router)

    counts = cnt[:, 0].astype(jnp.int32)
    padded = (counts + bm - 1) // bm * bm
    pend = jnp.cumsum(padded)
    pstart = pend - padded
    experts = jnp.arange(N_EXPERTS, dtype=jnp.int32)
    dest = rank + jnp.sum(jnp.where(idx[..., None] == experts, pstart, 0), axis=-1)
    n = t * TOP_K
    p_rows = -(-(n + N_EXPERTS * bm) // bm) * bm
    nb = p_rows // bm
    blk_start = jnp.arange(nb, dtype=jnp.int32) * bm
    blk_e = jnp.minimum(jnp.sum((pend[None, :] <= blk_start[:, None]).astype(jnp.int32), axis=1),
                        N_EXPERTS - 1)

    td = min(DISPATCH_TILE, t)
    xs = moe_dispatch(_tile_major(dest, td), slab, p_rows, td)
    ys = moe_experts(blk_e, xs, w_gate_up.astype(F32), b_gate_up, w_down.astype(F32), b_down, layer, bm)
    tc = min(COMBINE_TILE, t)
    out = moe_combine(_tile_major(dest, tc), gates.T, h2, ys, tc, final_g)
    return out.reshape(bsz, seq, d)


def kernel(x, mem, mixer_norm, xattn_norm, mem_norm, ffn_norm, ssd_w_in, ssd_conv_w, ssd_conv_b, ssd_dt_bias, ssd_a_log, ssd_d, ssd_norm, ssd_w_out, da_w_qkv, da_lambda_q1, da_lambda_k1, da_lambda_q2, da_lambda_k2, da_subln, da_w_o, xa_w_q, xa_w_kv, xa_w_o, moe_w_router, moe_b_router, moe_w_gate_up, moe_b_gate_up, moe_w_down, moe_b_down, final_norm):
    depth = mixer_norm.shape[0]
    bsz, seq, d = x.shape
    h = x
    for i in range(depth):
        j = i // N_MIXERS
        if i % N_MIXERS == 0:
            h = ssd_layer(h, mixer_norm[i], ssd_w_in[j], ssd_conv_w[j], ssd_conv_b[j], ssd_dt_bias[j],
                          ssd_a_log[j], ssd_d[j], ssd_norm[j], ssd_w_out[j])
        else:
            h = da_layer(h, mixer_norm[i], da_w_qkv[j], da_lambda_q1[j], da_lambda_k1[j], da_lambda_q2[j],
                         da_lambda_k2[j], da_subln[j], da_w_o[j], i)
        h = xattn_layer(h, mem, xattn_norm[i], mem_norm[i], xa_w_q[i], xa_w_kv[i], xa_w_o[i])
        h = moe_layer(h, ffn_norm[i], moe_w_router[i], moe_b_router[i], moe_w_gate_up,
                      moe_b_gate_up[i], moe_w_down, moe_b_down[i], i,
                      final_g=final_norm if i == depth - 1 else None)
    return h
```

```python
import functools
import math

import jax
import jax.numpy as jnp
from jax import lax
from jax.experimental import pallas as pl
from jax.experimental.pallas import tpu as pltpu

F32 = jnp.float32
BF16 = jnp.bfloat16

D_MODEL = 1024
RMS_EPS = 1e-5
LOG2_E = 1.4426950408889634
N_MIXERS = 2

SSD_D_INNER = 2048
SSD_HEADDIM = 64
SSD_N_HEADS = 32
SSD_N_GROUPS = 4
SSD_HEADS_PER_GROUP = 8
SSD_D_STATE = 128
SSD_D_CONV = 4
SSD_CHUNK = 128
SSD_GN = 512
SSD_CONV_DIM = 3072
SSD_GROUP_WIDTH = SSD_D_INNER // SSD_N_GROUPS

DA_HEAD_DIM = 64
DA_N_HEADS = 8

XA_N_HEADS = 4
XA_HEAD_DIM = 256

N_EXPERTS = 32
TOP_K = 4
D_FF = 1024
SWIGLU_LIMIT = 7.0
SWIGLU_ALPHA = 1.702

LANES = 128
SUBLANES = 8
VMEM_LIMIT_BYTES = 56 * 1024 * 1024

MOE_ROWS_PER_BLOCK = 512
ROUTER_TILE = 512
DISPATCH_TILE = 256
COMBINE_TILE = 256
DMA_ISSUE_UNROLL = 8
PAD_PIECE = 256
PAD_BITS = (256, 128, 64, 32, 16, 8, 4, 2, 1)


def _cparams(*sem):
    return pltpu.CompilerParams(dimension_semantics=sem, vmem_limit_bytes=VMEM_LIMIT_BYTES)


def _rms(x, g):
    ms = jnp.mean(x * x, axis=-1, keepdims=True)
    return x * lax.rsqrt(ms + RMS_EPS) * g


def _dot(a, b):
    return jnp.dot(a, b, preferred_element_type=F32)


def _dot_nt(a, b):
    return lax.dot_general(a, b, (((1,), (1,)), ((), ())), preferred_element_type=F32)


def _split2(x):
    hi = x.astype(BF16)
    lo = (x - hi.astype(F32)).astype(BF16)
    return hi, lo


def _split3(x):
    hi = x.astype(BF16)
    r = x - hi.astype(F32)
    mid = r.astype(BF16)
    lo = (r - mid.astype(F32)).astype(BF16)
    return hi, mid, lo


def _norm_mm_kernel(x_ref, g_ref, w_ref, o_ref, xn_ref):
    @pl.when(pl.program_id(1) == 0)
    def _():
        xn_ref[...] = _rms(x_ref[...], g_ref[...]).astype(BF16)

    o_ref[...] = _dot(xn_ref[...], w_ref[...]).astype(o_ref.dtype)


def norm_matmul(x, g, w, out_dtype, tm, tn):
    m, k = x.shape
    n = w.shape[1]
    tm = min(tm, m)
    tn = min(tn, n)
    return pl.pallas_call(
        _norm_mm_kernel,
        grid=(m // tm, n // tn),
        in_specs=[
            pl.BlockSpec((tm, k), lambda i, j: (i, 0)),
            pl.BlockSpec((1, k), lambda i, j: (0, 0)),
            pl.BlockSpec((k, tn), lambda i, j: (0, j)),
        ],
        out_specs=pl.BlockSpec((tm, tn), lambda i, j: (i, j)),
        out_shape=jax.ShapeDtypeStruct((m, n), out_dtype),
        scratch_shapes=[pltpu.VMEM((tm, k), BF16)],
        compiler_params=_cparams("parallel", "arbitrary"),
        name="norm_matmul",
    )(x, g.reshape(1, k), w)


def _mm_res_kernel(x_ref, w_ref, r_ref, o_ref):
    o_ref[...] = r_ref[...] + _dot(x_ref[...], w_ref[...])


def matmul_residual(x, w, res, tm=512):
    m, k = x.shape
    n = w.shape[1]
    tm = min(tm, m)
    return pl.pallas_call(
        _mm_res_kernel,
        grid=(m // tm,),
        in_specs=[
            pl.BlockSpec((tm, k), lambda i: (i, 0)),
            pl.BlockSpec((k, n), lambda i: (0, 0)),
            pl.BlockSpec((tm, n), lambda i: (i, 0)),
        ],
        out_specs=pl.BlockSpec((tm, n), lambda i: (i, 0)),
        out_shape=jax.ShapeDtypeStruct((m, n), F32),
        compiler_params=_cparams("parallel"),
        name="matmul_residual",
    )(x, w, res)


def _norm_kernel(x_ref, g_ref, o_ref):
    o_ref[...] = _rms(x_ref[...], g_ref[...])


def final_norm_call(x, g, tm=1024):
    m, k = x.shape
    tm = min(tm, m)
    return pl.pallas_call(
        _norm_kernel,
        grid=(m // tm,),
        in_specs=[pl.BlockSpec((tm, k), lambda i: (i, 0)), pl.BlockSpec((1, k), lambda i: (0, 0))],
        out_specs=pl.BlockSpec((tm, k), lambda i: (i, 0)),
        out_shape=jax.ShapeDtypeStruct((m, k), F32),
        compiler_params=_cparams("parallel"),
        name="final_norm",
    )(x, g.reshape(1, k))


def _ssd_kernel(z0_ref, z1_ref, x0_ref, x1_ref, bc_ref, dtr_ref, h_ref,
                convw_ref, convb_ref, dtb_ref, alog_ref, dexp_ref, ng_ref, expand_ref, wout_ref,
                o_ref, state_ref, ext_ref):
    L = SSD_CHUNK
    c = pl.program_id(1)

    @pl.when(c == 0)
    def _():
        state_ref[...] = jnp.zeros_like(state_ref)
        ext_ref[0:L, :] = jnp.zeros((L, SSD_CONV_DIM), BF16)

    srow = lax.broadcasted_iota(jnp.int32, (L, 2 * L), 0)
    scol = lax.broadcasted_iota(jnp.int32, (L, 2 * L), 1)
    shifts = [jnp.where(scol == srow + (L - (SSD_D_CONV - 1) + k), 1.0, 0.0).astype(BF16)
              for k in range(SSD_D_CONV - 1)]
    pieces = []
    for blk, ref in enumerate((x0_ref, x1_ref, bc_ref)):
        cols = slice(blk * 1024, (blk + 1) * 1024)
        cur = ref[...]
        ext_ref[L:2 * L, cols] = cur
        both = ext_ref[:, cols]
        acc = convb_ref[:, cols] + convw_ref[SSD_D_CONV - 1:SSD_D_CONV, cols] * cur.astype(F32)
        for k in range(SSD_D_CONV - 1):
            acc = acc + convw_ref[k:k + 1, cols] * _dot(shifts[k], both)
        pieces.append(acc * jax.nn.sigmoid(acc))
        ext_ref[0:L, cols] = cur
    xs = jnp.concatenate(pieces[:2], axis=-1)
    b_all = pieces[2][:, :SSD_GN]
    c_all = pieces[2][:, SSD_GN:]

    dtr = dtr_ref[...] + dtb_ref[...]
    dt = jnp.maximum(dtr, 0.0) + jnp.log1p(jnp.exp(-jnp.abs(dtr)))
    a = -jnp.exp(alog_ref[...])
    da = dt * a
    row = lax.broadcasted_iota(jnp.int32, (L, L), 0)
    col = lax.broadcasted_iota(jnp.int32, (L, L), 1)
    causal = col <= row
    tril = jnp.where(causal, 1.0, 0.0).astype(BF16)
    d_hi, d_mid, d_lo = _split3(da)
    a_cum = _dot(tril, d_hi) + _dot(tril, d_mid) + _dot(tril, d_lo)
    a_cum_t = a_cum.T

    expand = expand_ref[...]
    t_hi, t_lo = _split2(dt)
    dt_e = _dot(t_hi, expand) + _dot(t_lo, expand)
    e_hi, e_lo = _split2(jnp.exp(a_cum))
    exp_acum_e = _dot(e_hi, expand) + _dot(e_lo, expand)
    w_hi, w_lo = _split2(jnp.exp(a_cum[L - 1:L, :] - a_cum))
    dte_e = _dot(w_hi, expand) + _dot(w_lo, expand)
    cd_e = exp_acum_e[L - 1:L, :]

    xd = xs * dt_e
    xd_b = xd.astype(BF16)
    xdw_b = (xd * dte_e).astype(BF16)

    lane = lax.broadcasted_iota(jnp.int32, (L, LANES), 1)
    first_half = lane < SSD_HEADDIM

    y_parts = []
    for g in range(SSD_N_GROUPS):
        gs = slice(g * SSD_D_STATE, (g + 1) * SSD_D_STATE)
        gw = slice(g * SSD_GROUP_WIDTH, (g + 1) * SSD_GROUP_WIDTH)
        b_g = b_all[:, gs]
        c_g = c_all[:, gs].astype(BF16)
        cb = _dot_nt(c_g, b_g.astype(BF16))
        y_pairs = []
        for jp in range(SSD_HEADS_PER_GROUP // 2):
            res = []
            pair_col = g * SSD_GROUP_WIDTH + jp * LANES
            xd_pair = xd_b[:, pair_col:pair_col + LANES]
            for sub in range(2):
                hd = g * SSD_HEADS_PER_GROUP + jp * 2 + sub
                diff = a_cum[:, hd:hd + 1] - a_cum_t[hd:hd + 1, :]
                dec = jnp.exp(jnp.where(causal, diff, -jnp.inf))
                res.append(_dot((cb * dec).astype(BF16), xd_pair))
            y_pairs.append(jnp.where(first_half, res[0], res[1]))
        y_diag = jnp.concatenate(y_pairs, axis=-1)
        st = state_ref[g]
        y_off = _dot(c_g, st.astype(BF16)) * exp_acum_e[:, gw]
        state_ref[g] = st * cd_e[:, gw] + _dot(b_g.T.astype(BF16), xdw_b[:, gw])
        y_parts.append(y_diag + y_off)
    y = jnp.concatenate(y_parts, axis=-1) + dexp_ref[...] * xs

    z = jnp.concatenate([z0_ref[...], z1_ref[...]], axis=-1).astype(F32)
    u = y * (z * jax.nn.sigmoid(z))
    u_parts = []
    for g in range(SSD_N_GROUPS):
        gw = slice(g * SSD_GROUP_WIDTH, (g + 1) * SSD_GROUP_WIDTH)
        ug = u[:, gw]
        ms = jnp.mean(ug * ug, axis=-1, keepdims=True)
        u_parts.append(ug * lax.rsqrt(ms + RMS_EPS) * ng_ref[:, gw])
    un = jnp.concatenate(u_parts, axis=-1).astype(BF16)
    o_ref[...] = h_ref[...] + _dot(un, wout_ref[...])


def ssd_core(zxbc, dt_raw, h, conv_w, conv_b, dt_bias, a_log, d_skip, norm_g, w_out_b):
    bsz, seq, _ = h.shape
    L = SSD_CHUNK
    nc = seq // L
    pad_heads = LANES - SSD_N_HEADS
    dtb = jnp.pad(dt_bias.astype(F32), (0, pad_heads)).reshape(1, LANES)
    alog = jnp.pad(a_log.astype(F32), (0, pad_heads)).reshape(1, LANES)
    dexp = jnp.repeat(d_skip.astype(F32), SSD_HEADDIM).reshape(1, SSD_D_INNER)
    head_of_col = jnp.arange(SSD_D_INNER, dtype=jnp.int32) // SSD_HEADDIM
    expand = (jnp.arange(LANES, dtype=jnp.int32)[:, None] == head_of_col[None, :]).astype(BF16)

    def zx_spec(k):
        return pl.BlockSpec((None, L, 1024), lambda b, c, k=k: (b, c, k))

    def const_spec(shape):
        return pl.BlockSpec(shape, lambda b, c: (0,) * len(shape))

    return pl.pallas_call(
        _ssd_kernel,
        grid=(bsz, nc),
        in_specs=[
            zx_spec(0), zx_spec(1), zx_spec(2), zx_spec(3), zx_spec(4),
            pl.BlockSpec((None, L, LANES), lambda b, c: (b, c, 0)),
            pl.BlockSpec((None, L, D_MODEL), lambda b, c: (b, c, 0)),
            const_spec((SSD_D_CONV, SSD_CONV_DIM)),
            const_spec((1, SSD_CONV_DIM)),
            const_spec((1, LANES)),
            const_spec((1, LANES)),
            const_spec((1, SSD_D_INNER)),
            const_spec((1, SSD_D_INNER)),
            const_spec((LANES, SSD_D_INNER)),
            const_spec((SSD_D_INNER, D_MODEL)),
        ],
        out_specs=pl.BlockSpec((None, L, D_MODEL), lambda b, c: (b, c, 0)),
        out_shape=jax.ShapeDtypeStruct((bsz, seq, D_MODEL), F32),
        scratch_shapes=[
            pltpu.VMEM((SSD_N_GROUPS, SSD_D_STATE, SSD_GROUP_WIDTH), F32),
            pltpu.VMEM((2 * L, SSD_CONV_DIM), BF16),
        ],
        compiler_params=_cparams("parallel", "arbitrary"),
        name="ssd_core",
    )(zxbc, zxbc, zxbc, zxbc, zxbc, dt_raw, h,
      conv_w.astype(F32), conv_b.astype(F32).reshape(1, SSD_CONV_DIM), dtb, alog, dexp,
      norm_g.astype(F32).reshape(1, SSD_D_INNER), expand, w_out_b)


def ssd_layer(h, norm_g_in, w_in, conv_w, conv_b, dt_bias, a_log, d_skip, norm_g, w_out):
    bsz, seq, d = h.shape
    h2 = h.reshape(bsz * seq, d)
    n_main = SSD_D_INNER + SSD_CONV_DIM
    w_main = w_in[:, :n_main].astype(BF16)
    w_dt = jnp.pad(w_in[:, n_main:], ((0, 0), (0, LANES - SSD_N_HEADS))).astype(BF16)
    zxbc = norm_matmul(h2, norm_g_in, w_main, BF16, tm=1024, tn=1024)
    dt_raw = norm_matmul(h2, norm_g_in, w_dt, F32, tm=1024, tn=LANES)
    return ssd_core(zxbc.reshape(bsz, seq, n_main), dt_raw.reshape(bsz, seq, LANES), h,
                    conv_w, conv_b, dt_bias, a_log, d_skip, norm_g, w_out.astype(BF16))


def _da_kernel(lq1_ref, lk1_ref, lq2_ref, lk2_ref, sub_ref, q_ref, k_ref, v_ref, o_ref,
               acc1_ref, acc2_ref, m1_ref, m2_ref, *, tq, lambda_init):
    i = pl.program_id(2)
    q = q_ref[...]
    lane = lax.broadcasted_iota(jnp.int32, (tq, LANES), 1)
    qs = (q.astype(F32) * (DA_HEAD_DIM ** -0.5 * LOG2_E)).astype(BF16)
    zero = jnp.zeros_like(qs)
    q_maps = (jnp.where(lane < DA_HEAD_DIM, qs, zero), jnp.where(lane >= DA_HEAD_DIM, qs, zero))
    states = ((m1_ref, acc1_ref), (m2_ref, acc2_ref))

    for m_ref, acc_ref in states:
        m_ref[...] = jnp.full((tq, LANES), -jnp.inf, F32)
        acc_ref[...] = jnp.zeros((tq, 2 * LANES), F32)

    ones = jnp.ones((tq, LANES), BF16)

    def step(j, masked):
        start = pl.multiple_of(j * tq, tq)
        kt = k_ref[pl.ds(start, tq), :]
        v_aug = jnp.concatenate([v_ref[pl.ds(start, tq), :], ones], axis=1)
        for qm, (m_ref, acc_ref) in zip(q_maps, states):
            s = _dot_nt(qm, kt)
            if masked:
                r = lax.broadcasted_iota(jnp.int32, (tq, tq), 0)
                cidx = lax.broadcasted_iota(jnp.int32, (tq, tq), 1)
                s = jnp.where(cidx <= r, s, -jnp.inf)
            m_old = m_ref[...]
            m_new = jnp.maximum(m_old, jnp.max(s, axis=-1, keepdims=True))
            alpha = jnp.exp2(m_old - m_new)
            p = jnp.exp2(s - jnp.concatenate([m_new] * (tq // LANES), axis=1))
            acc_ref[...] = (jnp.concatenate([alpha, alpha], axis=1) * acc_ref[...]
                            + _dot(p.astype(BF16), v_aug))
            m_ref[...] = m_new

    def body(jj, carry):
        step(2 * jj, False)
        step(2 * jj + 1, False)
        return carry

    lax.fori_loop(0, i // 2, body, 0)

    @pl.when(i % 2 == 1)
    def _():
        step(i - 1, False)

    step(i, True)

    lam = (jnp.exp(jnp.sum(lq1_ref[...] * lk1_ref[...], axis=-1, keepdims=True))
           - jnp.exp(jnp.sum(lq2_ref[...] * lk2_ref[...], axis=-1, keepdims=True)) + lambda_init)
    o1 = acc1_ref[:, :LANES] / acc1_ref[:, LANES:]
    o2 = acc2_ref[:, :LANES] / acc2_ref[:, LANES:]
    o = _rms(o1 - lam * o2, sub_ref[...]) * (1.0 - lambda_init)
    o_ref[...] = o.astype(o_ref.dtype)


def diff_attention_core(qkv, lq1, lk1, lq2, lk2, subln_g, layer_idx, tq=512):
    bsz, seq, _ = qkv.shape
    tq = min(tq, seq)
    lambda_init = 0.8 - 0.6 * math.exp(-0.3 * layer_idx)
    nh = DA_N_HEADS

    def vec_spec(n):
        return pl.BlockSpec((1, n), lambda b, h, i: (0, 0))

    return pl.pallas_call(
        functools.partial(_da_kernel, tq=tq, lambda_init=lambda_init),
        grid=(bsz, nh, seq // tq),
        in_specs=[
            vec_spec(DA_HEAD_DIM), vec_spec(DA_HEAD_DIM), vec_spec(DA_HEAD_DIM), vec_spec(DA_HEAD_DIM),
            vec_spec(LANES),
            pl.BlockSpec((None, tq, LANES), lambda b, h, i: (b, i, h)),
            pl.BlockSpec((None, seq, LANES), lambda b, h, i: (b, 0, nh + h)),
            pl.BlockSpec((None, seq, LANES), lambda b, h, i: (b, 0, 2 * nh + h)),
        ],
        out_specs=pl.BlockSpec((None, tq, LANES), lambda b, h, i: (b, i, h)),
        out_shape=jax.ShapeDtypeStruct((bsz, seq, D_MODEL), BF16),
        scratch_shapes=[
            pltpu.VMEM((tq, 2 * LANES), F32), pltpu.VMEM((tq, 2 * LANES), F32),
            pltpu.VMEM((tq, LANES), F32), pltpu.VMEM((tq, LANES), F32),
        ],
        compiler_params=_cparams("parallel", "parallel", "arbitrary"),
        name="diff_attention",
    )(lq1.astype(F32).reshape(1, -1), lk1.astype(F32).reshape(1, -1),
      lq2.astype(F32).reshape(1, -1), lk2.astype(F32).reshape(1, -1),
      subln_g.astype(F32).reshape(1, -1), qkv, qkv, qkv)


def da_layer(h, norm_g_in, w_qkv, lq1, lk1, lq2, lk2, subln_g, w_o, layer_idx):
    bsz, seq, d = h.shape
    h2 = h.reshape(bsz * seq, d)
    qkv = norm_matmul(h2, norm_g_in, w_qkv.astype(BF16), BF16, tm=1024, tn=1024)
    o = diff_attention_core(qkv.reshape(bsz, seq, 3 * d), lq1, lk1, lq2, lk2, subln_g, layer_idx)
    return matmul_residual(o.reshape(bsz * seq, d), w_o.astype(BF16), h2).reshape(bsz, seq, d)


def _xattn_kernel(h_ref, g_ref, wq_ref, kv_ref, wo_ref, o_ref):
    h = h_ref[...]
    hn = _rms(h, g_ref[...]).astype(BF16)
    scale = XA_HEAD_DIM ** -0.5
    q = (_dot(hn, wq_ref[...]) * scale).astype(BF16)
    outs = []
    for hd in range(XA_N_HEADS):
        cs = slice(hd * XA_HEAD_DIM, (hd + 1) * XA_HEAD_DIM)
        vs = slice(D_MODEL + hd * XA_HEAD_DIM, D_MODEL + (hd + 1) * XA_HEAD_DIM)
        s = _dot_nt(q[:, cs], kv_ref[:, cs])
        m = jnp.max(s, axis=-1, keepdims=True)
        p = jnp.exp(s - m)
        l = jnp.sum(p, axis=-1, keepdims=True)
        outs.append((_dot(p.astype(BF16), kv_ref[:, vs]) / l).astype(BF16))
    o = jnp.concatenate(outs, axis=-1)
    o_ref[...] = h + _dot(o, wo_ref[...])


def xattn_layer(h, mem, norm_g, mem_norm_g, w_q, w_kv, w_o, tq=512):
    bsz, seq, d = h.shape
    mlen = mem.shape[1]
    tq = min(tq, seq)
    kv = norm_matmul(mem.reshape(bsz * mlen, d), mem_norm_g, w_kv.astype(BF16), BF16, tm=512, tn=1024)
    kv = kv.reshape(bsz, mlen, 2 * d)
    return pl.pallas_call(
        _xattn_kernel,
        grid=(bsz, seq // tq),
        in_specs=[
            pl.BlockSpec((None, tq, d), lambda b, i: (b, i, 0)),
            pl.BlockSpec((1, d), lambda b, i: (0, 0)),
            pl.BlockSpec((d, d), lambda b, i: (0, 0)),
            pl.BlockSpec((None, mlen, 2 * d), lambda b, i: (b, 0, 0)),
            pl.BlockSpec((d, d), lambda b, i: (0, 0)),
        ],
        out_specs=pl.BlockSpec((None, tq, d), lambda b, i: (b, i, 0)),
        out_shape=jax.ShapeDtypeStruct((bsz, seq, d), F32),
        compiler_params=_cparams("parallel", "parallel"),
        name="mem_xattn",
    )(h, norm_g.astype(F32).reshape(1, d), w_q.astype(BF16), kv, w_o.astype(BF16))


SLAB = D_MODEL // LANES


def _router_kernel(h_ref, g_ref, wh_ref, wl_ref, br_ref, su_ref,
                   slab_ref, idx_ref, gate_ref, rank_ref, cnt_ref, run_ref, *, tm):
    i = pl.program_id(0)

    @pl.when(i == 0)
    def _():
        run_ref[...] = jnp.zeros_like(run_ref)

    hn = _rms(h_ref[...], g_ref[...])
    for s in range(SLAB):
        slab_ref[pl.ds(s, tm, stride=SLAB), :] = hn[:, s * LANES:(s + 1) * LANES]

    x_hi, x_lo = _split2(hn)
    wh = wh_ref[...]
    logits = _dot_nt(wh, x_hi) + _dot_nt(wh, x_lo) + _dot_nt(wl_ref[...], x_hi) + br_ref[...]

    rows = lax.broadcasted_iota(jnp.int32, (N_EXPERTS, tm), 0).astype(F32)
    tops, idxs, onehots = [], [], []
    cur = logits
    for _ in range(TOP_K):
        m = jnp.max(cur, axis=0, keepdims=True)
        idx = jnp.min(jnp.where(cur == m, rows, float(N_EXPERTS)), axis=0, keepdims=True)
        oh = rows == idx
        cur = jnp.where(oh, -jnp.inf, cur)
        tops.append(m)
        idxs.append(idx)
        onehots.append(oh)
    exps = [jnp.exp(t - tops[0]) for t in tops]
    denom = exps[0] + exps[1] + exps[2] + exps[3]
    gate_ref[...] = jnp.concatenate([e / denom for e in exps], axis=0)
    idx_ref[...] = jnp.concatenate(idxs, axis=0).astype(jnp.int32)

    oh_sum = jnp.zeros((N_EXPERTS, tm), F32)
    for oh in onehots:
        oh_sum = oh_sum + jnp.where(oh, 1.0, 0.0)
    run = run_ref[...]
    prefix = _dot(oh_sum.astype(BF16), su_ref[...]) + run[:, 0:1]
    ranks = [jnp.sum(jnp.where(oh, prefix, 0.0), axis=0, keepdims=True) for oh in onehots]
    rank_ref[...] = jnp.concatenate(ranks, axis=0).astype(jnp.int32)
    run_new = run + jnp.sum(oh_sum, axis=1, keepdims=True)
    run_ref[...] = run_new
    cnt_ref[...] = run_new


def moe_router(h2, norm_g, w_router, b_router, tm=ROUTER_TILE):
    t, d = h2.shape
    tm = min(tm, t)
    wt = w_router.astype(F32).T
    wh = wt.astype(BF16)
    wl = (wt - wh.astype(F32)).astype(BF16)
    su = (jnp.arange(tm)[:, None] < jnp.arange(tm)[None, :]).astype(BF16)
    return pl.pallas_call(
        functools.partial(_router_kernel, tm=tm),
        grid=(t // tm,),
        in_specs=[
            pl.BlockSpec((tm, d), lambda i: (i, 0)),
            pl.BlockSpec((1, d), lambda i: (0, 0)),
            pl.BlockSpec((N_EXPERTS, d), lambda i: (0, 0)),
            pl.BlockSpec((N_EXPERTS, d), lambda i: (0, 0)),
            pl.BlockSpec((N_EXPERTS, 1), lambda i: (0, 0)),
            pl.BlockSpec((tm, tm), lambda i: (0, 0)),
        ],
        out_specs=[
            pl.BlockSpec((tm * SLAB, LANES), lambda i: (i, 0)),
            pl.BlockSpec((TOP_K, tm), lambda i: (0, i)),
            pl.BlockSpec((TOP_K, tm), lambda i: (0, i)),
            pl.BlockSpec((TOP_K, tm), lambda i: (0, i)),
            pl.BlockSpec((N_EXPERTS, LANES), lambda i: (0, 0)),
        ],
        out_shape=[
            jax.ShapeDtypeStruct((t * SLAB, LANES), F32),
            jax.ShapeDtypeStruct((TOP_K, t), jnp.int32),
            jax.ShapeDtypeStruct((TOP_K, t), F32),
            jax.ShapeDtypeStruct((TOP_K, t), jnp.int32),
            jax.ShapeDtypeStruct((N_EXPERTS, LANES), F32),
        ],
        scratch_shapes=[pltpu.VMEM((N_EXPERTS, LANES), F32)],
        compiler_params=_cparams("arbitrary"),
        name="moe_router",
    )(h2, norm_g.astype(F32).reshape(1, d), wh, wl, b_router.astype(F32).reshape(N_EXPERTS, 1), su)


def _slab_rows(r):
    return pl.ds(pl.multiple_of(r * SLAB, SLAB), SLAB)


def _dispatch_kernel(dest_ref, pad_ref, slab_ref, xs_ref, zero_ref, sem, *, tm, bm, nb):
    def zero_fill(action):
        def per_expert(e, carry):
            off = pad_ref[e]
            plen = pad_ref[N_EXPERTS + e]
            for bit in PAD_BITS[PAD_BITS.index(bm // 2):]:
                present = (plen & bit) != 0

                @pl.when(present)
                def _(off=off, bit=bit):
                    action(pltpu.make_async_copy(zero_ref.at[pl.ds(0, bit * SLAB), :],
                                                 xs_ref.at[pl.ds(pl.multiple_of(off * SLAB, SLAB), bit * SLAB), :],
                                                 sem))
                off = off + jnp.where(present, bit, 0)
            return carry

        lax.fori_loop(0, N_EXPERTS, per_expert, 0)

        def per_piece(p, carry):
            row = pl.multiple_of(p * PAD_PIECE * SLAB, PAD_PIECE * SLAB)
            action(pltpu.make_async_copy(zero_ref, xs_ref.at[pl.ds(row, PAD_PIECE * SLAB), :], sem))
            return carry

        lax.fori_loop(pad_ref[2 * N_EXPERTS] * (bm // PAD_PIECE), nb * (bm // PAD_PIECE), per_piece, 0)

    @pl.when(pl.program_id(0) == 0)
    def _():
        zero_ref[...] = jnp.zeros_like(zero_ref)
        zero_fill(lambda copy: copy.start())
        zero_fill(lambda copy: copy.wait())

    def row_copy(t, d):
        return pltpu.make_async_copy(slab_ref.at[_slab_rows(t), :], xs_ref.at[_slab_rows(d), :], sem)

    def issue(g, carry):
        for u in range(DMA_ISSUE_UNROLL):
            t = g * DMA_ISSUE_UNROLL + u
            for k in range(TOP_K):
                row_copy(t, dest_ref[0, 0, k * tm + t]).start(priority=k % 2)
        return carry

    lax.fori_loop(0, tm // DMA_ISSUE_UNROLL, issue, 0)

    for k in range(TOP_K):
        pltpu.make_async_copy(slab_ref, xs_ref.at[pl.ds(0, tm * SLAB), :], sem).wait()


def moe_dispatch(dest_tiles, pad_info, slab, p_rows, tm, bm):
    t = slab.shape[0] // SLAB
    assert bm % PAD_PIECE == 0 and bm // 2 in PAD_BITS
    return pl.pallas_call(
        functools.partial(_dispatch_kernel, tm=tm, bm=bm, nb=p_rows // bm),
        grid=(t // tm,),
        in_specs=[
            pl.BlockSpec((1, 1, TOP_K * tm), lambda i: (i, 0, 0), memory_space=pltpu.SMEM),
            pl.BlockSpec(memory_space=pltpu.SMEM),
            pl.BlockSpec((tm * SLAB, LANES), lambda i: (i, 0)),
        ],
        out_specs=pl.BlockSpec(memory_space=pl.ANY),
        out_shape=jax.ShapeDtypeStruct((p_rows * SLAB, LANES), F32),
        scratch_shapes=[pltpu.VMEM((PAD_PIECE * SLAB, LANES), F32), pltpu.SemaphoreType.DMA(())],
        compiler_params=_cparams("arbitrary"),
        name="moe_dispatch",
    )(dest_tiles, pad_info, slab)


def _expert_kernel(blk_e_ref, xs_ref, wgu_ref, bgu_ref, wd_ref, bd_ref, ys_ref, wgu_b_ref, wd_b_ref, *, bm, nb):
    i = pl.program_id(0)
    prev_e = blk_e_ref[jnp.maximum(i - 1, 0)]
    in_use = i < blk_e_ref[nb]

    @pl.when(in_use & ((i == 0) | (blk_e_ref[i] != prev_e)))
    def _():
        wgu_b_ref[...] = wgu_ref[...].astype(BF16)
        wd_b_ref[...] = wd_ref[...].astype(BF16)

    @pl.when(in_use)
    def _():
        x = jnp.concatenate([xs_ref[pl.ds(s, bm, stride=SLAB), :] for s in range(SLAB)], axis=-1)
        gu = _dot(x.astype(BF16), wgu_b_ref[...]) + bgu_ref[...]
        gate = jnp.minimum(gu[:, :D_FF], SWIGLU_LIMIT)
        up = jnp.clip(gu[:, D_FF:], -SWIGLU_LIMIT, SWIGLU_LIMIT)
        act = (up + 1.0) * (gate * jax.nn.sigmoid(gate * SWIGLU_ALPHA))
        y = _dot(act.astype(BF16), wd_b_ref[...]) + bd_ref[...]
        for s in range(SLAB):
            ys_ref[pl.ds(s, bm, stride=SLAB), :] = y[:, s * LANES:(s + 1) * LANES]

    @pl.when(jnp.logical_not(in_use))
    def _():
        ys_ref[...] = jnp.zeros_like(ys_ref)


def moe_experts(blk_e, xs, w_gate_up, b_gate_up, w_down, b_down, layer, bm):
    nb = blk_e.shape[0] - 1
    d = D_MODEL
    grid_spec = pltpu.PrefetchScalarGridSpec(
        num_scalar_prefetch=1,
        grid=(nb,),
        in_specs=[
            pl.BlockSpec((bm * SLAB, LANES), lambda i, be: (i, 0)),
            pl.BlockSpec((None, None, d, 2 * D_FF), lambda i, be: (layer, be[i], 0, 0)),
            pl.BlockSpec((None, 1, 2 * D_FF), lambda i, be: (be[i], 0, 0)),
            pl.BlockSpec((None, None, D_FF, d), lambda i, be: (layer, be[i], 0, 0)),
            pl.BlockSpec((None, 1, d), lambda i, be: (be[i], 0, 0)),
        ],
        out_specs=pl.BlockSpec((bm * SLAB, LANES), lambda i, be: (i, 0)),
        scratch_shapes=[pltpu.VMEM((d, 2 * D_FF), BF16), pltpu.VMEM((D_FF, d), BF16)],
    )
    return pl.pallas_call(
        functools.partial(_expert_kernel, bm=bm, nb=nb),
        grid_spec=grid_spec,
        out_shape=jax.ShapeDtypeStruct((nb * bm * SLAB, LANES), F32),
        compiler_params=_cparams("arbitrary"),
        name="moe_experts",
    )(blk_e, xs, w_gate_up, b_gate_up.astype(F32).reshape(N_EXPERTS, 1, 2 * D_FF),
      w_down, b_down.astype(F32).reshape(N_EXPERTS, 1, d))


def _combine_kernel(dest_ref, dest_next_ref, gate_ref, h_ref, fg_ref, ys_ref, o_ref, buf_ref, sem,
                    *, tm, final_norm):
    i = pl.program_id(0)
    n = pl.num_programs(0)
    rows = TOP_K * tm * SLAB

    def issue_tile(d_ref, slot):
        base = slot * rows

        def issue(g, carry):
            for u in range(DMA_ISSUE_UNROLL):
                j = g * DMA_ISSUE_UNROLL + u
                pltpu.make_async_copy(
                    ys_ref.at[_slab_rows(d_ref[0, 0, j]), :],
                    buf_ref.at[pl.ds(pl.multiple_of(base + j * SLAB, SLAB), SLAB), :],
                    sem.at[slot]).start(priority=u % 2)
            return carry

        lax.fori_loop(0, TOP_K * tm // DMA_ISSUE_UNROLL, issue, 0)

    slot = i % 2

    @pl.when(i == 0)
    def _():
        issue_tile(dest_ref, 0)

    @pl.when(i + 1 < n)
    def _():
        issue_tile(dest_next_ref, 1 - slot)

    base = pl.multiple_of(slot * rows, rows)
    pltpu.make_async_copy(ys_ref.at[pl.ds(0, rows), :], buf_ref.at[pl.ds(base, rows), :], sem.at[slot]).wait()

    gates = gate_ref[...]
    for s in range(SLAB):
        cols = slice(s * LANES, (s + 1) * LANES)
        acc = h_ref[:, cols]
        for k in range(TOP_K):
            acc = acc + gates[:, k:k + 1] * buf_ref[pl.ds(base + k * tm * SLAB + s, tm, stride=SLAB), :]
        o_ref[:, cols] = acc
    if final_norm:
        o_ref[...] = _rms(o_ref[...], fg_ref[...])


def moe_combine(dest_tiles, gates_col, h2, ys, tm, final_g=None):
    t, d = h2.shape
    nt = t // tm
    fg = jnp.ones((1, d), F32) if final_g is None else final_g.astype(F32).reshape(1, d)
    return pl.pallas_call(
        functools.partial(_combine_kernel, tm=tm, final_norm=final_g is not None),
        grid=(nt,),
        in_specs=[
            pl.BlockSpec((1, 1, TOP_K * tm), lambda i: (i, 0, 0), memory_space=pltpu.SMEM),
            pl.BlockSpec((1, 1, TOP_K * tm), lambda i: (jnp.minimum(i + 1, nt - 1), 0, 0),
                         memory_space=pltpu.SMEM),
            pl.BlockSpec((tm, TOP_K), lambda i: (i, 0)),
            pl.BlockSpec((tm, d), lambda i: (i, 0)),
            pl.BlockSpec((1, d), lambda i: (0, 0)),
            pl.BlockSpec(memory_space=pl.ANY),
        ],
        out_specs=pl.BlockSpec((tm, d), lambda i: (i, 0)),
        out_shape=jax.ShapeDtypeStruct((t, d), F32),
        scratch_shapes=[pltpu.VMEM((2 * TOP_K * tm * SLAB, LANES), F32), pltpu.SemaphoreType.DMA((2,))],
        compiler_params=_cparams("arbitrary"),
        name="moe_combine",
    )(dest_tiles, dest_tiles, gates_col, h2, fg, ys)


def _tile_major(a, tm):
    t = a.shape[1]
    return a.reshape(TOP_K, t // tm, tm).transpose(1, 0, 2).reshape(t // tm, 1, TOP_K * tm)


def moe_layer(h, norm_g, w_router, b_router, w_gate_up, b_gate_up, w_down, b_down, layer, final_g=None):
    bsz, seq, d = h.shape
    t = bsz * seq
    h2 = h.reshape(t, d)
    bm = MOE_ROWS_PER_BLOCK
    slab, idx, gates, rank, cnt = moe_router(h2, norm_g, w_router, b_router)

    counts = cnt[:, 0].astype(jnp.int32)
    padded = (counts + bm - 1) // bm * bm
    pend = jnp.cumsum(padded)
    pstart = pend - padded
    experts = jnp.arange(N_EXPERTS, dtype=jnp.int32)
    dest = rank + jnp.sum(jnp.where(idx[..., None] == experts, pstart, 0), axis=-1)
    n = t * TOP_K
    p_rows = -(-(n + N_EXPERTS * bm) // bm) * bm
    nb = p_rows // bm
    blk_start = jnp.arange(nb, dtype=jnp.int32) * bm
    blk_e = jnp.minimum(jnp.sum((pend[None, :] <= blk_start[:, None]).astype(jnp.int32), axis=1),
                        N_EXPERTS - 1)

    n_used = pend[N_EXPERTS - 1:] // bm
    pad_info = jnp.concatenate([pstart + counts, padded - counts, n_used]).astype(jnp.int32)
    blk_meta = jnp.concatenate([blk_e, n_used]).astype(jnp.int32)

    td = min(DISPATCH_TILE, t)
    xs = moe_dispatch(_tile_major(dest, td), pad_info, slab, p_rows, td, bm)
    ys = moe_experts(blk_meta, xs, w_gate_up.astype(F32), b_gate_up, w_down.astype(F32), b_down, layer, bm)
    tc = min(COMBINE_TILE, t)
    out = moe_combine(_tile_major(dest, tc), gates.T, h2, ys, tc, final_g)
    return out.reshape(bsz, seq, d)


def kernel(x, mem, mixer_norm, xattn_norm, mem_norm, ffn_norm, ssd_w_in, ssd_conv_w, ssd_conv_b, ssd_dt_bias, ssd_a_log, ssd_d, ssd_norm, ssd_w_out, da_w_qkv, da_lambda_q1, da_lambda_k1, da_lambda_q2, da_lambda_k2, da_subln, da_w_o, xa_w_q, xa_w_kv, xa_w_o, moe_w_router, moe_b_router, moe_w_gate_up, moe_b_gate_up, moe_w_down, moe_b_down, final_norm):
    depth = mixer_norm.shape[0]
    bsz, seq, d = x.shape
    h = x
    for i in range(depth):
        j = i // N_MIXERS
        if i % N_MIXERS == 0:
            h = ssd_layer(h, mixer_norm[i], ssd_w_in[j], ssd_conv_w[j], ssd_conv_b[j], ssd_dt_bias[j],
                          ssd_a_log[j], ssd_d[j], ssd_norm[j], ssd_w_out[j])
        else:
            h = da_layer(h, mixer_norm[i], da_w_qkv[j], da_lambda_q1[j], da_lambda_k1[j], da_lambda_q2[j],
                         da_lambda_k2[j], da_subln[j], da_w_o[j], i)
        h = xattn_layer(h, mem, xattn_norm[i], mem_norm[i], xa_w_q[i], xa_w_kv[i], xa_w_o[i])
        h = moe_layer(h, ffn_norm[i], moe_w_router[i], moe_b_router[i], moe_w_gate_up,
                      moe_b_gate_up[i], moe_w_down, moe_b_down[i], i,
                      final_g=final_norm if i == depth - 1 else None)
    return h
```

```python
import functools
import math

import jax
import jax.numpy as jnp
from jax import lax
from jax.experimental import pallas as pl
from jax.experimental.pallas import tpu as pltpu

F32 = jnp.float32
BF16 = jnp.bfloat16

D_MODEL = 1024
RMS_EPS = 1e-5
LOG2_E = 1.4426950408889634
N_MIXERS = 2

SSD_D_INNER = 2048
SSD_HEADDIM = 64
SSD_N_HEADS = 32
SSD_N_GROUPS = 4
SSD_HEADS_PER_GROUP = 8
SSD_D_STATE = 128
SSD_D_CONV = 4
SSD_CHUNK = 128
SSD_GN = 512
SSD_CONV_DIM = 3072
SSD_GROUP_WIDTH = SSD_D_INNER // SSD_N_GROUPS

DA_HEAD_DIM = 64
DA_N_HEADS = 8
DA_KV_UNROLL = 4

XA_N_HEADS = 4
XA_HEAD_DIM = 256

N_EXPERTS = 32
TOP_K = 4
D_FF = 1024
SWIGLU_LIMIT = 7.0
SWIGLU_ALPHA = 1.702

LANES = 128
SUBLANES = 8
VMEM_LIMIT_BYTES = 56 * 1024 * 1024

MOE_ROWS_PER_BLOCK = 512
ROUTER_TILE = 512
DISPATCH_TILE = 256
COMBINE_TILE = 256
DMA_ISSUE_UNROLL = 8
PAD_PIECE = 256
PAD_BITS = (256, 128, 64, 32, 16, 8, 4, 2, 1)


def _cparams(*sem):
    return pltpu.CompilerParams(dimension_semantics=sem, vmem_limit_bytes=VMEM_LIMIT_BYTES)


def _rms(x, g):
    ms = jnp.mean(x * x, axis=-1, keepdims=True)
    return x * lax.rsqrt(ms + RMS_EPS) * g


def _dot(a, b):
    return jnp.dot(a, b, preferred_element_type=F32)


def _dot_nt(a, b):
    return lax.dot_general(a, b, (((1,), (1,)), ((), ())), preferred_element_type=F32)


def _split2(x):
    hi = x.astype(BF16)
    lo = (x - hi.astype(F32)).astype(BF16)
    return hi, lo


def _split3(x):
    hi = x.astype(BF16)
    r = x - hi.astype(F32)
    mid = r.astype(BF16)
    lo = (r - mid.astype(F32)).astype(BF16)
    return hi, mid, lo


def _norm_mm_kernel(x_ref, g_ref, w_ref, o_ref, xn_ref):
    @pl.when(pl.program_id(1) == 0)
    def _():
        xn_ref[...] = _rms(x_ref[...], g_ref[...]).astype(BF16)

    o_ref[...] = _dot(xn_ref[...], w_ref[...]).astype(o_ref.dtype)


def norm_matmul(x, g, w, out_dtype, tm, tn):
    m, k = x.shape
    n = w.shape[1]
    tm = min(tm, m)
    tn = min(tn, n)
    return pl.pallas_call(
        _norm_mm_kernel,
        grid=(m // tm, n // tn),
        in_specs=[
            pl.BlockSpec((tm, k), lambda i, j: (i, 0)),
            pl.BlockSpec((1, k), lambda i, j: (0, 0)),
            pl.BlockSpec((k, tn), lambda i, j: (0, j)),
        ],
        out_specs=pl.BlockSpec((tm, tn), lambda i, j: (i, j)),
        out_shape=jax.ShapeDtypeStruct((m, n), out_dtype),
        scratch_shapes=[pltpu.VMEM((tm, k), BF16)],
        compiler_params=_cparams("parallel", "arbitrary"),
        name="norm_matmul",
    )(x, g.reshape(1, k), w)


def _mm_res_kernel(x_ref, w_ref, r_ref, o_ref):
    o_ref[...] = r_ref[...] + _dot(x_ref[...], w_ref[...])


def matmul_residual(x, w, res, tm=512):
    m, k = x.shape
    n = w.shape[1]
    tm = min(tm, m)
    return pl.pallas_call(
        _mm_res_kernel,
        grid=(m // tm,),
        in_specs=[
            pl.BlockSpec((tm, k), lambda i: (i, 0)),
            pl.BlockSpec((k, n), lambda i: (0, 0)),
            pl.BlockSpec((tm, n), lambda i: (i, 0)),
        ],
        out_specs=pl.BlockSpec((tm, n), lambda i: (i, 0)),
        out_shape=jax.ShapeDtypeStruct((m, n), F32),
        compiler_params=_cparams("parallel"),
        name="matmul_residual",
    )(x, w, res)


def _norm_kernel(x_ref, g_ref, o_ref):
    o_ref[...] = _rms(x_ref[...], g_ref[...])


def final_norm_call(x, g, tm=1024):
    m, k = x.shape
    tm = min(tm, m)
    return pl.pallas_call(
        _norm_kernel,
        grid=(m // tm,),
        in_specs=[pl.BlockSpec((tm, k), lambda i: (i, 0)), pl.BlockSpec((1, k), lambda i: (0, 0))],
        out_specs=pl.BlockSpec((tm, k), lambda i: (i, 0)),
        out_shape=jax.ShapeDtypeStruct((m, k), F32),
        compiler_params=_cparams("parallel"),
        name="final_norm",
    )(x, g.reshape(1, k))


def _ssd_kernel(z0_ref, z1_ref, x0_ref, x1_ref, bc_ref, dtr_ref, h_ref,
                convw_ref, convb_ref, dtb_ref, alog_ref, dexp_ref, ng_ref, expand_ref, wout_ref,
                o_ref, state_ref, ext_ref):
    L = SSD_CHUNK
    GW = SSD_GROUP_WIDTH
    c = pl.program_id(1)

    @pl.when(c == 0)
    def _():
        state_ref[...] = jnp.zeros_like(state_ref)
        ext_ref[0:L, :] = jnp.zeros((L, SSD_CONV_DIM), BF16)

    srow = lax.broadcasted_iota(jnp.int32, (L, 2 * L), 0)
    scol = lax.broadcasted_iota(jnp.int32, (L, 2 * L), 1)
    shifts = [jnp.where(scol == srow + (L - (SSD_D_CONV - 1) + k), 1.0, 0.0).astype(BF16)
              for k in range(SSD_D_CONV - 1)]
    pieces = []
    for blk, ref in enumerate((x0_ref, x1_ref, bc_ref)):
        cols = slice(blk * 1024, (blk + 1) * 1024)
        cur = ref[...]
        ext_ref[L:2 * L, cols] = cur
        both = ext_ref[:, cols]
        acc = convb_ref[:, cols] + convw_ref[SSD_D_CONV - 1:SSD_D_CONV, cols] * cur.astype(F32)
        for k in range(SSD_D_CONV - 1):
            acc = acc + convw_ref[k:k + 1, cols] * _dot(shifts[k], both)
        pieces.append(acc * jax.nn.sigmoid(acc))
        ext_ref[0:L, cols] = cur
    xs = jnp.concatenate(pieces[:2], axis=-1)
    b_all = pieces[2][:, :SSD_GN]
    c_all = pieces[2][:, SSD_GN:]

    dtr = dtr_ref[...] + dtb_ref[...]
    dt = jnp.maximum(dtr, 0.0) + jnp.log1p(jnp.exp(-jnp.abs(dtr)))
    a = -jnp.exp(alog_ref[...])
    da = dt * a
    row = lax.broadcasted_iota(jnp.int32, (L, L), 0)
    col = lax.broadcasted_iota(jnp.int32, (L, L), 1)
    causal = col <= row
    tril = jnp.where(causal, 1.0, 0.0).astype(BF16)
    d_hi, d_mid, d_lo = _split3(da)
    a_cum = _dot(tril, d_hi) + _dot(tril, d_mid) + _dot(tril, d_lo)
    a_cum_t = a_cum.T

    expand = expand_ref[...]
    dt_e = _dot(dt.astype(BF16), expand)
    w_e = _dot((dt * jnp.exp(a_cum[L - 1:L, :] - a_cum)).astype(BF16), expand)
    e_hi, e_lo = _split2(jnp.exp(a_cum))
    exp_acum_e = _dot(e_hi, expand) + _dot(e_lo, expand)
    cd_e = exp_acum_e[L - 1:L, :]

    xd_b = (xs * dt_e).astype(BF16)
    xdw_b = (xs * w_e).astype(BF16)

    lane = lax.broadcasted_iota(jnp.int32, (L, LANES), 1)
    first_half = lane < SSD_HEADDIM

    y_parts = []
    for g in range(SSD_N_GROUPS):
        gs = slice(g * SSD_D_STATE, (g + 1) * SSD_D_STATE)
        gw = slice(g * GW, (g + 1) * GW)
        b_g = b_all[:, gs]
        c_g = c_all[:, gs].astype(BF16)
        cb = _dot_nt(c_g, b_g.astype(BF16))
        y_pairs = []
        for jp in range(SSD_HEADS_PER_GROUP // 2):
            res = []
            pair_col = g * GW + jp * LANES
            xd_pair = xd_b[:, pair_col:pair_col + LANES]
            for sub in range(2):
                hd = g * SSD_HEADS_PER_GROUP + jp * 2 + sub
                diff = a_cum[:, hd:hd + 1] - a_cum_t[hd:hd + 1, :]
                dec = jnp.exp(jnp.where(causal, diff, -jnp.inf))
                res.append(_dot((cb * dec).astype(BF16), xd_pair))
            y_pairs.append(jnp.where(first_half, res[0], res[1]))
        y_diag = jnp.concatenate(y_pairs, axis=-1)
        st = state_ref[g]
        y_off = _dot(c_g, st.astype(BF16)) * exp_acum_e[:, gw]
        state_ref[g] = st * cd_e[:, gw] + _dot(b_g.T.astype(BF16), xdw_b[:, gw])
        y_parts.append(y_diag + y_off)
    y = jnp.concatenate(y_parts, axis=-1) + dexp_ref[...] * xs

    z = jnp.concatenate([z0_ref[...], z1_ref[...]], axis=-1).astype(F32)
    u = y * (z * jax.nn.sigmoid(z))
    u_parts = []
    for g in range(SSD_N_GROUPS):
        gw = slice(g * GW, (g + 1) * GW)
        ug = u[:, gw]
        ms = jnp.mean(ug * ug, axis=-1, keepdims=True)
        u_parts.append(ug * lax.rsqrt(ms + RMS_EPS) * ng_ref[:, gw])
    un = jnp.concatenate(u_parts, axis=-1).astype(BF16)
    o_ref[...] = h_ref[...] + _dot(un, wout_ref[...])


def ssd_core(zxbc, dt_raw, h, conv_w, conv_b, dt_bias, a_log, d_skip, norm_g, w_out_b):
    bsz, seq, _ = h.shape
    L = SSD_CHUNK
    nc = seq // L
    pad_heads = LANES - SSD_N_HEADS
    dtb = jnp.pad(dt_bias.astype(F32), (0, pad_heads)).reshape(1, LANES)
    alog = jnp.pad(a_log.astype(F32), (0, pad_heads)).reshape(1, LANES)
    dexp = jnp.repeat(d_skip.astype(F32), SSD_HEADDIM).reshape(1, SSD_D_INNER)
    head_of_col = jnp.arange(SSD_D_INNER, dtype=jnp.int32) // SSD_HEADDIM
    expand = (jnp.arange(LANES, dtype=jnp.int32)[:, None] == head_of_col[None, :]).astype(BF16)

    def zx_spec(k):
        return pl.BlockSpec((None, L, 1024), lambda b, c, k=k: (b, c, k))

    def const_spec(shape):
        return pl.BlockSpec(shape, lambda b, c: (0,) * len(shape))

    return pl.pallas_call(
        _ssd_kernel,
        grid=(bsz, nc),
        in_specs=[
            zx_spec(0), zx_spec(1), zx_spec(2), zx_spec(3), zx_spec(4),
            pl.BlockSpec((None, L, LANES), lambda b, c: (b, c, 0)),
            pl.BlockSpec((None, L, D_MODEL), lambda b, c: (b, c, 0)),
            const_spec((SSD_D_CONV, SSD_CONV_DIM)),
            const_spec((1, SSD_CONV_DIM)),
            const_spec((1, LANES)),
            const_spec((1, LANES)),
            const_spec((1, SSD_D_INNER)),
            const_spec((1, SSD_D_INNER)),
            const_spec((LANES, SSD_D_INNER)),
            const_spec((SSD_D_INNER, D_MODEL)),
        ],
        out_specs=pl.BlockSpec((None, L, D_MODEL), lambda b, c: (b, c, 0)),
        out_shape=jax.ShapeDtypeStruct((bsz, seq, D_MODEL), F32),
        scratch_shapes=[
            pltpu.VMEM((SSD_N_GROUPS, SSD_D_STATE, SSD_GROUP_WIDTH), F32),
            pltpu.VMEM((2 * L, SSD_CONV_DIM), BF16),
        ],
        compiler_params=_cparams("parallel", "arbitrary"),
        name="ssd_core",
    )(zxbc, zxbc, zxbc, zxbc, zxbc, dt_raw, h,
      conv_w.astype(F32), conv_b.astype(F32).reshape(1, SSD_CONV_DIM), dtb, alog, dexp,
      norm_g.astype(F32).reshape(1, SSD_D_INNER), expand, w_out_b)


def ssd_layer(h, norm_g_in, w_in, conv_w, conv_b, dt_bias, a_log, d_skip, norm_g, w_out):
    bsz, seq, d = h.shape
    h2 = h.reshape(bsz * seq, d)
    n_main = SSD_D_INNER + SSD_CONV_DIM
    w_main = w_in[:, :n_main].astype(BF16)
    w_dt = jnp.pad(w_in[:, n_main:], ((0, 0), (0, LANES - SSD_N_HEADS))).astype(BF16)
    zxbc = norm_matmul(h2, norm_g_in, w_main, BF16, tm=1024, tn=1024)
    dt_raw = norm_matmul(h2, norm_g_in, w_dt, F32, tm=1024, tn=LANES)
    return ssd_core(zxbc.reshape(bsz, seq, n_main), dt_raw.reshape(bsz, seq, LANES), h,
                    conv_w, conv_b, dt_bias, a_log, d_skip, norm_g, w_out.astype(BF16))


def _da_kernel(lq1_ref, lk1_ref, lq2_ref, lk2_ref, sub_ref, q_ref, k_ref, v_ref, o_ref,
               acc1_ref, acc2_ref, m1_ref, m2_ref, *, tq, lambda_init):
    i = pl.program_id(2)
    q = q_ref[...]
    lane = lax.broadcasted_iota(jnp.int32, (tq, LANES), 1)
    qs = (q.astype(F32) * (DA_HEAD_DIM ** -0.5 * LOG2_E)).astype(BF16)
    zero = jnp.zeros_like(qs)
    q_maps = (jnp.where(lane < DA_HEAD_DIM, qs, zero), jnp.where(lane >= DA_HEAD_DIM, qs, zero))
    states = ((m1_ref, acc1_ref), (m2_ref, acc2_ref))

    for m_ref, acc_ref in states:
        m_ref[...] = jnp.full((tq, LANES), -jnp.inf, F32)
        acc_ref[...] = jnp.zeros((tq, 2 * LANES), F32)

    ones = jnp.ones((tq, LANES), BF16)

    def step(j, masked):
        start = pl.multiple_of(j * tq, tq)
        kt = k_ref[pl.ds(start, tq), :]
        v_aug = jnp.concatenate([v_ref[pl.ds(start, tq), :], ones], axis=1)
        for qm, (m_ref, acc_ref) in zip(q_maps, states):
            s = _dot_nt(qm, kt)
            if masked:
                r = lax.broadcasted_iota(jnp.int32, (tq, tq), 0)
                cidx = lax.broadcasted_iota(jnp.int32, (tq, tq), 1)
                s = jnp.where(cidx <= r, s, -jnp.inf)
            m_old = m_ref[...]
            m_new = jnp.maximum(m_old, jnp.max(s, axis=-1, keepdims=True))
            alpha = jnp.exp2(m_old - m_new)
            p = jnp.exp2(s - jnp.concatenate([m_new] * (tq // LANES), axis=1))
            acc_ref[...] = (jnp.concatenate([alpha, alpha], axis=1) * acc_ref[...]
                            + _dot(p.astype(BF16), v_aug))
            m_ref[...] = m_new

    def body(jj, carry):
        for u in range(DA_KV_UNROLL):
            step(DA_KV_UNROLL * jj + u, False)
        return carry

    lax.fori_loop(0, i // DA_KV_UNROLL, body, 0)

    for rem in range(DA_KV_UNROLL):
        @pl.when(i % DA_KV_UNROLL == rem)
        def _(rem=rem):
            for u in range(rem):
                step(i - rem + u, False)
            step(i, True)

    lam = (jnp.exp(jnp.sum(lq1_ref[...] * lk1_ref[...], axis=-1, keepdims=True))
           - jnp.exp(jnp.sum(lq2_ref[...] * lk2_ref[...], axis=-1, keepdims=True)) + lambda_init)
    o1 = acc1_ref[:, :LANES] / acc1_ref[:, LANES:]
    o2 = acc2_ref[:, :LANES] / acc2_ref[:, LANES:]
    o = _rms(o1 - lam * o2, sub_ref[...]) * (1.0 - lambda_init)
    o_ref[...] = o.astype(o_ref.dtype)


def diff_attention_core(qkv, lq1, lk1, lq2, lk2, subln_g, layer_idx, tq=512):
    bsz, seq, _ = qkv.shape
    tq = min(tq, seq)
    lambda_init = 0.8 - 0.6 * math.exp(-0.3 * layer_idx)
    nh = DA_N_HEADS

    def vec_spec(n):
        return pl.BlockSpec((1, n), lambda b, h, i: (0, 0))

    return pl.pallas_call(
        functools.partial(_da_kernel, tq=tq, lambda_init=lambda_init),
        grid=(bsz, nh, seq // tq),
        in_specs=[
            vec_spec(DA_HEAD_DIM), vec_spec(DA_HEAD_DIM), vec_spec(DA_HEAD_DIM), vec_spec(DA_HEAD_DIM),
            vec_spec(LANES),
            pl.BlockSpec((None, tq, LANES), lambda b, h, i: (b, i, h)),
            pl.BlockSpec((None, seq, LANES), lambda b, h, i: (b, 0, nh + h)),
            pl.BlockSpec((None, seq, LANES), lambda b, h, i: (b, 0, 2 * nh + h)),
        ],
        out_specs=pl.BlockSpec((None, tq, LANES), lambda b, h, i: (b, i, h)),
        out_shape=jax.ShapeDtypeStruct((bsz, seq, D_MODEL), BF16),
        scratch_shapes=[
            pltpu.VMEM((tq, 2 * LANES), F32), pltpu.VMEM((tq, 2 * LANES), F32),
            pltpu.VMEM((tq, LANES), F32), pltpu.VMEM((tq, LANES), F32),
        ],
        compiler_params=_cparams("parallel", "parallel", "arbitrary"),
        name="diff_attention",
    )(lq1.astype(F32).reshape(1, -1), lk1.astype(F32).reshape(1, -1),
      lq2.astype(F32).reshape(1, -1), lk2.astype(F32).reshape(1, -1),
      subln_g.astype(F32).reshape(1, -1), qkv, qkv, qkv)


def da_layer(h, norm_g_in, w_qkv, lq1, lk1, lq2, lk2, subln_g, w_o, layer_idx):
    bsz, seq, d = h.shape
    h2 = h.reshape(bsz * seq, d)
    qkv = norm_matmul(h2, norm_g_in, w_qkv.astype(BF16), BF16, tm=1024, tn=1024)
    o = diff_attention_core(qkv.reshape(bsz, seq, 3 * d), lq1, lk1, lq2, lk2, subln_g, layer_idx)
    return matmul_residual(o.reshape(bsz * seq, d), w_o.astype(BF16), h2).reshape(bsz, seq, d)


def _xattn_kernel(h_ref, g_ref, wq_ref, kv_ref, wo_ref, o_ref):
    h = h_ref[...]
    hn = _rms(h, g_ref[...]).astype(BF16)
    scale = XA_HEAD_DIM ** -0.5
    q = (_dot(hn, wq_ref[...]) * scale).astype(BF16)
    outs = []
    for hd in range(XA_N_HEADS):
        cs = slice(hd * XA_HEAD_DIM, (hd + 1) * XA_HEAD_DIM)
        vs = slice(D_MODEL + hd * XA_HEAD_DIM, D_MODEL + (hd + 1) * XA_HEAD_DIM)
        s = _dot_nt(q[:, cs], kv_ref[:, cs])
        m = jnp.max(s, axis=-1, keepdims=True)
        p = jnp.exp(s - m)
        l = jnp.sum(p, axis=-1, keepdims=True)
        outs.append((_dot(p.astype(BF16), kv_ref[:, vs]) / l).astype(BF16))
    o = jnp.concatenate(outs, axis=-1)
    o_ref[...] = h + _dot(o, wo_ref[...])


def xattn_layer(h, mem, norm_g, mem_norm_g, w_q, w_kv, w_o, tq=512):
    bsz, seq, d = h.shape
    mlen = mem.shape[1]
    tq = min(tq, seq)
    kv = norm_matmul(mem.reshape(bsz * mlen, d), mem_norm_g, w_kv.astype(BF16), BF16, tm=512, tn=1024)
    kv = kv.reshape(bsz, mlen, 2 * d)
    return pl.pallas_call(
        _xattn_kernel,
        grid=(bsz, seq // tq),
        in_specs=[
            pl.BlockSpec((None, tq, d), lambda b, i: (b, i, 0)),
            pl.BlockSpec((1, d), lambda b, i: (0, 0)),
            pl.BlockSpec((d, d), lambda b, i: (0, 0)),
            pl.BlockSpec((None, mlen, 2 * d), lambda b, i: (b, 0, 0)),
            pl.BlockSpec((d, d), lambda b, i: (0, 0)),
        ],
        out_specs=pl.BlockSpec((None, tq, d), lambda b, i: (b, i, 0)),
        out_shape=jax.ShapeDtypeStruct((bsz, seq, d), F32),
        compiler_params=_cparams("parallel", "parallel"),
        name="mem_xattn",
    )(h, norm_g.astype(F32).reshape(1, d), w_q.astype(BF16), kv, w_o.astype(BF16))


SLAB = D_MODEL // LANES


def _router_kernel(h_ref, g_ref, wh_ref, wl_ref, br_ref, su_ref,
                   slab_ref, idx_ref, gate_ref, rank_ref, cnt_ref, run_ref, *, tm):
    i = pl.program_id(0)

    @pl.when(i == 0)
    def _():
        run_ref[...] = jnp.zeros_like(run_ref)

    hn = _rms(h_ref[...], g_ref[...])
    for s in range(SLAB):
        slab_ref[pl.ds(s, tm, stride=SLAB), :] = hn[:, s * LANES:(s + 1) * LANES]

    x_hi, x_lo = _split2(hn)
    wh = wh_ref[...]
    logits = _dot_nt(wh, x_hi) + _dot_nt(wh, x_lo) + _dot_nt(wl_ref[...], x_hi) + br_ref[...]

    rows = lax.broadcasted_iota(jnp.int32, (N_EXPERTS, tm), 0).astype(F32)
    tops, idxs, onehots = [], [], []
    cur = logits
    for _ in range(TOP_K):
        m = jnp.max(cur, axis=0, keepdims=True)
        idx = jnp.min(jnp.where(cur == m, rows, float(N_EXPERTS)), axis=0, keepdims=True)
        oh = rows == idx
        cur = jnp.where(oh, -jnp.inf, cur)
        tops.append(m)
        idxs.append(idx)
        onehots.append(oh)
    exps = [jnp.exp(t - tops[0]) for t in tops]
    denom = exps[0] + exps[1] + exps[2] + exps[3]
    gate_ref[...] = jnp.concatenate([e / denom for e in exps], axis=0)
    idx_ref[...] = jnp.concatenate(idxs, axis=0).astype(jnp.int32)

    oh_sum = jnp.zeros((N_EXPERTS, tm), F32)
    for oh in onehots:
        oh_sum = oh_sum + jnp.where(oh, 1.0, 0.0)
    run = run_ref[...]
    prefix = _dot(oh_sum.astype(BF16), su_ref[...]) + run[:, 0:1]
    ranks = [jnp.sum(jnp.where(oh, prefix, 0.0), axis=0, keepdims=True) for oh in onehots]
    rank_ref[...] = jnp.concatenate(ranks, axis=0).astype(jnp.int32)
    run_new = run + jnp.sum(oh_sum, axis=1, keepdims=True)
    run_ref[...] = run_new
    cnt_ref[...] = run_new


def moe_router(h2, norm_g, w_router, b_router, tm=ROUTER_TILE):
    t, d = h2.shape
    tm = min(tm, t)
    wt = w_router.astype(F32).T
    wh = wt.astype(BF16)
    wl = (wt - wh.astype(F32)).astype(BF16)
    su = (jnp.arange(tm)[:, None] < jnp.arange(tm)[None, :]).astype(BF16)
    return pl.pallas_call(
        functools.partial(_router_kernel, tm=tm),
        grid=(t // tm,),
        in_specs=[
            pl.BlockSpec((tm, d), lambda i: (i, 0)),
            pl.BlockSpec((1, d), lambda i: (0, 0)),
            pl.BlockSpec((N_EXPERTS, d), lambda i: (0, 0)),
            pl.BlockSpec((N_EXPERTS, d), lambda i: (0, 0)),
            pl.BlockSpec((N_EXPERTS, 1), lambda i: (0, 0)),
            pl.BlockSpec((tm, tm), lambda i: (0, 0)),
        ],
        out_specs=[
            pl.BlockSpec((tm * SLAB, LANES), lambda i: (i, 0)),
            pl.BlockSpec((TOP_K, tm), lambda i: (0, i)),
            pl.BlockSpec((TOP_K, tm), lambda i: (0, i)),
            pl.BlockSpec((TOP_K, tm), lambda i: (0, i)),
            pl.BlockSpec((N_EXPERTS, LANES), lambda i: (0, 0)),
        ],
        out_shape=[
            jax.ShapeDtypeStruct((t * SLAB, LANES), F32),
            jax.ShapeDtypeStruct((TOP_K, t), jnp.int32),
            jax.ShapeDtypeStruct((TOP_K, t), F32),
            jax.ShapeDtypeStruct((TOP_K, t), jnp.int32),
            jax.ShapeDtypeStruct((N_EXPERTS, LANES), F32),
        ],
        scratch_shapes=[pltpu.VMEM((N_EXPERTS, LANES), F32)],
        compiler_params=_cparams("arbitrary"),
        name="moe_router",
    )(h2, norm_g.astype(F32).reshape(1, d), wh, wl, b_router.astype(F32).reshape(N_EXPERTS, 1), su)


def _slab_rows(r):
    return pl.ds(pl.multiple_of(r * SLAB, SLAB), SLAB)


def _dispatch_kernel(dest_ref, pad_ref, slab_ref, xs_ref, zero_ref, sem, *, tm, bm, nb):
    def zero_fill(action):
        def per_expert(e, carry):
            off = pad_ref[e]
            plen = pad_ref[N_EXPERTS + e]
            for bit in PAD_BITS[PAD_BITS.index(bm // 2):]:
                present = (plen & bit) != 0

                @pl.when(present)
                def _(off=off, bit=bit):
                    action(pltpu.make_async_copy(zero_ref.at[pl.ds(0, bit * SLAB), :],
                                                 xs_ref.at[pl.ds(pl.multiple_of(off * SLAB, SLAB), bit * SLAB), :],
                                                 sem))
                off = off + jnp.where(present, bit, 0)
            return carry

        lax.fori_loop(0, N_EXPERTS, per_expert, 0)

        def per_piece(p, carry):
            row = pl.multiple_of(p * PAD_PIECE * SLAB, PAD_PIECE * SLAB)
            action(pltpu.make_async_copy(zero_ref, xs_ref.at[pl.ds(row, PAD_PIECE * SLAB), :], sem))
            return carry

        lax.fori_loop(pad_ref[2 * N_EXPERTS] * (bm // PAD_PIECE), nb * (bm // PAD_PIECE), per_piece, 0)

    @pl.when(pl.program_id(0) == 0)
    def _():
        zero_ref[...] = jnp.zeros_like(zero_ref)
        zero_fill(lambda copy: copy.start())
        zero_fill(lambda copy: copy.wait())

    def row_copy(t, d):
        return pltpu.make_async_copy(slab_ref.at[_slab_rows(t), :], xs_ref.at[_slab_rows(d), :], sem)

    def issue(g, carry):
        for u in range(DMA_ISSUE_UNROLL):
            t = g * DMA_ISSUE_UNROLL + u
            for k in range(TOP_K):
                row_copy(t, dest_ref[0, 0, k * tm + t]).start(priority=k % 2)
        return carry

    lax.fori_loop(0, tm // DMA_ISSUE_UNROLL, issue, 0)

    for k in range(TOP_K):
        pltpu.make_async_copy(slab_ref, xs_ref.at[pl.ds(0, tm * SLAB), :], sem).wait()


def moe_dispatch(dest_tiles, pad_info, slab, p_rows, tm, bm):
    t = slab.shape[0] // SLAB
    assert bm % PAD_PIECE == 0 and bm // 2 in PAD_BITS
    return pl.pallas_call(
        functools.partial(_dispatch_kernel, tm=tm, bm=bm, nb=p_rows // bm),
        grid=(t // tm,),
        in_specs=[
            pl.BlockSpec((1, 1, TOP_K * tm), lambda i: (i, 0, 0), memory_space=pltpu.SMEM),
            pl.BlockSpec(memory_space=pltpu.SMEM),
            pl.BlockSpec((tm * SLAB, LANES), lambda i: (i, 0)),
        ],
        out_specs=pl.BlockSpec(memory_space=pl.ANY),
        out_shape=jax.ShapeDtypeStruct((p_rows * SLAB, LANES), F32),
        scratch_shapes=[pltpu.VMEM((PAD_PIECE * SLAB, LANES), F32), pltpu.SemaphoreType.DMA(())],
        compiler_params=_cparams("arbitrary"),
        name="moe_dispatch",
    )(dest_tiles, pad_info, slab)


def _expert_kernel(blk_e_ref, xs_ref, wgu_ref, bgu_ref, wd_ref, bd_ref, ys_ref, wgu_b_ref, wd_b_ref, *, bm, nb):
    i = pl.program_id(0)
    prev_e = blk_e_ref[jnp.maximum(i - 1, 0)]
    in_use = i < blk_e_ref[nb]

    @pl.when(in_use & ((i == 0) | (blk_e_ref[i] != prev_e)))
    def _():
        wgu_b_ref[...] = wgu_ref[...].astype(BF16)
        wd_b_ref[...] = wd_ref[...].astype(BF16)

    @pl.when(in_use)
    def _():
        x = jnp.concatenate([xs_ref[pl.ds(s, bm, stride=SLAB), :] for s in range(SLAB)], axis=-1)
        gu = _dot(x.astype(BF16), wgu_b_ref[...]) + bgu_ref[...]
        gate = jnp.minimum(gu[:, :D_FF], SWIGLU_LIMIT)
        up = jnp.clip(gu[:, D_FF:], -SWIGLU_LIMIT, SWIGLU_LIMIT)
        act = (up + 1.0) * (gate * jax.nn.sigmoid(gate * SWIGLU_ALPHA))
        y = _dot(act.astype(BF16), wd_b_ref[...]) + bd_ref[...]
        for s in range(SLAB):
            ys_ref[pl.ds(s, bm, stride=SLAB), :] = y[:, s * LANES:(s + 1) * LANES]

    @pl.when(jnp.logical_not(in_use))
    def _():
        ys_ref[...] = jnp.zeros_like(ys_ref)


def moe_experts(blk_e, xs, w_gate_up, b_gate_up, w_down, b_down, layer, bm):
    nb = blk_e.shape[0] - 1
    d = D_MODEL
    grid_spec = pltpu.PrefetchScalarGridSpec(
        num_scalar_prefetch=1,
        grid=(nb,),
        in_specs=[
            pl.BlockSpec((bm * SLAB, LANES), lambda i, be: (i, 0)),
            pl.BlockSpec((None, None, d, 2 * D_FF), lambda i, be: (layer, be[i], 0, 0)),
            pl.BlockSpec((None, 1, 2 * D_FF), lambda i, be: (be[i], 0, 0)),
            pl.BlockSpec((None, None, D_FF, d), lambda i, be: (layer, be[i], 0, 0)),
            pl.BlockSpec((None, 1, d), lambda i, be: (be[i], 0, 0)),
        ],
        out_specs=pl.BlockSpec((bm * SLAB, LANES), lambda i, be: (i, 0)),
        scratch_shapes=[pltpu.VMEM((d, 2 * D_FF), BF16), pltpu.VMEM((D_FF, d), BF16)],
    )
    return pl.pallas_call(
        functools.partial(_expert_kernel, bm=bm, nb=nb),
        grid_spec=grid_spec,
        out_shape=jax.ShapeDtypeStruct((nb * bm * SLAB, LANES), F32),
        compiler_params=_cparams("arbitrary"),
        name="moe_experts",
    )(blk_e, xs, w_gate_up, b_gate_up.astype(F32).reshape(N_EXPERTS, 1, 2 * D_FF),
      w_down, b_down.astype(F32).reshape(N_EXPERTS, 1, d))


def _combine_kernel(dest_ref, dest_next_ref, gate_ref, h_ref, fg_ref, ys_ref, o_ref, buf_a, buf_b, sem,
                    *, tm, final_norm):
    i = pl.program_id(0)
    n = pl.num_programs(0)
    n_rows = TOP_K * tm

    def row_copy(d_ref, j, buf, s):
        return pltpu.make_async_copy(ys_ref.at[_slab_rows(d_ref[0, 0, j]), :],
                                     buf.at[pl.ds(j * SLAB, SLAB), :], s)

    def wait_tile(buf, s):
        pltpu.make_async_copy(ys_ref.at[pl.ds(0, n_rows * SLAB), :], buf, s).wait()

    @pl.when(i == 0)
    def _():
        def issue(g, carry):
            for u in range(DMA_ISSUE_UNROLL):
                j = g * DMA_ISSUE_UNROLL + u
                pltpu.make_async_copy(ys_ref.at[_slab_rows(dest_ref[0, 0, j]), :],
                                      buf_a.at[_slab_rows(j), :], sem.at[0]).start(priority=u % 2)
            return carry
        lax.fori_loop(0, n_rows // DMA_ISSUE_UNROLL, issue, 0)

    def run(cur, cur_sem, nxt, nxt_sem):
        wait_tile(cur, cur_sem)
        gates = gate_ref[...]
        per_block = n_rows // SLAB
        for s in range(SLAB):
            for j in range(s * per_block, (s + 1) * per_block):
                row_copy(dest_next_ref, j, nxt, nxt_sem).start(priority=j % 2)
            cols = slice(s * LANES, (s + 1) * LANES)
            acc = h_ref[:, cols]
            for k in range(TOP_K):
                acc = acc + gates[:, k:k + 1] * cur[pl.ds(k * tm * SLAB + s, tm, stride=SLAB), :]
            o_ref[:, cols] = acc
        if final_norm:
            o_ref[...] = _rms(o_ref[...], fg_ref[...])

        @pl.when(i == n - 1)
        def _():
            wait_tile(nxt, nxt_sem)

    @pl.when(i % 2 == 0)
    def _():
        run(buf_a, sem.at[0], buf_b, sem.at[1])

    @pl.when(i % 2 == 1)
    def _():
        run(buf_b, sem.at[1], buf_a, sem.at[0])


def moe_combine(dest_tiles, gates_col, h2, ys, tm, final_g=None):
    t, d = h2.shape
    nt = t // tm
    fg = jnp.ones((1, d), F32) if final_g is None else final_g.astype(F32).reshape(1, d)
    return pl.pallas_call(
        functools.partial(_combine_kernel, tm=tm, final_norm=final_g is not None),
        grid=(nt,),
        in_specs=[
            pl.BlockSpec((1, 1, TOP_K * tm), lambda i: (i, 0, 0), memory_space=pltpu.SMEM),
            pl.BlockSpec((1, 1, TOP_K * tm), lambda i: (jnp.minimum(i + 1, nt - 1), 0, 0),
                         memory_space=pltpu.SMEM),
            pl.BlockSpec((tm, TOP_K), lambda i: (i, 0)),
            pl.BlockSpec((tm, d), lambda i: (i, 0)),
            pl.BlockSpec((1, d), lambda i: (0, 0)),
            pl.BlockSpec(memory_space=pl.ANY),
        ],
        out_specs=pl.BlockSpec((tm, d), lambda i: (i, 0)),
        out_shape=jax.ShapeDtypeStruct((t, d), F32),
        scratch_shapes=[pltpu.VMEM((TOP_K * tm * SLAB, LANES), F32), pltpu.VMEM((TOP_K * tm * SLAB, LANES), F32),
                        pltpu.SemaphoreType.DMA((2,))],
        compiler_params=_cparams("arbitrary"),
        name="moe_combine",
    )(dest_tiles, dest_tiles, gates_col, h2, fg, ys)


def _tile_major(a, tm):
    t = a.shape[1]
    return a.reshape(TOP_K, t // tm, tm).transpose(1, 0, 2).reshape(t // tm, 1, TOP_K * tm)


def moe_layer(h, norm_g, w_router, b_router, w_gate_up, b_gate_up, w_down, b_down, layer, final_g=None):
    bsz, seq, d = h.shape
    t = bsz * seq
    h2 = h.reshape(t, d)
    bm = MOE_ROWS_PER_BLOCK
    slab, idx, gates, rank, cnt = moe_router(h2, norm_g, w_router, b_router)

    counts = cnt[:, 0].astype(jnp.int32)
    padded = (counts + bm - 1) // bm * bm
    pend = jnp.cumsum(padded)
    pstart = pend - padded
    experts = jnp.arange(N_EXPERTS, dtype=jnp.int32)
    dest = rank + jnp.sum(jnp.where(idx[..., None] == experts, pstart, 0), axis=-1)
    n = t * TOP_K
    p_rows = -(-(n + N_EXPERTS * bm) // bm) * bm
    nb = p_rows // bm
    blk_start = jnp.arange(nb, dtype=jnp.int32) * bm
    blk_e = jnp.minimum(jnp.sum((pend[None, :] <= blk_start[:, None]).astype(jnp.int32), axis=1),
                        N_EXPERTS - 1)

    n_used = pend[N_EXPERTS - 1:] // bm
    pad_info = jnp.concatenate([pstart + counts, padded - counts, n_used]).astype(jnp.int32)
    blk_meta = jnp.concatenate([blk_e, n_used]).astype(jnp.int32)

    td = min(DISPATCH_TILE, t)
    xs = moe_dispatch(_tile_major(dest, td), pad_info, slab, p_rows, td, bm)
    ys = moe_experts(blk_meta, xs, w_gate_up.astype(F32), b_gate_up, w_down.astype(F32), b_down, layer, bm)
    tc = min(COMBINE_TILE, t)
    out = moe_combine(_tile_major(dest, tc), gates.T, h2, ys, tc, final_g)
    return out.reshape(bsz, seq, d)


def kernel(x, mem, mixer_norm, xattn_norm, mem_norm, ffn_norm, ssd_w_in, ssd_conv_w, ssd_conv_b, ssd_dt_bias, ssd_a_log, ssd_d, ssd_norm, ssd_w_out, da_w_qkv, da_lambda_q1, da_lambda_k1, da_lambda_q2, da_lambda_k2, da_subln, da_w_o, xa_w_q, xa_w_kv, xa_w_o, moe_w_router, moe_b_router, moe_w_gate_up, moe_b_gate_up, moe_w_down, moe_b_down, final_norm):
    depth = mixer_norm.shape[0]
    bsz, seq, d = x.shape
    h = x
    for i in range(depth):
        j = i // N_MIXERS
        if i % N_MIXERS == 0:
            h = ssd_layer(h, mixer_norm[i], ssd_w_in[j], ssd_conv_w[j], ssd_conv_b[j], ssd_dt_bias[j],
                          ssd_a_log[j], ssd_d[j], ssd_norm[j], ssd_w_out[j])
        else:
            h = da_layer(h, mixer_norm[i], da_w_qkv[j], da_lambda_q1[j], da_lambda_k1[j], da_lambda_q2[j],
                         da_lambda_k2[j], da_subln[j], da_w_o[j], i)
        h = xattn_layer(h, mem, xattn_norm[i], mem_norm[i], xa_w_q[i], xa_w_kv[i], xa_w_o[i])
        h = moe_layer(h, ffn_norm[i], moe_w_router[i], moe_b_router[i], moe_w_gate_up,
                      moe_b_gate_up[i], moe_w_down, moe_b_down[i], i,
                      final_g=final_norm if i == depth - 1 else None)
    return h
```

```python
import functools
import math

import jax
import jax.numpy as jnp
from jax import lax
from jax.experimental import pallas as pl
from jax.experimental.pallas import tpu as pltpu

F32 = jnp.float32
BF16 = jnp.bfloat16

D_MODEL = 1024
RMS_EPS = 1e-5
LOG2_E = 1.4426950408889634
N_MIXERS = 2

SSD_D_INNER = 2048
SSD_HEADDIM = 64
SSD_N_HEADS = 32
SSD_N_GROUPS = 4
SSD_HEADS_PER_GROUP = 8
SSD_D_STATE = 128
SSD_D_CONV = 4
SSD_CHUNK = 128
SSD_GN = 512
SSD_CONV_DIM = 3072
SSD_GROUP_WIDTH = SSD_D_INNER // SSD_N_GROUPS

DA_HEAD_DIM = 64
DA_N_HEADS = 8
DA_KV_UNROLL = 4

XA_N_HEADS = 4
XA_HEAD_DIM = 256

N_EXPERTS = 32
TOP_K = 4
D_FF = 1024
SWIGLU_LIMIT = 7.0
SWIGLU_ALPHA = 1.702

LANES = 128
SUBLANES = 8
VMEM_LIMIT_BYTES = 56 * 1024 * 1024

MOE_ROWS_PER_BLOCK = 512
ROUTER_TILE = 512
COMBINE_TILE = 256
DMA_ISSUE_UNROLL = 8
EXPERT_FF_CHUNKS = 1


def _cparams(*sem):
    return pltpu.CompilerParams(dimension_semantics=sem, vmem_limit_bytes=VMEM_LIMIT_BYTES)


def _rms(x, g):
    ms = jnp.mean(x * x, axis=-1, keepdims=True)
    return x * lax.rsqrt(ms + RMS_EPS) * g


def _dot(a, b):
    return jnp.dot(a, b, preferred_element_type=F32)


def _dot_nt(a, b):
    return lax.dot_general(a, b, (((1,), (1,)), ((), ())), preferred_element_type=F32)


def _split2(x):
    hi = x.astype(BF16)
    lo = (x - hi.astype(F32)).astype(BF16)
    return hi, lo


def _split3(x):
    hi = x.astype(BF16)
    r = x - hi.astype(F32)
    mid = r.astype(BF16)
    lo = (r - mid.astype(F32)).astype(BF16)
    return hi, mid, lo


def _norm_mm_kernel(x_ref, g_ref, w_ref, o_ref, xn_ref):
    @pl.when(pl.program_id(1) == 0)
    def _():
        xn_ref[...] = _rms(x_ref[...], g_ref[...]).astype(BF16)

    o_ref[...] = _dot(xn_ref[...], w_ref[...]).astype(o_ref.dtype)


def norm_matmul(x, g, w, out_dtype, tm, tn):
    m, k = x.shape
    n = w.shape[1]
    tm = min(tm, m)
    tn = min(tn, n)
    return pl.pallas_call(
        _norm_mm_kernel,
        grid=(m // tm, n // tn),
        in_specs=[
            pl.BlockSpec((tm, k), lambda i, j: (i, 0)),
            pl.BlockSpec((1, k), lambda i, j: (0, 0)),
            pl.BlockSpec((k, tn), lambda i, j: (0, j)),
        ],
        out_specs=pl.BlockSpec((tm, tn), lambda i, j: (i, j)),
        out_shape=jax.ShapeDtypeStruct((m, n), out_dtype),
        scratch_shapes=[pltpu.VMEM((tm, k), BF16)],
        compiler_params=_cparams("parallel", "arbitrary"),
        name="norm_matmul",
    )(x, g.reshape(1, k), w)


def _mm_res_kernel(x_ref, w_ref, r_ref, o_ref):
    o_ref[...] = r_ref[...] + _dot(x_ref[...], w_ref[...])


def matmul_residual(x, w, res, tm=512):
    m, k = x.shape
    n = w.shape[1]
    tm = min(tm, m)
    return pl.pallas_call(
        _mm_res_kernel,
        grid=(m // tm,),
        in_specs=[
            pl.BlockSpec((tm, k), lambda i: (i, 0)),
            pl.BlockSpec((k, n), lambda i: (0, 0)),
            pl.BlockSpec((tm, n), lambda i: (i, 0)),
        ],
        out_specs=pl.BlockSpec((tm, n), lambda i: (i, 0)),
        out_shape=jax.ShapeDtypeStruct((m, n), F32),
        compiler_params=_cparams("parallel"),
        name="matmul_residual",
    )(x, w, res)


def _norm_kernel(x_ref, g_ref, o_ref):
    o_ref[...] = _rms(x_ref[...], g_ref[...])


def final_norm_call(x, g, tm=1024):
    m, k = x.shape
    tm = min(tm, m)
    return pl.pallas_call(
        _norm_kernel,
        grid=(m // tm,),
        in_specs=[pl.BlockSpec((tm, k), lambda i: (i, 0)), pl.BlockSpec((1, k), lambda i: (0, 0))],
        out_specs=pl.BlockSpec((tm, k), lambda i: (i, 0)),
        out_shape=jax.ShapeDtypeStruct((m, k), F32),
        compiler_params=_cparams("parallel"),
        name="final_norm",
    )(x, g.reshape(1, k))


def _ssd_kernel(z0_ref, z1_ref, x0_ref, x1_ref, bc_ref, dtr_ref, h_ref,
                convw_ref, convb_ref, dtb_ref, alog_ref, dexp_ref, ng_ref, expand_ref, wout_ref,
                o_ref, state_ref, ext_ref):
    L = SSD_CHUNK
    GW = SSD_GROUP_WIDTH
    c = pl.program_id(1)

    @pl.when(c == 0)
    def _():
        state_ref[...] = jnp.zeros_like(state_ref)
        ext_ref[0:L, :] = jnp.zeros((L, SSD_CONV_DIM), BF16)

    srow = lax.broadcasted_iota(jnp.int32, (L, 2 * L), 0)
    scol = lax.broadcasted_iota(jnp.int32, (L, 2 * L), 1)
    shifts = [jnp.where(scol == srow + (L - (SSD_D_CONV - 1) + k), 1.0, 0.0).astype(BF16)
              for k in range(SSD_D_CONV - 1)]
    pieces = []
    for blk, ref in enumerate((x0_ref, x1_ref, bc_ref)):
        cols = slice(blk * 1024, (blk + 1) * 1024)
        cur = ref[...]
        ext_ref[L:2 * L, cols] = cur
        both = ext_ref[:, cols]
        acc = convb_ref[:, cols] + convw_ref[SSD_D_CONV - 1:SSD_D_CONV, cols] * cur.astype(F32)
        for k in range(SSD_D_CONV - 1):
            acc = acc + convw_ref[k:k + 1, cols] * _dot(shifts[k], both)
        pieces.append(acc * jax.nn.sigmoid(acc))
        ext_ref[0:L, cols] = cur
    xs = jnp.concatenate(pieces[:2], axis=-1)
    b_all = pieces[2][:, :SSD_GN]
    c_all = pieces[2][:, SSD_GN:]

    dtr = dtr_ref[...] + dtb_ref[...]
    dt = jnp.maximum(dtr, 0.0) + jnp.log1p(jnp.exp(-jnp.abs(dtr)))
    a = -jnp.exp(alog_ref[...])
    da = dt * a
    row = lax.broadcasted_iota(jnp.int32, (L, L), 0)
    col = lax.broadcasted_iota(jnp.int32, (L, L), 1)
    causal = col <= row
    tril = jnp.where(causal, 1.0, 0.0).astype(BF16)
    d_hi, d_mid, d_lo = _split3(da)
    a_cum = _dot(tril, d_hi) + _dot(tril, d_mid) + _dot(tril, d_lo)
    a_cum_t = a_cum.T

    expand = expand_ref[...]
    dt_e = _dot(dt.astype(BF16), expand)
    w_e = _dot((dt * jnp.exp(a_cum[L - 1:L, :] - a_cum)).astype(BF16), expand)
    e_hi, e_lo = _split2(jnp.exp(a_cum))
    exp_acum_e = _dot(e_hi, expand) + _dot(e_lo, expand)
    cd_e = exp_acum_e[L - 1:L, :]

    xd_b = (xs * dt_e).astype(BF16)
    xdw_b = (xs * w_e).astype(BF16)

    lane = lax.broadcasted_iota(jnp.int32, (L, LANES), 1)
    first_half = lane < SSD_HEADDIM

    y_parts = []
    for g in range(SSD_N_GROUPS):
        gs = slice(g * SSD_D_STATE, (g + 1) * SSD_D_STATE)
        gw = slice(g * GW, (g + 1) * GW)
        b_g = b_all[:, gs]
        c_g = c_all[:, gs].astype(BF16)
        cb = _dot_nt(c_g, b_g.astype(BF16))
        y_pairs = []
        for jp in range(SSD_HEADS_PER_GROUP // 2):
            res = []
            pair_col = g * GW + jp * LANES
            xd_pair = xd_b[:, pair_col:pair_col + LANES]
            for sub in range(2):
                hd = g * SSD_HEADS_PER_GROUP + jp * 2 + sub
                diff = a_cum[:, hd:hd + 1] - a_cum_t[hd:hd + 1, :]
                dec = jnp.exp(jnp.where(causal, diff, -jnp.inf))
                res.append(_dot((cb * dec).astype(BF16), xd_pair))
            y_pairs.append(jnp.where(first_half, res[0], res[1]))
        y_diag = jnp.concatenate(y_pairs, axis=-1)
        st = state_ref[g]
        y_off = _dot(c_g, st.astype(BF16)) * exp_acum_e[:, gw]
        state_ref[g] = st * cd_e[:, gw] + _dot(b_g.T.astype(BF16), xdw_b[:, gw])
        y_parts.append(y_diag + y_off)
    y = jnp.concatenate(y_parts, axis=-1) + dexp_ref[...] * xs

    z = jnp.concatenate([z0_ref[...], z1_ref[...]], axis=-1).astype(F32)
    u = y * (z * jax.nn.sigmoid(z))
    u_parts = []
    for g in range(SSD_N_GROUPS):
        gw = slice(g * GW, (g + 1) * GW)
        ug = u[:, gw]
        ms = jnp.mean(ug * ug, axis=-1, keepdims=True)
        u_parts.append(ug * lax.rsqrt(ms + RMS_EPS) * ng_ref[:, gw])
    un = jnp.concatenate(u_parts, axis=-1).astype(BF16)
    o_ref[...] = h_ref[...] + _dot(un, wout_ref[...])


def ssd_core(zxbc, dt_raw, h, conv_w, conv_b, dt_bias, a_log, d_skip, norm_g, w_out_b):
    bsz, seq, _ = h.shape
    L = SSD_CHUNK
    nc = seq // L
    pad_heads = LANES - SSD_N_HEADS
    dtb = jnp.pad(dt_bias.astype(F32), (0, pad_heads)).reshape(1, LANES)
    alog = jnp.pad(a_log.astype(F32), (0, pad_heads)).reshape(1, LANES)
    dexp = jnp.repeat(d_skip.astype(F32), SSD_HEADDIM).reshape(1, SSD_D_INNER)
    head_of_col = jnp.arange(SSD_D_INNER, dtype=jnp.int32) // SSD_HEADDIM
    expand = (jnp.arange(LANES, dtype=jnp.int32)[:, None] == head_of_col[None, :]).astype(BF16)

    def zx_spec(k):
        return pl.BlockSpec((None, L, 1024), lambda b, c, k=k: (b, c, k))

    def const_spec(shape):
        return pl.BlockSpec(shape, lambda b, c: (0,) * len(shape))

    return pl.pallas_call(
        _ssd_kernel,
        grid=(bsz, nc),
        in_specs=[
            zx_spec(0), zx_spec(1), zx_spec(2), zx_spec(3), zx_spec(4),
            pl.BlockSpec((None, L, LANES), lambda b, c: (b, c, 0)),
            pl.BlockSpec((None, L, D_MODEL), lambda b, c: (b, c, 0)),
            const_spec((SSD_D_CONV, SSD_CONV_DIM)),
            const_spec((1, SSD_CONV_DIM)),
            const_spec((1, LANES)),
            const_spec((1, LANES)),
            const_spec((1, SSD_D_INNER)),
            const_spec((1, SSD_D_INNER)),
            const_spec((LANES, SSD_D_INNER)),
            const_spec((SSD_D_INNER, D_MODEL)),
        ],
        out_specs=pl.BlockSpec((None, L, D_MODEL), lambda b, c: (b, c, 0)),
        out_shape=jax.ShapeDtypeStruct((bsz, seq, D_MODEL), F32),
        scratch_shapes=[
            pltpu.VMEM((SSD_N_GROUPS, SSD_D_STATE, SSD_GROUP_WIDTH), F32),
            pltpu.VMEM((2 * L, SSD_CONV_DIM), BF16),
        ],
        compiler_params=_cparams("parallel", "arbitrary"),
        name="ssd_core",
    )(zxbc, zxbc, zxbc, zxbc, zxbc, dt_raw, h,
      conv_w.astype(F32), conv_b.astype(F32).reshape(1, SSD_CONV_DIM), dtb, alog, dexp,
      norm_g.astype(F32).reshape(1, SSD_D_INNER), expand, w_out_b)


def ssd_layer(h, norm_g_in, w_in, conv_w, conv_b, dt_bias, a_log, d_skip, norm_g, w_out):
    bsz, seq, d = h.shape
    h2 = h.reshape(bsz * seq, d)
    n_main = SSD_D_INNER + SSD_CONV_DIM
    w_main = w_in[:, :n_main].astype(BF16)
    w_dt = jnp.pad(w_in[:, n_main:], ((0, 0), (0, LANES - SSD_N_HEADS))).astype(BF16)
    zxbc = norm_matmul(h2, norm_g_in, w_main, BF16, tm=1024, tn=1024)
    dt_raw = norm_matmul(h2, norm_g_in, w_dt, F32, tm=1024, tn=LANES)
    return ssd_core(zxbc.reshape(bsz, seq, n_main), dt_raw.reshape(bsz, seq, LANES), h,
                    conv_w, conv_b, dt_bias, a_log, d_skip, norm_g, w_out.astype(BF16))


def _da_kernel(lq1_ref, lk1_ref, lq2_ref, lk2_ref, sub_ref, q_ref, k_ref, v_ref, o_ref,
               acc1_ref, acc2_ref, m1_ref, m2_ref, *, tq, lambda_init):
    i = pl.program_id(2)
    q = q_ref[...]
    lane = lax.broadcasted_iota(jnp.int32, (tq, LANES), 1)
    qs = (q.astype(F32) * (DA_HEAD_DIM ** -0.5 * LOG2_E)).astype(BF16)
    zero = jnp.zeros_like(qs)
    q_maps = (jnp.where(lane < DA_HEAD_DIM, qs, zero), jnp.where(lane >= DA_HEAD_DIM, qs, zero))
    states = ((m1_ref, acc1_ref), (m2_ref, acc2_ref))

    for m_ref, acc_ref in states:
        m_ref[...] = jnp.full((tq, LANES), -jnp.inf, F32)
        acc_ref[...] = jnp.zeros((tq, 2 * LANES), F32)

    ones = jnp.ones((tq, LANES), BF16)

    def step(j, masked):
        start = pl.multiple_of(j * tq, tq)
        kt = k_ref[pl.ds(start, tq), :]
        v_aug = jnp.concatenate([v_ref[pl.ds(start, tq), :], ones], axis=1)
        for qm, (m_ref, acc_ref) in zip(q_maps, states):
            s = _dot_nt(qm, kt)
            if masked:
                r = lax.broadcasted_iota(jnp.int32, (tq, tq), 0)
                cidx = lax.broadcasted_iota(jnp.int32, (tq, tq), 1)
                s = jnp.where(cidx <= r, s, -jnp.inf)
            m_old = m_ref[...]
            m_new = jnp.maximum(m_old, jnp.max(s, axis=-1, keepdims=True))
            alpha = jnp.exp2(m_old - m_new)
            p = jnp.exp2(s - jnp.concatenate([m_new] * (tq // LANES), axis=1))
            acc_ref[...] = (jnp.concatenate([alpha, alpha], axis=1) * acc_ref[...]
                            + _dot(p.astype(BF16), v_aug))
            m_ref[...] = m_new

    def body(jj, carry):
        for u in range(DA_KV_UNROLL):
            step(DA_KV_UNROLL * jj + u, False)
        return carry

    lax.fori_loop(0, i // DA_KV_UNROLL, body, 0)

    for rem in range(DA_KV_UNROLL):
        @pl.when(i % DA_KV_UNROLL == rem)
        def _(rem=rem):
            for u in range(rem):
                step(i - rem + u, False)
            step(i, True)

    lam = (jnp.exp(jnp.sum(lq1_ref[...] * lk1_ref[...], axis=-1, keepdims=True))
           - jnp.exp(jnp.sum(lq2_ref[...] * lk2_ref[...], axis=-1, keepdims=True)) + lambda_init)
    o1 = acc1_ref[:, :LANES] / acc1_ref[:, LANES:]
    o2 = acc2_ref[:, :LANES] / acc2_ref[:, LANES:]
    o = _rms(o1 - lam * o2, sub_ref[...]) * (1.0 - lambda_init)
    o_ref[...] = o.astype(o_ref.dtype)


def diff_attention_core(qkv, lq1, lk1, lq2, lk2, subln_g, layer_idx, tq=512):
    bsz, seq, _ = qkv.shape
    tq = min(tq, seq)
    lambda_init = 0.8 - 0.6 * math.exp(-0.3 * layer_idx)
    nh = DA_N_HEADS

    def vec_spec(n):
        return pl.BlockSpec((1, n), lambda b, h, i: (0, 0))

    return pl.pallas_call(
        functools.partial(_da_kernel, tq=tq, lambda_init=lambda_init),
        grid=(bsz, nh, seq // tq),
        in_specs=[
            vec_spec(DA_HEAD_DIM), vec_spec(DA_HEAD_DIM), vec_spec(DA_HEAD_DIM), vec_spec(DA_HEAD_DIM),
            vec_spec(LANES),
            pl.BlockSpec((None, tq, LANES), lambda b, h, i: (b, i, h)),
            pl.BlockSpec((None, seq, LANES), lambda b, h, i: (b, 0, nh + h)),
            pl.BlockSpec((None, seq, LANES), lambda b, h, i: (b, 0, 2 * nh + h)),
        ],
        out_specs=pl.BlockSpec((None, tq, LANES), lambda b, h, i: (b, i, h)),
        out_shape=jax.ShapeDtypeStruct((bsz, seq, D_MODEL), BF16),
        scratch_shapes=[
            pltpu.VMEM((tq, 2 * LANES), F32), pltpu.VMEM((tq, 2 * LANES), F32),
            pltpu.VMEM((tq, LANES), F32), pltpu.VMEM((tq, LANES), F32),
        ],
        compiler_params=_cparams("parallel", "parallel", "arbitrary"),
        name="diff_attention",
    )(lq1.astype(F32).reshape(1, -1), lk1.astype(F32).reshape(1, -1),
      lq2.astype(F32).reshape(1, -1), lk2.astype(F32).reshape(1, -1),
      subln_g.astype(F32).reshape(1, -1), qkv, qkv, qkv)


def da_layer(h, norm_g_in, w_qkv, lq1, lk1, lq2, lk2, subln_g, w_o, layer_idx):
    bsz, seq, d = h.shape
    h2 = h.reshape(bsz * seq, d)
    qkv = norm_matmul(h2, norm_g_in, w_qkv.astype(BF16), BF16, tm=1024, tn=1024)
    o = diff_attention_core(qkv.reshape(bsz, seq, 3 * d), lq1, lk1, lq2, lk2, subln_g, layer_idx)
    return matmul_residual(o.reshape(bsz * seq, d), w_o.astype(BF16), h2).reshape(bsz, seq, d)


def _xattn_kernel(h_ref, g_ref, wq_ref, kv_ref, wo_ref, o_ref):
    h = h_ref[...]
    hn = _rms(h, g_ref[...]).astype(BF16)
    scale = XA_HEAD_DIM ** -0.5
    q = (_dot(hn, wq_ref[...]) * scale).astype(BF16)
    outs = []
    for hd in range(XA_N_HEADS):
        cs = slice(hd * XA_HEAD_DIM, (hd + 1) * XA_HEAD_DIM)
        vs = slice(D_MODEL + hd * XA_HEAD_DIM, D_MODEL + (hd + 1) * XA_HEAD_DIM)
        s = _dot_nt(q[:, cs], kv_ref[:, cs])
        m = jnp.max(s, axis=-1, keepdims=True)
        p = jnp.exp(s - m)
        l = jnp.sum(p, axis=-1, keepdims=True)
        outs.append((_dot(p.astype(BF16), kv_ref[:, vs]) / l).astype(BF16))
    o = jnp.concatenate(outs, axis=-1)
    o_ref[...] = h + _dot(o, wo_ref[...])


def xattn_layer(h, mem, norm_g, mem_norm_g, w_q, w_kv, w_o, tq=512):
    bsz, seq, d = h.shape
    mlen = mem.shape[1]
    tq = min(tq, seq)
    kv = norm_matmul(mem.reshape(bsz * mlen, d), mem_norm_g, w_kv.astype(BF16), BF16, tm=512, tn=1024)
    kv = kv.reshape(bsz, mlen, 2 * d)
    return pl.pallas_call(
        _xattn_kernel,
        grid=(bsz, seq // tq),
        in_specs=[
            pl.BlockSpec((None, tq, d), lambda b, i: (b, i, 0)),
            pl.BlockSpec((1, d), lambda b, i: (0, 0)),
            pl.BlockSpec((d, d), lambda b, i: (0, 0)),
            pl.BlockSpec((None, mlen, 2 * d), lambda b, i: (b, 0, 0)),
            pl.BlockSpec((d, d), lambda b, i: (0, 0)),
        ],
        out_specs=pl.BlockSpec((None, tq, d), lambda b, i: (b, i, 0)),
        out_shape=jax.ShapeDtypeStruct((bsz, seq, d), F32),
        compiler_params=_cparams("parallel", "parallel"),
        name="mem_xattn",
    )(h, norm_g.astype(F32).reshape(1, d), w_q.astype(BF16), kv, w_o.astype(BF16))


SLAB = D_MODEL // LANES


def _router_kernel(h_ref, g_ref, wh_ref, wl_ref, br_ref, su_ref,
                   slab_ref, idx_ref, gate_ref, rank_ref, cnt_ref, run_ref, *, tm):
    i = pl.program_id(0)

    @pl.when(i == 0)
    def _():
        run_ref[...] = jnp.zeros_like(run_ref)

    hn = _rms(h_ref[...], g_ref[...])
    for s in range(SLAB):
        slab_ref[pl.ds(s, tm, stride=SLAB), :] = hn[:, s * LANES:(s + 1) * LANES]

    x_hi, x_lo = _split2(hn)
    wh = wh_ref[...]
    logits = _dot_nt(wh, x_hi) + _dot_nt(wh, x_lo) + _dot_nt(wl_ref[...], x_hi) + br_ref[...]

    rows = lax.broadcasted_iota(jnp.int32, (N_EXPERTS, tm), 0).astype(F32)
    tops, idxs, onehots = [], [], []
    cur = logits
    for _ in range(TOP_K):
        m = jnp.max(cur, axis=0, keepdims=True)
        idx = jnp.min(jnp.where(cur == m, rows, float(N_EXPERTS)), axis=0, keepdims=True)
        oh = rows == idx
        cur = jnp.where(oh, -jnp.inf, cur)
        tops.append(m)
        idxs.append(idx)
        onehots.append(oh)
    exps = [jnp.exp(t - tops[0]) for t in tops]
    denom = exps[0] + exps[1] + exps[2] + exps[3]
    gate_ref[...] = jnp.concatenate([e / denom for e in exps], axis=0)
    idx_ref[...] = jnp.concatenate(idxs, axis=0).astype(jnp.int32)

    oh_sum = jnp.zeros((N_EXPERTS, tm), F32)
    for oh in onehots:
        oh_sum = oh_sum + jnp.where(oh, 1.0, 0.0)
    run = run_ref[...]
    prefix = _dot(oh_sum.astype(BF16), su_ref[...]) + run[:, 0:1]
    ranks = [jnp.sum(jnp.where(oh, prefix, 0.0), axis=0, keepdims=True) for oh in onehots]
    rank_ref[...] = jnp.concatenate(ranks, axis=0).astype(jnp.int32)
    run_new = run + jnp.sum(oh_sum, axis=1, keepdims=True)
    run_ref[...] = run_new
    cnt_ref[...] = run_new


def moe_router(h2, norm_g, w_router, b_router, tm=ROUTER_TILE):
    t, d = h2.shape
    tm = min(tm, t)
    wt = w_router.astype(F32).T
    wh = wt.astype(BF16)
    wl = (wt - wh.astype(F32)).astype(BF16)
    su = (jnp.arange(tm)[:, None] < jnp.arange(tm)[None, :]).astype(BF16)
    return pl.pallas_call(
        functools.partial(_router_kernel, tm=tm),
        grid=(t // tm,),
        in_specs=[
            pl.BlockSpec((tm, d), lambda i: (i, 0)),
            pl.BlockSpec((1, d), lambda i: (0, 0)),
            pl.BlockSpec((N_EXPERTS, d), lambda i: (0, 0)),
            pl.BlockSpec((N_EXPERTS, d), lambda i: (0, 0)),
            pl.BlockSpec((N_EXPERTS, 1), lambda i: (0, 0)),
            pl.BlockSpec((tm, tm), lambda i: (0, 0)),
        ],
        out_specs=[
            pl.BlockSpec((tm * SLAB, LANES), lambda i: (i, 0)),
            pl.BlockSpec((TOP_K, tm), lambda i: (0, i)),
            pl.BlockSpec((TOP_K, tm), lambda i: (0, i)),
            pl.BlockSpec((TOP_K, tm), lambda i: (0, i)),
            pl.BlockSpec((N_EXPERTS, LANES), lambda i: (0, 0)),
        ],
        out_shape=[
            jax.ShapeDtypeStruct((t * SLAB, LANES), F32),
            jax.ShapeDtypeStruct((TOP_K, t), jnp.int32),
            jax.ShapeDtypeStruct((TOP_K, t), F32),
            jax.ShapeDtypeStruct((TOP_K, t), jnp.int32),
            jax.ShapeDtypeStruct((N_EXPERTS, LANES), F32),
        ],
        scratch_shapes=[pltpu.VMEM((N_EXPERTS, LANES), F32)],
        compiler_params=_cparams("arbitrary"),
        name="moe_router",
    )(h2, norm_g.astype(F32).reshape(1, d), wh, wl, b_router.astype(F32).reshape(N_EXPERTS, 1), su)


def _slab_rows(r):
    return pl.ds(pl.multiple_of(r * SLAB, SLAB), SLAB)


IDS_LEN = 2048
TOP_K_SHIFT = 2


def _expert_kernel(be_ref, off_ref, nv_ref, ids_hbm, slab_hbm, wgu_ref, bgu_ref, wd_ref, bd_ref, ys_ref,
                   wgu_b_ref, wd_b_ref, xa_ref, xb_ref, ids_ref, sem_x, sem_i, *, bm, nb, n_assign):
    b = pl.program_id(0)
    nxt = jnp.minimum(b + 1, nb - 1)
    nxt2 = jnp.minimum(b + 2, nb - 1)
    in_use = b < be_ref[nb]
    slot_n = (b + 1) % 2

    def ids_copy(blk, slot):
        first = pl.multiple_of((off_ref[blk] >> 10) << 10, 1024)
        return pltpu.make_async_copy(ids_hbm.at[pl.ds(first, IDS_LEN)],
                                     ids_ref.at[pl.ds(pl.multiple_of(slot * IDS_LEN, IDS_LEN), IDS_LEN)],
                                     sem_i.at[slot])

    def block_info(blk, slot):
        return slot * IDS_LEN + (off_ref[blk] & 1023), nv_ref[blk], be_ref[blk] * n_assign

    def row_gather(info, r, dst_rows, xbuf, sem):
        base, nv, key0 = info
        tok = jnp.where(r < nv, (ids_ref[base + r] - key0) >> TOP_K_SHIFT, 0)
        return pltpu.make_async_copy(slab_hbm.at[_slab_rows(tok), :], xbuf.at[dst_rows, :], sem)

    def issue_rolled(blk, slot, xbuf, sem):
        info = block_info(blk, slot)

        def issue(g, carry):
            for u in range(DMA_ISSUE_UNROLL):
                r = g * DMA_ISSUE_UNROLL + u
                row_gather(info, r, _slab_rows(r), xbuf, sem).start(priority=u % 2)
            return carry
        lax.fori_loop(0, bm // DMA_ISSUE_UNROLL, issue, 0)

    def wait_rows(xbuf, sem):
        pltpu.make_async_copy(slab_hbm.at[pl.ds(0, bm * SLAB), :], xbuf, sem).wait()

    @pl.when(b == 0)
    def _():
        first = ids_copy(0, 0)
        first.start()
        first.wait()
        issue_rolled(0, 0, xa_ref, sem_x.at[0])
        ids_copy(nxt, 1).start()

    ids_copy(nxt, slot_n).wait()
    ids_copy(nxt2, b % 2).start()

    @pl.when(in_use & ((b == 0) | (be_ref[b] != be_ref[jnp.maximum(b - 1, 0)])))
    def _():
        wgu_b_ref[...] = wgu_ref[...].astype(BF16)
        wd_b_ref[...] = wd_ref[...].astype(BF16)

    def run(x_cur, sem_cur, x_nxt, sem_nxt):
        wait_rows(x_cur, sem_cur)

        @pl.when(in_use)
        def _():
            x = jnp.concatenate([x_cur[pl.ds(s, bm, stride=SLAB), :] for s in range(SLAB)], axis=-1)
            xb16 = x.astype(BF16)
            y = bd_ref[...]
            per_chunk = bm // EXPERT_FF_CHUNKS
            cw = D_FF // EXPERT_FF_CHUNKS
            info = block_info(nxt, slot_n)
            for c in range(EXPERT_FF_CHUNKS):
                for r in range(c * per_chunk, (c + 1) * per_chunk):
                    row_gather(info, r, pl.ds(r * SLAB, SLAB), x_nxt, sem_nxt).start(priority=r % 2)
                gc = slice(c * cw, (c + 1) * cw)
                uc = slice(D_FF + c * cw, D_FF + (c + 1) * cw)
                gate = jnp.minimum(_dot(xb16, wgu_b_ref[:, gc]) + bgu_ref[:, gc], SWIGLU_LIMIT)
                up = jnp.clip(_dot(xb16, wgu_b_ref[:, uc]) + bgu_ref[:, uc], -SWIGLU_LIMIT, SWIGLU_LIMIT)
                act = (up + 1.0) * (gate * jax.nn.sigmoid(gate * SWIGLU_ALPHA))
                y = y + _dot(act.astype(BF16), wd_b_ref[gc, :])
            for s in range(SLAB):
                ys_ref[pl.ds(s, bm, stride=SLAB), :] = y[:, s * LANES:(s + 1) * LANES]

        @pl.when(jnp.logical_not(in_use))
        def _():
            issue_rolled(nxt, slot_n, x_nxt, sem_nxt)
            ys_ref[...] = jnp.zeros_like(ys_ref)

        @pl.when(b == nb - 1)
        def _():
            wait_rows(x_nxt, sem_nxt)
            ids_copy(nxt2, b % 2).wait()

    @pl.when(b % 2 == 0)
    def _():
        run(xa_ref, sem_x.at[0], xb_ref, sem_x.at[1])

    @pl.when(b % 2 == 1)
    def _():
        run(xb_ref, sem_x.at[1], xa_ref, sem_x.at[0])


def moe_experts(blk_meta, blk_off, blk_nv, ids, slab, w_gate_up, b_gate_up, w_down, b_down, layer, bm, n_assign):
    nb = blk_meta.shape[0] - 1
    d = D_MODEL
    assert bm + 1023 <= IDS_LEN and bm % EXPERT_FF_CHUNKS == 0 and TOP_K == 1 << TOP_K_SHIFT
    grid_spec = pltpu.PrefetchScalarGridSpec(
        num_scalar_prefetch=3,
        grid=(nb,),
        in_specs=[
            pl.BlockSpec(memory_space=pl.ANY),
            pl.BlockSpec(memory_space=pl.ANY),
            pl.BlockSpec((None, None, d, 2 * D_FF), lambda i, be, off, nv: (layer, be[i], 0, 0)),
            pl.BlockSpec((None, 1, 2 * D_FF), lambda i, be, off, nv: (be[i], 0, 0)),
            pl.BlockSpec((None, None, D_FF, d), lambda i, be, off, nv: (layer, be[i], 0, 0)),
            pl.BlockSpec((None, 1, d), lambda i, be, off, nv: (be[i], 0, 0)),
        ],
        out_specs=pl.BlockSpec((bm * SLAB, LANES), lambda i, be, off, nv: (i, 0)),
        scratch_shapes=[
            pltpu.VMEM((d, 2 * D_FF), BF16), pltpu.VMEM((D_FF, d), BF16),
            pltpu.VMEM((bm * SLAB, LANES), F32), pltpu.VMEM((bm * SLAB, LANES), F32),
            pltpu.SMEM((2 * IDS_LEN,), jnp.int32),
            pltpu.SemaphoreType.DMA((2,)), pltpu.SemaphoreType.DMA((2,)),
        ],
    )
    return pl.pallas_call(
        functools.partial(_expert_kernel, bm=bm, nb=nb, n_assign=n_assign),
        grid_spec=grid_spec,
        out_shape=jax.ShapeDtypeStruct((nb * bm * SLAB, LANES), F32),
        compiler_params=_cparams("arbitrary"),
        name="moe_experts",
    )(blk_meta, blk_off, blk_nv, ids, slab, w_gate_up, b_gate_up.astype(F32).reshape(N_EXPERTS, 1, 2 * D_FF),
      w_down, b_down.astype(F32).reshape(N_EXPERTS, 1, d))


def _combine_kernel(dest_ref, dest_next_ref, gate_ref, h_ref, fg_ref, ys_ref, o_ref, buf_a, buf_b, sem,
                    *, tm, final_norm):
    i = pl.program_id(0)
    n = pl.num_programs(0)
    n_rows = TOP_K * tm

    def row_copy(d_ref, j, buf, s):
        return pltpu.make_async_copy(ys_ref.at[_slab_rows(d_ref[0, 0, j]), :],
                                     buf.at[pl.ds(j * SLAB, SLAB), :], s)

    def wait_tile(buf, s):
        pltpu.make_async_copy(ys_ref.at[pl.ds(0, n_rows * SLAB), :], buf, s).wait()

    @pl.when(i == 0)
    def _():
        def issue(g, carry):
            for u in range(DMA_ISSUE_UNROLL):
                j = g * DMA_ISSUE_UNROLL + u
                pltpu.make_async_copy(ys_ref.at[_slab_rows(dest_ref[0, 0, j]), :],
                                      buf_a.at[_slab_rows(j), :], sem.at[0]).start(priority=u % 2)
            return carry
        lax.fori_loop(0, n_rows // DMA_ISSUE_UNROLL, issue, 0)

    def run(cur, cur_sem, nxt, nxt_sem):
        wait_tile(cur, cur_sem)
        gates = gate_ref[...]
        per_block = n_rows // SLAB
        for s in range(SLAB):
            for j in range(s * per_block, (s + 1) * per_block):
                row_copy(dest_next_ref, j, nxt, nxt_sem).start(priority=j % 2)
            cols = slice(s * LANES, (s + 1) * LANES)
            acc = h_ref[:, cols]
            for k in range(TOP_K):
                acc = acc + gates[:, k:k + 1] * cur[pl.ds(k * tm * SLAB + s, tm, stride=SLAB), :]
            o_ref[:, cols] = acc
        if final_norm:
            o_ref[...] = _rms(o_ref[...], fg_ref[...])

        @pl.when(i == n - 1)
        def _():
            wait_tile(nxt, nxt_sem)

    @pl.when(i % 2 == 0)
    def _():
        run(buf_a, sem.at[0], buf_b, sem.at[1])

    @pl.when(i % 2 == 1)
    def _():
        run(buf_b, sem.at[1], buf_a, sem.at[0])


def moe_combine(dest_tiles, gates_col, h2, ys, tm, final_g=None):
    t, d = h2.shape
    nt = t // tm
    fg = jnp.ones((1, d), F32) if final_g is None else final_g.astype(F32).reshape(1, d)
    return pl.pallas_call(
        functools.partial(_combine_kernel, tm=tm, final_norm=final_g is not None),
        grid=(nt,),
        in_specs=[
            pl.BlockSpec((1, 1, TOP_K * tm), lambda i: (i, 0, 0), memory_space=pltpu.SMEM),
            pl.BlockSpec((1, 1, TOP_K * tm), lambda i: (jnp.minimum(i + 1, nt - 1), 0, 0),
                         memory_space=pltpu.SMEM),
            pl.BlockSpec((tm, TOP_K), lambda i: (i, 0)),
            pl.BlockSpec((tm, d), lambda i: (i, 0)),
            pl.BlockSpec((1, d), lambda i: (0, 0)),
            pl.BlockSpec(memory_space=pl.ANY),
        ],
        out_specs=pl.BlockSpec((tm, d), lambda i: (i, 0)),
        out_shape=jax.ShapeDtypeStruct((t, d), F32),
        scratch_shapes=[pltpu.VMEM((TOP_K * tm * SLAB, LANES), F32), pltpu.VMEM((TOP_K * tm * SLAB, LANES), F32),
                        pltpu.SemaphoreType.DMA((2,))],
        compiler_params=_cparams("arbitrary"),
        name="moe_combine",
    )(dest_tiles, dest_tiles, gates_col, h2, fg, ys)


def _tile_major(a, tm):
    t = a.shape[1]
    return a.reshape(TOP_K, t // tm, tm).transpose(1, 0, 2).reshape(t // tm, 1, TOP_K * tm)


def moe_layer(h, norm_g, w_router, b_router, w_gate_up, b_gate_up, w_down, b_down, layer, final_g=None):
    bsz, seq, d = h.shape
    t = bsz * seq
    h2 = h.reshape(t, d)
    bm = MOE_ROWS_PER_BLOCK
    slab, idx, gates, rank, cnt = moe_router(h2, norm_g, w_router, b_router)

    counts = cnt[:, 0].astype(jnp.int32)
    padded = (counts + bm - 1) // bm * bm
    pend = jnp.cumsum(padded)
    pstart = pend - padded
    experts = jnp.arange(N_EXPERTS, dtype=jnp.int32)
    dest = rank + jnp.sum(jnp.where(idx[..., None] == experts, pstart, 0), axis=-1)
    n = t * TOP_K
    p_rows = -(-(n + N_EXPERTS * bm) // bm) * bm
    nb = p_rows // bm
    blk_start = jnp.arange(nb, dtype=jnp.int32) * bm
    blk_e = jnp.minimum(jnp.sum((pend[None, :] <= blk_start[:, None]).astype(jnp.int32), axis=1),
                        N_EXPERTS - 1)
    n_used = pend[N_EXPERTS - 1:] // bm
    blk_meta = jnp.concatenate([blk_e, n_used]).astype(jnp.int32)

    flat = jnp.arange(t, dtype=jnp.int32)[None, :] * TOP_K + jnp.arange(TOP_K, dtype=jnp.int32)[:, None]
    keys = jnp.sort((idx * n + flat).reshape(-1))
    ids = jnp.pad(keys, (0, IDS_LEN))
    start = jnp.cumsum(counts) - counts
    rank0 = blk_start - pstart[blk_e]
    blk_nv = jnp.clip(counts[blk_e] - rank0, 0, bm).astype(jnp.int32)
    blk_off = jnp.where(blk_nv > 0, start[blk_e] + rank0, 0).astype(jnp.int32)

    ys = moe_experts(blk_meta, blk_off, blk_nv, ids, slab, w_gate_up.astype(F32), b_gate_up,
                     w_down.astype(F32), b_down, layer, bm, n)
    tc = min(COMBINE_TILE, t)
    out = moe_combine(_tile_major(dest, tc), gates.T, h2, ys, tc, final_g)
    return out.reshape(bsz, seq, d)


def kernel(x, mem, mixer_norm, xattn_norm, mem_norm, ffn_norm, ssd_w_in, ssd_conv_w, ssd_conv_b, ssd_dt_bias, ssd_a_log, ssd_d, ssd_norm, ssd_w_out, da_w_qkv, da_lambda_q1, da_lambda_k1, da_lambda_q2, da_lambda_k2, da_subln, da_w_o, xa_w_q, xa_w_kv, xa_w_o, moe_w_router, moe_b_router, moe_w_gate_up, moe_b_gate_up, moe_w_down, moe_b_down, final_norm):
    depth = mixer_norm.shape[0]
    bsz, seq, d = x.shape
    h = x
    for i in range(depth):
        j = i // N_MIXERS
        if i % N_MIXERS == 0:
            h = ssd_layer(h, mixer_norm[i], ssd_w_in[j], ssd_conv_w[j], ssd_conv_b[j], ssd_dt_bias[j],
                          ssd_a_log[j], ssd_d[j], ssd_norm[j], ssd_w_out[j])
        else:
            h = da_layer(h, mixer_norm[i], da_w_qkv[j], da_lambda_q1[j], da_lambda_k1[j], da_lambda_q2[j],
                         da_lambda_k2[j], da_subln[j], da_w_o[j], i)
        h = xattn_layer(h, mem, xattn_norm[i], mem_norm[i], xa_w_q[i], xa_w_kv[i], xa_w_o[i])
        h = moe_layer(h, ffn_norm[i], moe_w_router[i], moe_b_router[i], moe_w_gate_up,
                      moe_b_gate_up[i], moe_w_down, moe_b_down[i], i,
                      final_g=final_norm if i == depth - 1 else None)
    return h
```

```python
import functools
import math

import jax
import jax.numpy as jnp
from jax import lax
from jax.experimental import pallas as pl
from jax.experimental.pallas import tpu as pltpu

F32 = jnp.float32
BF16 = jnp.bfloat16

D_MODEL = 1024
RMS_EPS = 1e-5
LOG2_E = 1.4426950408889634
N_MIXERS = 2

SSD_D_INNER = 2048
SSD_HEADDIM = 64
SSD_N_HEADS = 32
SSD_N_GROUPS = 4
SSD_HEADS_PER_GROUP = 8
SSD_D_STATE = 128
SSD_D_CONV = 4
SSD_CHUNK = 128
SSD_GN = 512
SSD_CONV_DIM = 3072
SSD_GROUP_WIDTH = SSD_D_INNER // SSD_N_GROUPS

DA_HEAD_DIM = 64
DA_N_HEADS = 8
DA_KV_UNROLL = 8

XA_N_HEADS = 4
XA_HEAD_DIM = 256

N_EXPERTS = 32
TOP_K = 4
D_FF = 1024
SWIGLU_LIMIT = 7.0
SWIGLU_ALPHA = 1.702

LANES = 128
SUBLANES = 8
VMEM_LIMIT_BYTES = 56 * 1024 * 1024

MOE_ROWS_PER_BLOCK = 512
ROUTER_TILE = 512
DISPATCH_TILE = 512
COMBINE_TILE = 256
DMA_ISSUE_UNROLL = 8
PAD_PIECE = 256
PAD_BITS = (256, 128, 64, 32, 16, 8, 4, 2, 1)


def _cparams(*sem):
    return pltpu.CompilerParams(dimension_semantics=sem, vmem_limit_bytes=VMEM_LIMIT_BYTES)


def _rms(x, g):
    ms = jnp.mean(x * x, axis=-1, keepdims=True)
    return x * lax.rsqrt(ms + RMS_EPS) * g


def _dot(a, b):
    return jnp.dot(a, b, preferred_element_type=F32)


def _dot_nt(a, b):
    return lax.dot_general(a, b, (((1,), (1,)), ((), ())), preferred_element_type=F32)


def _split2(x):
    hi = x.astype(BF16)
    lo = (x - hi.astype(F32)).astype(BF16)
    return hi, lo


def _split3(x):
    hi = x.astype(BF16)
    r = x - hi.astype(F32)
    mid = r.astype(BF16)
    lo = (r - mid.astype(F32)).astype(BF16)
    return hi, mid, lo


def _norm_mm_kernel(x_ref, g_ref, w_ref, o_ref, xn_ref):
    @pl.when(pl.program_id(1) == 0)
    def _():
        xn_ref[...] = _rms(x_ref[...], g_ref[...]).astype(BF16)

    o_ref[...] = _dot(xn_ref[...], w_ref[...]).astype(o_ref.dtype)


def norm_matmul(x, g, w, out_dtype, tm, tn):
    m, k = x.shape
    n = w.shape[1]
    tm = min(tm, m)
    tn = min(tn, n)
    return pl.pallas_call(
        _norm_mm_kernel,
        grid=(m // tm, n // tn),
        in_specs=[
            pl.BlockSpec((tm, k), lambda i, j: (i, 0)),
            pl.BlockSpec((1, k), lambda i, j: (0, 0)),
            pl.BlockSpec((k, tn), lambda i, j: (0, j)),
        ],
        out_specs=pl.BlockSpec((tm, tn), lambda i, j: (i, j)),
        out_shape=jax.ShapeDtypeStruct((m, n), out_dtype),
        scratch_shapes=[pltpu.VMEM((tm, k), BF16)],
        compiler_params=_cparams("parallel", "arbitrary"),
        name="norm_matmul",
    )(x, g.reshape(1, k), w)


def _mm_res_kernel(x_ref, w_ref, r_ref, o_ref):
    o_ref[...] = r_ref[...] + _dot(x_ref[...], w_ref[...])


def matmul_residual(x, w, res, tm=512):
    m, k = x.shape
    n = w.shape[1]
    tm = min(tm, m)
    return pl.pallas_call(
        _mm_res_kernel,
        grid=(m // tm,),
        in_specs=[
            pl.BlockSpec((tm, k), lambda i: (i, 0)),
            pl.BlockSpec((k, n), lambda i: (0, 0)),
            pl.BlockSpec((tm, n), lambda i: (i, 0)),
        ],
        out_specs=pl.BlockSpec((tm, n), lambda i: (i, 0)),
        out_shape=jax.ShapeDtypeStruct((m, n), F32),
        compiler_params=_cparams("parallel"),
        name="matmul_residual",
    )(x, w, res)


def _norm_kernel(x_ref, g_ref, o_ref):
    o_ref[...] = _rms(x_ref[...], g_ref[...])


def final_norm_call(x, g, tm=1024):
    m, k = x.shape
    tm = min(tm, m)
    return pl.pallas_call(
        _norm_kernel,
        grid=(m // tm,),
        in_specs=[pl.BlockSpec((tm, k), lambda i: (i, 0)), pl.BlockSpec((1, k), lambda i: (0, 0))],
        out_specs=pl.BlockSpec((tm, k), lambda i: (i, 0)),
        out_shape=jax.ShapeDtypeStruct((m, k), F32),
        compiler_params=_cparams("parallel"),
        name="final_norm",
    )(x, g.reshape(1, k))


def _ssd_kernel(z0_ref, z1_ref, x0_ref, x1_ref, bc_ref, dtr_ref, h_ref,
                convw_ref, convb_ref, dtb_ref, alog_ref, dexp_ref, ng_ref, expand_ref, wout_ref,
                o_ref, state_ref, ext_ref):
    L = SSD_CHUNK
    GW = SSD_GROUP_WIDTH
    c = pl.program_id(1)

    @pl.when(c == 0)
    def _():
        state_ref[...] = jnp.zeros_like(state_ref)
        ext_ref[0:L, :] = jnp.zeros((L, SSD_CONV_DIM), BF16)

    srow = lax.broadcasted_iota(jnp.int32, (L, 2 * L), 0)
    scol = lax.broadcasted_iota(jnp.int32, (L, 2 * L), 1)
    shifts = [jnp.where(scol == srow + (L - (SSD_D_CONV - 1) + k), 1.0, 0.0).astype(BF16)
              for k in range(SSD_D_CONV - 1)]
    pieces = []
    for blk, ref in enumerate((x0_ref, x1_ref, bc_ref)):
        cols = slice(blk * 1024, (blk + 1) * 1024)
        cur = ref[...]
        ext_ref[L:2 * L, cols] = cur
        both = ext_ref[:, cols]
        acc = convb_ref[:, cols] + convw_ref[SSD_D_CONV - 1:SSD_D_CONV, cols] * cur.astype(F32)
        for k in range(SSD_D_CONV - 1):
            acc = acc + convw_ref[k:k + 1, cols] * _dot(shifts[k], both)
        pieces.append(acc * jax.nn.sigmoid(acc))
        ext_ref[0:L, cols] = cur
    xs = jnp.concatenate(pieces[:2], axis=-1)
    b_all = pieces[2][:, :SSD_GN]
    c_all = pieces[2][:, SSD_GN:]

    dtr = dtr_ref[...] + dtb_ref[...]
    dt = jnp.maximum(dtr, 0.0) + jnp.log1p(jnp.exp(-jnp.abs(dtr)))
    a = -jnp.exp(alog_ref[...])
    da = dt * a
    row = lax.broadcasted_iota(jnp.int32, (L, L), 0)
    col = lax.broadcasted_iota(jnp.int32, (L, L), 1)
    causal = col <= row
    tril = jnp.where(causal, 1.0, 0.0).astype(BF16)
    d_hi, d_mid, d_lo = _split3(da)
    a_cum = _dot(tril, d_hi) + _dot(tril, d_mid) + _dot(tril, d_lo)
    a_cum_t = a_cum.T

    expand = expand_ref[...]
    dt_e = _dot(dt.astype(BF16), expand)
    w_e = _dot((dt * jnp.exp(a_cum[L - 1:L, :] - a_cum)).astype(BF16), expand)
    e_hi, e_lo = _split2(jnp.exp(a_cum))
    exp_acum_e = _dot(e_hi, expand) + _dot(e_lo, expand)
    cd_e = exp_acum_e[L - 1:L, :]

    xd_b = (xs * dt_e).astype(BF16)
    xdw_b = (xs * w_e).astype(BF16)

    lane = lax.broadcasted_iota(jnp.int32, (L, LANES), 1)
    first_half = lane < SSD_HEADDIM

    y_parts = []
    for g in range(SSD_N_GROUPS):
        gs = slice(g * SSD_D_STATE, (g + 1) * SSD_D_STATE)
        gw = slice(g * GW, (g + 1) * GW)
        b_g = b_all[:, gs]
        c_g = c_all[:, gs].astype(BF16)
        cb = _dot_nt(c_g, b_g.astype(BF16))
        y_pairs = []
        for jp in range(SSD_HEADS_PER_GROUP // 2):
            res = []
            pair_col = g * GW + jp * LANES
            xd_pair = xd_b[:, pair_col:pair_col + LANES]
            for sub in range(2):
                hd = g * SSD_HEADS_PER_GROUP + jp * 2 + sub
                diff = a_cum[:, hd:hd + 1] - a_cum_t[hd:hd + 1, :]
                dec = jnp.exp(jnp.where(causal, diff, -jnp.inf))
                res.append(_dot((cb * dec).astype(BF16), xd_pair))
            y_pairs.append(jnp.where(first_half, res[0], res[1]))
        y_diag = jnp.concatenate(y_pairs, axis=-1)
        st = state_ref[g]
        y_off = _dot(c_g, st.astype(BF16)) * exp_acum_e[:, gw]
        state_ref[g] = st * cd_e[:, gw] + _dot(b_g.T.astype(BF16), xdw_b[:, gw])
        y_parts.append(y_diag + y_off)
    y = jnp.concatenate(y_parts, axis=-1) + dexp_ref[...] * xs

    z = jnp.concatenate([z0_ref[...], z1_ref[...]], axis=-1).astype(F32)
    u = y * (z * jax.nn.sigmoid(z))
    u_parts = []
    for g in range(SSD_N_GROUPS):
        gw = slice(g * GW, (g + 1) * GW)
        ug = u[:, gw]
        ms = jnp.mean(ug * ug, axis=-1, keepdims=True)
        u_parts.append(ug * lax.rsqrt(ms + RMS_EPS) * ng_ref[:, gw])
    un = jnp.concatenate(u_parts, axis=-1).astype(BF16)
    o_ref[...] = h_ref[...] + _dot(un, wout_ref[...])


def ssd_core(zxbc, dt_raw, h, conv_w, conv_b, dt_bias, a_log, d_skip, norm_g, w_out_b):
    bsz, seq, _ = h.shape
    L = SSD_CHUNK
    nc = seq // L
    pad_heads = LANES - SSD_N_HEADS
    dtb = jnp.pad(dt_bias.astype(F32), (0, pad_heads)).reshape(1, LANES)
    alog = jnp.pad(a_log.astype(F32), (0, pad_heads)).reshape(1, LANES)
    dexp = jnp.repeat(d_skip.astype(F32), SSD_HEADDIM).reshape(1, SSD_D_INNER)
    head_of_col = jnp.arange(SSD_D_INNER, dtype=jnp.int32) // SSD_HEADDIM
    expand = (jnp.arange(LANES, dtype=jnp.int32)[:, None] == head_of_col[None, :]).astype(BF16)

    def zx_spec(k):
        return pl.BlockSpec((None, L, 1024), lambda b, c, k=k: (b, c, k))

    def const_spec(shape):
        return pl.BlockSpec(shape, lambda b, c: (0,) * len(shape))

    return pl.pallas_call(
        _ssd_kernel,
        grid=(bsz, nc),
        in_specs=[
            zx_spec(0), zx_spec(1), zx_spec(2), zx_spec(3), zx_spec(4),
            pl.BlockSpec((None, L, LANES), lambda b, c: (b, c, 0)),
            pl.BlockSpec((None, L, D_MODEL), lambda b, c: (b, c, 0)),
            const_spec((SSD_D_CONV, SSD_CONV_DIM)),
            const_spec((1, SSD_CONV_DIM)),
            const_spec((1, LANES)),
            const_spec((1, LANES)),
            const_spec((1, SSD_D_INNER)),
            const_spec((1, SSD_D_INNER)),
            const_spec((LANES, SSD_D_INNER)),
            const_spec((SSD_D_INNER, D_MODEL)),
        ],
        out_specs=pl.BlockSpec((None, L, D_MODEL), lambda b, c: (b, c, 0)),
        out_shape=jax.ShapeDtypeStruct((bsz, seq, D_MODEL), F32),
        scratch_shapes=[
            pltpu.VMEM((SSD_N_GROUPS, SSD_D_STATE, SSD_GROUP_WIDTH), F32),
            pltpu.VMEM((2 * L, SSD_CONV_DIM), BF16),
        ],
        compiler_params=_cparams("parallel", "arbitrary"),
        name="ssd_core",
    )(zxbc, zxbc, zxbc, zxbc, zxbc, dt_raw, h,
      conv_w.astype(F32), conv_b.astype(F32).reshape(1, SSD_CONV_DIM), dtb, alog, dexp,
      norm_g.astype(F32).reshape(1, SSD_D_INNER), expand, w_out_b)


def ssd_layer(h, norm_g_in, w_in, conv_w, conv_b, dt_bias, a_log, d_skip, norm_g, w_out):
    bsz, seq, d = h.shape
    h2 = h.reshape(bsz * seq, d)
    n_main = SSD_D_INNER + SSD_CONV_DIM
    w_main = w_in[:, :n_main].astype(BF16)
    w_dt = jnp.pad(w_in[:, n_main:], ((0, 0), (0, LANES - SSD_N_HEADS))).astype(BF16)
    zxbc = norm_matmul(h2, norm_g_in, w_main, BF16, tm=2048, tn=1024)
    dt_raw = norm_matmul(h2, norm_g_in, w_dt, F32, tm=1024, tn=LANES)
    return ssd_core(zxbc.reshape(bsz, seq, n_main), dt_raw.reshape(bsz, seq, LANES), h,
                    conv_w, conv_b, dt_bias, a_log, d_skip, norm_g, w_out.astype(BF16))


def _da_kernel(lq1_ref, lk1_ref, lq2_ref, lk2_ref, sub_ref, q_ref, k_ref, v_ref, o_ref,
               acc1_ref, acc2_ref, m1_ref, m2_ref, *, tq, lambda_init):
    i = pl.program_id(2)
    q = q_ref[...]
    lane = lax.broadcasted_iota(jnp.int32, (tq, LANES), 1)
    qs = (q.astype(F32) * (DA_HEAD_DIM ** -0.5 * LOG2_E)).astype(BF16)
    zero = jnp.zeros_like(qs)
    q_maps = (jnp.where(lane < DA_HEAD_DIM, qs, zero), jnp.where(lane >= DA_HEAD_DIM, qs, zero))
    states = ((m1_ref, acc1_ref), (m2_ref, acc2_ref))

    for m_ref, acc_ref in states:
        m_ref[...] = jnp.full((tq, LANES), -jnp.inf, F32)
        acc_ref[...] = jnp.zeros((tq, 2 * LANES), F32)

    ones = jnp.ones((tq, LANES), BF16)

    def step(j, masked):
        start = pl.multiple_of(j * tq, tq)
        kt = k_ref[pl.ds(start, tq), :]
        v_aug = jnp.concatenate([v_ref[pl.ds(start, tq), :], ones], axis=1)
        for qm, (m_ref, acc_ref) in zip(q_maps, states):
            s = _dot_nt(qm, kt)
            if masked:
                r = lax.broadcasted_iota(jnp.int32, (tq, tq), 0)
                cidx = lax.broadcasted_iota(jnp.int32, (tq, tq), 1)
                s = jnp.where(cidx <= r, s, -jnp.inf)
            m_old = m_ref[...]
            m_new = jnp.maximum(m_old, jnp.max(s, axis=-1, keepdims=True))
            alpha = jnp.exp2(m_old - m_new)
            p = jnp.exp2(s - jnp.concatenate([m_new] * (tq // LANES), axis=1))
            acc_ref[...] = (jnp.concatenate([alpha, alpha], axis=1) * acc_ref[...]
                            + _dot(p.astype(BF16), v_aug))
            m_ref[...] = m_new

    def body(jj, carry):
        for u in range(DA_KV_UNROLL):
            step(DA_KV_UNROLL * jj + u, False)
        return carry

    lax.fori_loop(0, i // DA_KV_UNROLL, body, 0)

    for rem in range(DA_KV_UNROLL):
        @pl.when(i % DA_KV_UNROLL == rem)
        def _(rem=rem):
            for u in range(rem):
                step(i - rem + u, False)
            step(i, True)

    lam = (jnp.exp(jnp.sum(lq1_ref[...] * lk1_ref[...], axis=-1, keepdims=True))
           - jnp.exp(jnp.sum(lq2_ref[...] * lk2_ref[...], axis=-1, keepdims=True)) + lambda_init)
    o1 = acc1_ref[:, :LANES] / acc1_ref[:, LANES:]
    o2 = acc2_ref[:, :LANES] / acc2_ref[:, LANES:]
    o = _rms(o1 - lam * o2, sub_ref[...]) * (1.0 - lambda_init)
    o_ref[...] = o.astype(o_ref.dtype)


def diff_attention_core(qkv, lq1, lk1, lq2, lk2, subln_g, layer_idx, tq=512):
    bsz, seq, _ = qkv.shape
    tq = min(tq, seq)
    lambda_init = 0.8 - 0.6 * math.exp(-0.3 * layer_idx)
    nh = DA_N_HEADS

    def vec_spec(n):
        return pl.BlockSpec((1, n), lambda b, h, i: (0, 0))

    return pl.pallas_call(
        functools.partial(_da_kernel, tq=tq, lambda_init=lambda_init),
        grid=(bsz, nh, seq // tq),
        in_specs=[
            vec_spec(DA_HEAD_DIM), vec_spec(DA_HEAD_DIM), vec_spec(DA_HEAD_DIM), vec_spec(DA_HEAD_DIM),
            vec_spec(LANES),
            pl.BlockSpec((None, tq, LANES), lambda b, h, i: (b, i, h)),
            pl.BlockSpec((None, seq, LANES), lambda b, h, i: (b, 0, nh + h)),
            pl.BlockSpec((None, seq, LANES), lambda b, h, i: (b, 0, 2 * nh + h)),
        ],
        out_specs=pl.BlockSpec((None, tq, LANES), lambda b, h, i: (b, i, h)),
        out_shape=jax.ShapeDtypeStruct((bsz, seq, D_MODEL), BF16),
        scratch_shapes=[
            pltpu.VMEM((tq, 2 * LANES), F32), pltpu.VMEM((tq, 2 * LANES), F32),
            pltpu.VMEM((tq, LANES), F32), pltpu.VMEM((tq, LANES), F32),
        ],
        compiler_params=_cparams("parallel", "parallel", "arbitrary"),
        name="diff_attention",
    )(lq1.astype(F32).reshape(1, -1), lk1.astype(F32).reshape(1, -1),
      lq2.astype(F32).reshape(1, -1), lk2.astype(F32).reshape(1, -1),
      subln_g.astype(F32).reshape(1, -1), qkv, qkv, qkv)


def da_layer(h, norm_g_in, w_qkv, lq1, lk1, lq2, lk2, subln_g, w_o, layer_idx):
    bsz, seq, d = h.shape
    h2 = h.reshape(bsz * seq, d)
    qkv = norm_matmul(h2, norm_g_in, w_qkv.astype(BF16), BF16, tm=2048, tn=1024)
    o = diff_attention_core(qkv.reshape(bsz, seq, 3 * d), lq1, lk1, lq2, lk2, subln_g, layer_idx)
    return matmul_residual(o.reshape(bsz * seq, d), w_o.astype(BF16), h2).reshape(bsz, seq, d)


def _xattn_kernel(h_ref, g_ref, wq_ref, kv_ref, wo_ref, o_ref):
    h = h_ref[...]
    hn = _rms(h, g_ref[...]).astype(BF16)
    scale = XA_HEAD_DIM ** -0.5
    q = (_dot(hn, wq_ref[...]) * scale).astype(BF16)
    outs = []
    for hd in range(XA_N_HEADS):
        cs = slice(hd * XA_HEAD_DIM, (hd + 1) * XA_HEAD_DIM)
        vs = slice(D_MODEL + hd * XA_HEAD_DIM, D_MODEL + (hd + 1) * XA_HEAD_DIM)
        s = _dot_nt(q[:, cs], kv_ref[:, cs])
        m = jnp.max(s, axis=-1, keepdims=True)
        p = jnp.exp(s - m)
        l = jnp.sum(p, axis=-1, keepdims=True)
        outs.append((_dot(p.astype(BF16), kv_ref[:, vs]) / l).astype(BF16))
    o = jnp.concatenate(outs, axis=-1)
    o_ref[...] = h + _dot(o, wo_ref[...])


def xattn_layer(h, mem, norm_g, mem_norm_g, w_q, w_kv, w_o, tq=1024):
    bsz, seq, d = h.shape
    mlen = mem.shape[1]
    tq = min(tq, seq)
    kv = norm_matmul(mem.reshape(bsz * mlen, d), mem_norm_g, w_kv.astype(BF16), BF16, tm=512, tn=1024)
    kv = kv.reshape(bsz, mlen, 2 * d)
    return pl.pallas_call(
        _xattn_kernel,
        grid=(bsz, seq // tq),
        in_specs=[
            pl.BlockSpec((None, tq, d), lambda b, i: (b, i, 0)),
            pl.BlockSpec((1, d), lambda b, i: (0, 0)),
            pl.BlockSpec((d, d), lambda b, i: (0, 0)),
            pl.BlockSpec((None, mlen, 2 * d), lambda b, i: (b, 0, 0)),
            pl.BlockSpec((d, d), lambda b, i: (0, 0)),
        ],
        out_specs=pl.BlockSpec((None, tq, d), lambda b, i: (b, i, 0)),
        out_shape=jax.ShapeDtypeStruct((bsz, seq, d), F32),
        compiler_params=_cparams("parallel", "parallel"),
        name="mem_xattn",
    )(h, norm_g.astype(F32).reshape(1, d), w_q.astype(BF16), kv, w_o.astype(BF16))


SLAB = D_MODEL // LANES


def _router_kernel(h_ref, g_ref, wh_ref, wl_ref, br_ref, su_ref,
                   slab_ref, idx_ref, gate_ref, rank_ref, cnt_ref, run_ref, *, tm):
    i = pl.program_id(0)

    @pl.when(i == 0)
    def _():
        run_ref[...] = jnp.zeros_like(run_ref)

    hn = _rms(h_ref[...], g_ref[...])
    for s in range(SLAB):
        slab_ref[pl.ds(s, tm, stride=SLAB), :] = hn[:, s * LANES:(s + 1) * LANES]

    x_hi, x_lo = _split2(hn)
    wh = wh_ref[...]
    logits = _dot_nt(wh, x_hi) + _dot_nt(wh, x_lo) + _dot_nt(wl_ref[...], x_hi) + br_ref[...]

    rows = lax.broadcasted_iota(jnp.int32, (N_EXPERTS, tm), 0).astype(F32)
    tops, idxs, onehots = [], [], []
    cur = logits
    for _ in range(TOP_K):
        m = jnp.max(cur, axis=0, keepdims=True)
        idx = jnp.min(jnp.where(cur == m, rows, float(N_EXPERTS)), axis=0, keepdims=True)
        oh = rows == idx
        cur = jnp.where(oh, -jnp.inf, cur)
        tops.append(m)
        idxs.append(idx)
        onehots.append(oh)
    exps = [jnp.exp(t - tops[0]) for t in tops]
    denom = exps[0] + exps[1] + exps[2] + exps[3]
    gate_ref[...] = jnp.concatenate([e / denom for e in exps], axis=0)
    idx_ref[...] = jnp.concatenate(idxs, axis=0).astype(jnp.int32)

    oh_sum = jnp.zeros((N_EXPERTS, tm), F32)
    for oh in onehots:
        oh_sum = oh_sum + jnp.where(oh, 1.0, 0.0)
    run = run_ref[...]
    prefix = _dot(oh_sum.astype(BF16), su_ref[...]) + run[:, 0:1]
    ranks = [jnp.sum(jnp.where(oh, prefix, 0.0), axis=0, keepdims=True) for oh in onehots]
    rank_ref[...] = jnp.concatenate(ranks, axis=0).astype(jnp.int32)
    run_new = run + jnp.sum(oh_sum, axis=1, keepdims=True)
    run_ref[...] = run_new
    cnt_ref[...] = run_new


def moe_router(h2, norm_g, w_router, b_router, tm=ROUTER_TILE):
    t, d = h2.shape
    tm = min(tm, t)
    wt = w_router.astype(F32).T
    wh = wt.astype(BF16)
    wl = (wt - wh.astype(F32)).astype(BF16)
    su = (jnp.arange(tm)[:, None] < jnp.arange(tm)[None, :]).astype(BF16)
    return pl.pallas_call(
        functools.partial(_router_kernel, tm=tm),
        grid=(t // tm,),
        in_specs=[
            pl.BlockSpec((tm, d), lambda i: (i, 0)),
            pl.BlockSpec((1, d), lambda i: (0, 0)),
            pl.BlockSpec((N_EXPERTS, d), lambda i: (0, 0)),
            pl.BlockSpec((N_EXPERTS, d), lambda i: (0, 0)),
            pl.BlockSpec((N_EXPERTS, 1), lambda i: (0, 0)),
            pl.BlockSpec((tm, tm), lambda i: (0, 0)),
        ],
        out_specs=[
            pl.BlockSpec((tm * SLAB, LANES), lambda i: (i, 0)),
            pl.BlockSpec((TOP_K, tm), lambda i: (0, i)),
            pl.BlockSpec((TOP_K, tm), lambda i: (0, i)),
            pl.BlockSpec((TOP_K, tm), lambda i: (0, i)),
            pl.BlockSpec((N_EXPERTS, LANES), lambda i: (0, 0)),
        ],
        out_shape=[
            jax.ShapeDtypeStruct((t * SLAB, LANES), F32),
            jax.ShapeDtypeStruct((TOP_K, t), jnp.int32),
            jax.ShapeDtypeStruct((TOP_K, t), F32),
            jax.ShapeDtypeStruct((TOP_K, t), jnp.int32),
            jax.ShapeDtypeStruct((N_EXPERTS, LANES), F32),
        ],
        scratch_shapes=[pltpu.VMEM((N_EXPERTS, LANES), F32)],
        compiler_params=_cparams("arbitrary"),
        name="moe_router",
    )(h2, norm_g.astype(F32).reshape(1, d), wh, wl, b_router.astype(F32).reshape(N_EXPERTS, 1), su)


def _slab_rows(r):
    return pl.ds(pl.multiple_of(r * SLAB, SLAB), SLAB)


def _dispatch_kernel(dest_ref, pad_ref, slab_ref, xs_ref, zero_ref, sem, *, tm, bm, nb):
    def zero_fill(action):
        def per_expert(e, carry):
            off = pad_ref[e]
            plen = pad_ref[N_EXPERTS + e]
            for bit in PAD_BITS[PAD_BITS.index(bm // 2):]:
                present = (plen & bit) != 0

                @pl.when(present)
                def _(off=off, bit=bit):
                    action(pltpu.make_async_copy(zero_ref.at[pl.ds(0, bit * SLAB), :],
                                                 xs_ref.at[pl.ds(pl.multiple_of(off * SLAB, SLAB), bit * SLAB), :],
                                                 sem))
                off = off + jnp.where(present, bit, 0)
            return carry

        lax.fori_loop(0, N_EXPERTS, per_expert, 0)

        def per_piece(p, carry):
            row = pl.multiple_of(p * PAD_PIECE * SLAB, PAD_PIECE * SLAB)
            action(pltpu.make_async_copy(zero_ref, xs_ref.at[pl.ds(row, PAD_PIECE * SLAB), :], sem))
            return carry

        lax.fori_loop(pad_ref[2 * N_EXPERTS] * (bm // PAD_PIECE), nb * (bm // PAD_PIECE), per_piece, 0)

    @pl.when(pl.program_id(0) == 0)
    def _():
        zero_ref[...] = jnp.zeros_like(zero_ref)
        zero_fill(lambda copy: copy.start())
        zero_fill(lambda copy: copy.wait())

    def row_copy(t, d):
        return pltpu.make_async_copy(slab_ref.at[_slab_rows(t), :], xs_ref.at[_slab_rows(d), :], sem)

    def issue(g, carry):
        for u in range(DMA_ISSUE_UNROLL):
            t = g * DMA_ISSUE_UNROLL + u
            for k in range(TOP_K):
                row_copy(t, dest_ref[0, 0, k * tm + t]).start(priority=k % 2)
        return carry

    lax.fori_loop(0, tm // DMA_ISSUE_UNROLL, issue, 0)

    for k in range(TOP_K):
        pltpu.make_async_copy(slab_ref, xs_ref.at[pl.ds(0, tm * SLAB), :], sem).wait()


def moe_dispatch(dest_tiles, pad_info, slab, p_rows, tm, bm):
    t = slab.shape[0] // SLAB
    assert bm % PAD_PIECE == 0 and bm // 2 in PAD_BITS
    return pl.pallas_call(
        functools.partial(_dispatch_kernel, tm=tm, bm=bm, nb=p_rows // bm),
        grid=(t // tm,),
        in_specs=[
            pl.BlockSpec((1, 1, TOP_K * tm), lambda i: (i, 0, 0), memory_space=pltpu.SMEM),
            pl.BlockSpec(memory_space=pltpu.SMEM),
            pl.BlockSpec((tm * SLAB, LANES), lambda i: (i, 0)),
        ],
        out_specs=pl.BlockSpec(memory_space=pl.ANY),
        out_shape=jax.ShapeDtypeStruct((p_rows * SLAB, LANES), F32),
        scratch_shapes=[pltpu.VMEM((PAD_PIECE * SLAB, LANES), F32), pltpu.SemaphoreType.DMA(())],
        compiler_params=_cparams("arbitrary"),
        name="moe_dispatch",
    )(dest_tiles, pad_info, slab)


def _expert_kernel(blk_e_ref, xs_ref, wgu_ref, bgu_ref, wd_ref, bd_ref, ys_ref, wgu_b_ref, wd_b_ref, *, bm, nb):
    i = pl.program_id(0)
    prev_e = blk_e_ref[jnp.maximum(i - 1, 0)]
    in_use = i < blk_e_ref[nb]

    @pl.when(in_use & ((i == 0) | (blk_e_ref[i] != prev_e)))
    def _():
        wgu_b_ref[...] = wgu_ref[...].astype(BF16)
        wd_b_ref[...] = wd_ref[...].astype(BF16)

    @pl.when(in_use)
    def _():
        x = jnp.concatenate([xs_ref[pl.ds(s, bm, stride=SLAB), :] for s in range(SLAB)], axis=-1)
        gu = _dot(x.astype(BF16), wgu_b_ref[...]) + bgu_ref[...]
        gate = jnp.minimum(gu[:, :D_FF], SWIGLU_LIMIT)
        up = jnp.clip(gu[:, D_FF:], -SWIGLU_LIMIT, SWIGLU_LIMIT)
        act = (up + 1.0) * (gate * jax.nn.sigmoid(gate * SWIGLU_ALPHA))
        y = _dot(act.astype(BF16), wd_b_ref[...]) + bd_ref[...]
        for s in range(SLAB):
            ys_ref[pl.ds(s, bm, stride=SLAB), :] = y[:, s * LANES:(s + 1) * LANES]

    @pl.when(jnp.logical_not(in_use))
    def _():
        ys_ref[...] = jnp.zeros_like(ys_ref)


def moe_experts(blk_e, xs, w_gate_up, b_gate_up, w_down, b_down, layer, bm):
    nb = blk_e.shape[0] - 1
    d = D_MODEL
    grid_spec = pltpu.PrefetchScalarGridSpec(
        num_scalar_prefetch=1,
        grid=(nb,),
        in_specs=[
            pl.BlockSpec((bm * SLAB, LANES), lambda i, be: (i, 0)),
            pl.BlockSpec((None, None, d, 2 * D_FF), lambda i, be: (layer, be[i], 0, 0)),
            pl.BlockSpec((None, 1, 2 * D_FF), lambda i, be: (be[i], 0, 0)),
            pl.BlockSpec((None, None, D_FF, d), lambda i, be: (layer, be[i], 0, 0)),
            pl.BlockSpec((None, 1, d), lambda i, be: (be[i], 0, 0)),
        ],
        out_specs=pl.BlockSpec((bm * SLAB, LANES), lambda i, be: (i, 0)),
        scratch_shapes=[pltpu.VMEM((d, 2 * D_FF), BF16), pltpu.VMEM((D_FF, d), BF16)],
    )
    return pl.pallas_call(
        functools.partial(_expert_kernel, bm=bm, nb=nb),
        grid_spec=grid_spec,
        out_shape=jax.ShapeDtypeStruct((nb * bm * SLAB, LANES), F32),
        compiler_params=_cparams("arbitrary"),
        name="moe_experts",
    )(blk_e, xs, w_gate_up, b_gate_up.astype(F32).reshape(N_EXPERTS, 1, 2 * D_FF),
      w_down, b_down.astype(F32).reshape(N_EXPERTS, 1, d))


def _combine_kernel(dest_ref, dest_next_ref, gate_ref, h_ref, fg_ref, ys_ref, o_ref, buf_a, buf_b, sem,
                    *, tm, final_norm):
    i = pl.program_id(0)
    n = pl.num_programs(0)
    n_rows = TOP_K * tm

    def row_copy(d_ref, j, buf, s):
        return pltpu.make_async_copy(ys_ref.at[_slab_rows(d_ref[0, 0, j]), :],
                                     buf.at[pl.ds(j * SLAB, SLAB), :], s)

    def wait_tile(buf, s):
        pltpu.make_async_copy(ys_ref.at[pl.ds(0, n_rows * SLAB), :], buf, s).wait()

    @pl.when(i == 0)
    def _():
        def issue(g, carry):
            for u in range(DMA_ISSUE_UNROLL):
                j = g * DMA_ISSUE_UNROLL + u
                pltpu.make_async_copy(ys_ref.at[_slab_rows(dest_ref[0, 0, j]), :],
                                      buf_a.at[_slab_rows(j), :], sem.at[0]).start(priority=u % 2)
            return carry
        lax.fori_loop(0, n_rows // DMA_ISSUE_UNROLL, issue, 0)

    def run(cur, cur_sem, nxt, nxt_sem):
        wait_tile(cur, cur_sem)
        gates = gate_ref[...]
        per_block = n_rows // SLAB
        for s in range(SLAB):
            for j in range(s * per_block, (s + 1) * per_block):
                row_copy(dest_next_ref, j, nxt, nxt_sem).start(priority=j % 2)
            cols = slice(s * LANES, (s + 1) * LANES)
            acc = h_ref[:, cols]
            for k in range(TOP_K):
                acc = acc + gates[:, k:k + 1] * cur[pl.ds(k * tm * SLAB + s, tm, stride=SLAB), :]
            o_ref[:, cols] = acc
        if final_norm:
            o_ref[...] = _rms(o_ref[...], fg_ref[...])

        @pl.when(i == n - 1)
        def _():
            wait_tile(nxt, nxt_sem)

    @pl.when(i % 2 == 0)
    def _():
        run(buf_a, sem.at[0], buf_b, sem.at[1])

    @pl.when(i % 2 == 1)
    def _():
        run(buf_b, sem.at[1], buf_a, sem.at[0])


def moe_combine(dest_tiles, gates_col, h2, ys, tm, final_g=None):
    t, d = h2.shape
    nt = t // tm
    fg = jnp.ones((1, d), F32) if final_g is None else final_g.astype(F32).reshape(1, d)
    return pl.pallas_call(
        functools.partial(_combine_kernel, tm=tm, final_norm=final_g is not None),
        grid=(nt,),
        in_specs=[
            pl.BlockSpec((1, 1, TOP_K * tm), lambda i: (i, 0, 0), memory_space=pltpu.SMEM),
            pl.BlockSpec((1, 1, TOP_K * tm), lambda i: (jnp.minimum(i + 1, nt - 1), 0, 0),
                         memory_space=pltpu.SMEM),
            pl.BlockSpec((tm, TOP_K), lambda i: (i, 0)),
            pl.BlockSpec((tm, d), lambda i: (i, 0)),
            pl.BlockSpec((1, d), lambda i: (0, 0)),
            pl.BlockSpec(memory_space=pl.ANY),
        ],
        out_specs=pl.BlockSpec((tm, d), lambda i: (i, 0)),
        out_shape=jax.ShapeDtypeStruct((t, d), F32),
        scratch_shapes=[pltpu.VMEM((TOP_K * tm * SLAB, LANES), F32), pltpu.VMEM((TOP_K * tm * SLAB, LANES), F32),
                        pltpu.SemaphoreType.DMA((2,))],
        compiler_params=_cparams("arbitrary"),
        name="moe_combine",
    )(dest_tiles, dest_tiles, gates_col, h2, fg, ys)


def _tile_major(a, tm):
    t = a.shape[1]
    return a.reshape(TOP_K, t // tm, tm).transpose(1, 0, 2).reshape(t // tm, 1, TOP_K * tm)


def moe_layer(h, norm_g, w_router, b_router, w_gate_up, b_gate_up, w_down, b_down, layer, final_g=None):
    bsz, seq, d = h.shape
    t = bsz * seq
    h2 = h.reshape(t, d)
    bm = MOE_ROWS_PER_BLOCK
    slab, idx, gates, rank, cnt = moe_router(h2, norm_g, w_router, b_router)

    counts = cnt[:, 0].astype(jnp.int32)
    padded = (counts + bm - 1) // bm * bm
    pend = jnp.cumsum(padded)
    pstart = pend - padded
    experts = jnp.arange(N_EXPERTS, dtype=jnp.int32)
    dest = rank + jnp.sum(jnp.where(idx[..., None] == experts, pstart, 0), axis=-1)
    n = t * TOP_K
    p_rows = -(-(n + N_EXPERTS * bm) // bm) * bm
    nb = p_rows // bm
    blk_start = jnp.arange(nb, dtype=jnp.int32) * bm
    blk_e = jnp.minimum(jnp.sum((pend[None, :] <= blk_start[:, None]).astype(jnp.int32), axis=1),
                        N_EXPERTS - 1)

    n_used = pend[N_EXPERTS - 1:] // bm
    pad_info = jnp.concatenate([pstart + counts, padded - counts, n_used]).astype(jnp.int32)
    blk_meta = jnp.concatenate([blk_e, n_used]).astype(jnp.int32)

    td = min(DISPATCH_TILE, t)
    xs = moe_dispatch(_tile_major(dest, td), pad_info, slab, p_rows, td, bm)
    ys = moe_experts(blk_meta, xs, w_gate_up.astype(F32), b_gate_up, w_down.astype(F32), b_down, layer, bm)
    tc = min(COMBINE_TILE, t)
    out = moe_combine(_tile_major(dest, tc), gates.T, h2, ys, tc, final_g)
    return out.reshape(bsz, seq, d)


def kernel(x, mem, mixer_norm, xattn_norm, mem_norm, ffn_norm, ssd_w_in, ssd_conv_w, ssd_conv_b, ssd_dt_bias, ssd_a_log, ssd_d, ssd_norm, ssd_w_out, da_w_qkv, da_lambda_q1, da_lambda_k1, da_lambda_q2, da_lambda_k2, da_subln, da_w_o, xa_w_q, xa_w_kv, xa_w_o, moe_w_router, moe_b_router, moe_w_gate_up, moe_b_gate_up, moe_w_down, moe_b_down, final_norm):
    depth = mixer_norm.shape[0]
    bsz, seq, d = x.shape
    h = x
    for i in range(depth):
        j = i // N_MIXERS
        if i % N_MIXERS == 0:
            h = ssd_layer(h, mixer_norm[i], ssd_w_in[j], ssd_conv_w[j], ssd_conv_b[j], ssd_dt_bias[j],
                          ssd_a_log[j], ssd_d[j], ssd_norm[j], ssd_w_out[j])
        else:
            h = da_layer(h, mixer_norm[i], da_w_qkv[j], da_lambda_q1[j], da_lambda_k1[j], da_lambda_q2[j],
                         da_lambda_k2[j], da_subln[j], da_w_o[j], i)
        h = xattn_layer(h, mem, xattn_norm[i], mem_norm[i], xa_w_q[i], xa_w_kv[i], xa_w_o[i])
        h = moe_layer(h, ffn_norm[i], moe_w_router[i], moe_b_router[i], moe_w_gate_up,
                      moe_b_gate_up[i], moe_w_down, moe_b_down[i], i,
                      final_g=final_norm if i == depth - 1 else None)
    return h
```

```python
import functools
import math

import jax
import jax.numpy as jnp
from jax import lax
from jax.experimental import pallas as pl
from jax.experimental.pallas import tpu as pltpu

F32 = jnp.float32
BF16 = jnp.bfloat16

D_MODEL = 1024
RMS_EPS = 1e-5
LOG2_E = 1.4426950408889634
N_MIXERS = 2

SSD_D_INNER = 2048
SSD_HEADDIM = 64
SSD_N_HEADS = 32
SSD_N_GROUPS = 4
SSD_HEADS_PER_GROUP = 8
SSD_D_STATE = 128
SSD_D_CONV = 4
SSD_CHUNK = 128
SSD_GN = 512
SSD_CONV_DIM = 3072
SSD_GROUP_WIDTH = SSD_D_INNER // SSD_N_GROUPS

DA_HEAD_DIM = 64
DA_N_HEADS = 8
DA_KV_UNROLL = 8

XA_N_HEADS = 4
XA_HEAD_DIM = 256

N_EXPERTS = 32
TOP_K = 4
D_FF = 1024
SWIGLU_LIMIT = 7.0
SWIGLU_ALPHA = 1.702

LANES = 128
SUBLANES = 8
VMEM_LIMIT_BYTES = 56 * 1024 * 1024

MOE_ROWS_PER_BLOCK = 512
ROUTER_TILE = 512
DISPATCH_TILE = 1024
COMBINE_TILE = 256
DMA_ISSUE_UNROLL = 8
PAD_PIECE = 256
PAD_BITS = (256, 128, 64, 32, 16, 8, 4, 2, 1)


def _cparams(*sem):
    return pltpu.CompilerParams(dimension_semantics=sem, vmem_limit_bytes=VMEM_LIMIT_BYTES)


def _rms(x, g):
    ms = jnp.mean(x * x, axis=-1, keepdims=True)
    return x * lax.rsqrt(ms + RMS_EPS) * g


def _dot(a, b):
    return jnp.dot(a, b, preferred_element_type=F32)


def _dot_nt(a, b):
    return lax.dot_general(a, b, (((1,), (1,)), ((), ())), preferred_element_type=F32)


def _split2(x):
    hi = x.astype(BF16)
    lo = (x - hi.astype(F32)).astype(BF16)
    return hi, lo


def _split3(x):
    hi = x.astype(BF16)
    r = x - hi.astype(F32)
    mid = r.astype(BF16)
    lo = (r - mid.astype(F32)).astype(BF16)
    return hi, mid, lo


def _norm_mm_kernel(x_ref, g_ref, w_ref, o_ref, xn_ref):
    @pl.when(pl.program_id(1) == 0)
    def _():
        xn_ref[...] = _rms(x_ref[...], g_ref[...]).astype(BF16)

    o_ref[...] = _dot(xn_ref[...], w_ref[...]).astype(o_ref.dtype)


def norm_matmul(x, g, w, out_dtype, tm, tn):
    m, k = x.shape
    n = w.shape[1]
    tm = min(tm, m)
    tn = min(tn, n)
    return pl.pallas_call(
        _norm_mm_kernel,
        grid=(m // tm, n // tn),
        in_specs=[
            pl.BlockSpec((tm, k), lambda i, j: (i, 0)),
            pl.BlockSpec((1, k), lambda i, j: (0, 0)),
            pl.BlockSpec((k, tn), lambda i, j: (0, j)),
        ],
        out_specs=pl.BlockSpec((tm, tn), lambda i, j: (i, j)),
        out_shape=jax.ShapeDtypeStruct((m, n), out_dtype),
        scratch_shapes=[pltpu.VMEM((tm, k), BF16)],
        compiler_params=_cparams("parallel", "arbitrary"),
        name="norm_matmul",
    )(x, g.reshape(1, k), w)


def _norm_mm_side_kernel(x_ref, g_ref, w_ref, ws_ref, o_ref, os_ref, xn_ref):
    @pl.when(pl.program_id(1) == 0)
    def _():
        xn = _rms(x_ref[...], g_ref[...]).astype(BF16)
        xn_ref[...] = xn
        os_ref[...] = _dot(xn, ws_ref[...])

    o_ref[...] = _dot(xn_ref[...], w_ref[...]).astype(o_ref.dtype)


def norm_matmul_with_side(x, g, w, w_side, out_dtype, tm, tn):
    m, k = x.shape
    n = w.shape[1]
    ns = w_side.shape[1]
    tm = min(tm, m)
    tn = min(tn, n)
    return pl.pallas_call(
        _norm_mm_side_kernel,
        grid=(m // tm, n // tn),
        in_specs=[
            pl.BlockSpec((tm, k), lambda i, j: (i, 0)),
            pl.BlockSpec((1, k), lambda i, j: (0, 0)),
            pl.BlockSpec((k, tn), lambda i, j: (0, j)),
            pl.BlockSpec((k, ns), lambda i, j: (0, 0)),
        ],
        out_specs=[pl.BlockSpec((tm, tn), lambda i, j: (i, j)), pl.BlockSpec((tm, ns), lambda i, j: (i, 0))],
        out_shape=[jax.ShapeDtypeStruct((m, n), out_dtype), jax.ShapeDtypeStruct((m, ns), F32)],
        scratch_shapes=[pltpu.VMEM((tm, k), BF16)],
        compiler_params=_cparams("parallel", "arbitrary"),
        name="norm_matmul_side",
    )(x, g.reshape(1, k), w, w_side)


def _mm_res_kernel(x_ref, w_ref, r_ref, o_ref):
    o_ref[...] = r_ref[...] + _dot(x_ref[...], w_ref[...])


def matmul_residual(x, w, res, tm=1024):
    m, k = x.shape
    n = w.shape[1]
    tm = min(tm, m)
    return pl.pallas_call(
        _mm_res_kernel,
        grid=(m // tm,),
        in_specs=[
            pl.BlockSpec((tm, k), lambda i: (i, 0)),
            pl.BlockSpec((k, n), lambda i: (0, 0)),
            pl.BlockSpec((tm, n), lambda i: (i, 0)),
        ],
        out_specs=pl.BlockSpec((tm, n), lambda i: (i, 0)),
        out_shape=jax.ShapeDtypeStruct((m, n), F32),
        compiler_params=_cparams("parallel"),
        name="matmul_residual",
    )(x, w, res)


def _ssd_kernel(z0_ref, z1_ref, x0_ref, x1_ref, bc_ref, dtr_ref, h_ref,
                convw_ref, convb_ref, dtb_ref, alog_ref, dexp_ref, ng_ref, expand_ref, wout_ref,
                o_ref, state_ref, ext_ref):
    L = SSD_CHUNK
    GW = SSD_GROUP_WIDTH
    c = pl.program_id(1)

    @pl.when(c == 0)
    def _():
        state_ref[...] = jnp.zeros_like(state_ref)
        ext_ref[0:L, :] = jnp.zeros((L, SSD_CONV_DIM), BF16)

    srow = lax.broadcasted_iota(jnp.int32, (L, 2 * L), 0)
    scol = lax.broadcasted_iota(jnp.int32, (L, 2 * L), 1)
    shifts = [jnp.where(scol == srow + (L - (SSD_D_CONV - 1) + k), 1.0, 0.0).astype(BF16)
              for k in range(SSD_D_CONV - 1)]
    pieces = []
    for blk, ref in enumerate((x0_ref, x1_ref, bc_ref)):
        cols = slice(blk * 1024, (blk + 1) * 1024)
        cur = ref[...]
        ext_ref[L:2 * L, cols] = cur
        both = ext_ref[:, cols]
        acc = convb_ref[:, cols] + convw_ref[SSD_D_CONV - 1:SSD_D_CONV, cols] * cur.astype(F32)
        for k in range(SSD_D_CONV - 1):
            acc = acc + convw_ref[k:k + 1, cols] * _dot(shifts[k], both)
        pieces.append(acc * jax.nn.sigmoid(acc))
        ext_ref[0:L, cols] = cur
    xs = jnp.concatenate(pieces[:2], axis=-1)
    b_all = pieces[2][:, :SSD_GN]
    c_all = pieces[2][:, SSD_GN:]

    dtr = dtr_ref[...] + dtb_ref[...]
    dt = jnp.maximum(dtr, 0.0) + jnp.log1p(jnp.exp(-jnp.abs(dtr)))
    a = -jnp.exp(alog_ref[...])
    da = dt * a
    row = lax.broadcasted_iota(jnp.int32, (L, L), 0)
    col = lax.broadcasted_iota(jnp.int32, (L, L), 1)
    causal = col <= row
    tril = jnp.where(causal, 1.0, 0.0).astype(BF16)
    d_hi, d_mid, d_lo = _split3(da)
    a_cum = _dot(tril, d_hi) + _dot(tril, d_mid) + _dot(tril, d_lo)
    a_cum_t = a_cum.T

    expand = expand_ref[...]
    dt_e = _dot(dt.astype(BF16), expand)
    w_e = _dot((dt * jnp.exp(a_cum[L - 1:L, :] - a_cum)).astype(BF16), expand)
    e_hi, e_lo = _split2(jnp.exp(a_cum))
    exp_acum_e = _dot(e_hi, expand) + _dot(e_lo, expand)
    cd_e = exp_acum_e[L - 1:L, :]

    xd_b = (xs * dt_e).astype(BF16)
    xdw_b = (xs * w_e).astype(BF16)

    lane = lax.broadcasted_iota(jnp.int32, (L, LANES), 1)
    first_half = lane < SSD_HEADDIM

    y_parts = []
    for g in range(SSD_N_GROUPS):
        gs = slice(g * SSD_D_STATE, (g + 1) * SSD_D_STATE)
        gw = slice(g * GW, (g + 1) * GW)
        b_g = b_all[:, gs]
        c_g = c_all[:, gs].astype(BF16)
        cb = _dot_nt(c_g, b_g.astype(BF16))
        y_pairs = []
        for jp in range(SSD_HEADS_PER_GROUP // 2):
            res = []
            pair_col = g * GW + jp * LANES
            xd_pair = xd_b[:, pair_col:pair_col + LANES]
            for sub in range(2):
                hd = g * SSD_HEADS_PER_GROUP + jp * 2 + sub
                diff = a_cum[:, hd:hd + 1] - a_cum_t[hd:hd + 1, :]
                dec = jnp.exp(jnp.where(causal, diff, -jnp.inf))
                res.append(_dot((cb * dec).astype(BF16), xd_pair))
            y_pairs.append(jnp.where(first_half, res[0], res[1]))
        y_diag = jnp.concatenate(y_pairs, axis=-1)
        st = state_ref[g]
        y_off = _dot(c_g, st.astype(BF16)) * exp_acum_e[:, gw]
        state_ref[g] = st * cd_e[:, gw] + _dot(b_g.T.astype(BF16), xdw_b[:, gw])
        y_parts.append(y_diag + y_off)
    y = jnp.concatenate(y_parts, axis=-1) + dexp_ref[...] * xs

    z = jnp.concatenate([z0_ref[...], z1_ref[...]], axis=-1).astype(F32)
    u = y * (z * jax.nn.sigmoid(z))
    u_parts = []
    for g in range(SSD_N_GROUPS):
        gw = slice(g * GW, (g + 1) * GW)
        ug = u[:, gw]
        ms = jnp.mean(ug * ug, axis=-1, keepdims=True)
        u_parts.append(ug * lax.rsqrt(ms + RMS_EPS) * ng_ref[:, gw])
    un = jnp.concatenate(u_parts, axis=-1).astype(BF16)
    o_ref[...] = h_ref[...] + _dot(un, wout_ref[...])


def ssd_core(zxbc, dt_raw, h, conv_w, conv_b, dt_bias, a_log, d_skip, norm_g, w_out_b):
    bsz, seq, _ = h.shape
    L = SSD_CHUNK
    nc = seq // L
    pad_heads = LANES - SSD_N_HEADS
    dtb = jnp.pad(dt_bias.astype(F32), (0, pad_heads)).reshape(1, LANES)
    alog = jnp.pad(a_log.astype(F32), (0, pad_heads)).reshape(1, LANES)
    dexp = jnp.repeat(d_skip.astype(F32), SSD_HEADDIM).reshape(1, SSD_D_INNER)
    head_of_col = jnp.arange(SSD_D_INNER, dtype=jnp.int32) // SSD_HEADDIM
    expand = (jnp.arange(LANES, dtype=jnp.int32)[:, None] == head_of_col[None, :]).astype(BF16)

    def zx_spec(k):
        return pl.BlockSpec((None, L, 1024), lambda b, c, k=k: (b, c, k))

    def const_spec(shape):
        return pl.BlockSpec(shape, lambda b, c: (0,) * len(shape))

    return pl.pallas_call(
        _ssd_kernel,
        grid=(bsz, nc),
        in_specs=[
            zx_spec(0), zx_spec(1), zx_spec(2), zx_spec(3), zx_spec(4),
            pl.BlockSpec((None, L, LANES), lambda b, c: (b, c, 0)),
            pl.BlockSpec((None, L, D_MODEL), lambda b, c: (b, c, 0)),
            const_spec((SSD_D_CONV, SSD_CONV_DIM)),
            const_spec((1, SSD_CONV_DIM)),
            const_spec((1, LANES)),
            const_spec((1, LANES)),
            const_spec((1, SSD_D_INNER)),
            const_spec((1, SSD_D_INNER)),
            const_spec((LANES, SSD_D_INNER)),
            const_spec((SSD_D_INNER, D_MODEL)),
        ],
        out_specs=pl.BlockSpec((None, L, D_MODEL), lambda b, c: (b, c, 0)),
        out_shape=jax.ShapeDtypeStruct((bsz, seq, D_MODEL), F32),
        scratch_shapes=[
            pltpu.VMEM((SSD_N_GROUPS, SSD_D_STATE, SSD_GROUP_WIDTH), F32),
            pltpu.VMEM((2 * L, SSD_CONV_DIM), BF16),
        ],
        compiler_params=_cparams("parallel", "arbitrary"),
        name="ssd_core",
    )(zxbc, zxbc, zxbc, zxbc, zxbc, dt_raw, h,
      conv_w.astype(F32), conv_b.astype(F32).reshape(1, SSD_CONV_DIM), dtb, alog, dexp,
      norm_g.astype(F32).reshape(1, SSD_D_INNER), expand, w_out_b)


def ssd_layer(h, norm_g_in, w_in, conv_w, conv_b, dt_bias, a_log, d_skip, norm_g, w_out):
    bsz, seq, d = h.shape
    h2 = h.reshape(bsz * seq, d)
    n_main = SSD_D_INNER + SSD_CONV_DIM
    w_main = w_in[:, :n_main].astype(BF16)
    w_dt = jnp.pad(w_in[:, n_main:], ((0, 0), (0, LANES - SSD_N_HEADS))).astype(BF16)
    zxbc, dt_raw = norm_matmul_with_side(h2, norm_g_in, w_main, w_dt, BF16, tm=2048, tn=1024)
    return ssd_core(zxbc.reshape(bsz, seq, n_main), dt_raw.reshape(bsz, seq, LANES), h,
                    conv_w, conv_b, dt_bias, a_log, d_skip, norm_g, w_out.astype(BF16))


def _da_kernel(lq1_ref, lk1_ref, lq2_ref, lk2_ref, sub_ref, q_ref, k_ref, v_ref, o_ref,
               acc1_ref, acc2_ref, m1_ref, m2_ref, *, tq, lambda_init):
    i = pl.program_id(2)
    q = q_ref[...]
    lane = lax.broadcasted_iota(jnp.int32, (tq, LANES), 1)
    qs = (q.astype(F32) * (DA_HEAD_DIM ** -0.5 * LOG2_E)).astype(BF16)
    zero = jnp.zeros_like(qs)
    q_maps = (jnp.where(lane < DA_HEAD_DIM, qs, zero), jnp.where(lane >= DA_HEAD_DIM, qs, zero))
    states = ((m1_ref, acc1_ref), (m2_ref, acc2_ref))

    for m_ref, acc_ref in states:
        m_ref[...] = jnp.full((tq, LANES), -jnp.inf, F32)
        acc_ref[...] = jnp.zeros((tq, 2 * LANES), F32)

    ones = jnp.ones((tq, LANES), BF16)

    def step(j, masked):
        start = pl.multiple_of(j * tq, tq)
        kt = k_ref[pl.ds(start, tq), :]
        v_aug = jnp.concatenate([v_ref[pl.ds(start, tq), :], ones], axis=1)
        for qm, (m_ref, acc_ref) in zip(q_maps, states):
            s = _dot_nt(qm, kt)
            if masked:
                r = lax.broadcasted_iota(jnp.int32, (tq, tq), 0)
                cidx = lax.broadcasted_iota(jnp.int32, (tq, tq), 1)
                s = jnp.where(cidx <= r, s, -jnp.inf)
            m_old = m_ref[...]
            m_new = jnp.maximum(m_old, jnp.max(s, axis=-1, keepdims=True))
            alpha = jnp.exp2(m_old - m_new)
            p = jnp.exp2(s - jnp.concatenate([m_new] * (tq // LANES), axis=1))
            acc_ref[...] = (jnp.concatenate([alpha, alpha], axis=1) * acc_ref[...]
                            + _dot(p.astype(BF16), v_aug))
            m_ref[...] = m_new

    def body(jj, carry):
        for u in range(DA_KV_UNROLL):
            step(DA_KV_UNROLL * jj + u, False)
        return carry

    lax.fori_loop(0, i // DA_KV_UNROLL, body, 0)

    for rem in range(DA_KV_UNROLL):
        @pl.when(i % DA_KV_UNROLL == rem)
        def _(rem=rem):
            for u in range(rem):
                step(i - rem + u, False)
            step(i, True)

    lam = (jnp.exp(jnp.sum(lq1_ref[...] * lk1_ref[...], axis=-1, keepdims=True))
           - jnp.exp(jnp.sum(lq2_ref[...] * lk2_ref[...], axis=-1, keepdims=True)) + lambda_init)
    o1 = acc1_ref[:, :LANES] / acc1_ref[:, LANES:]
    o2 = acc2_ref[:, :LANES] / acc2_ref[:, LANES:]
    o = _rms(o1 - lam * o2, sub_ref[...]) * (1.0 - lambda_init)
    o_ref[...] = o.astype(o_ref.dtype)


def diff_attention_core(qkv, lq1, lk1, lq2, lk2, subln_g, layer_idx, tq=512):
    bsz, seq, _ = qkv.shape
    tq = min(tq, seq)
    lambda_init = 0.8 - 0.6 * math.exp(-0.3 * layer_idx)
    nh = DA_N_HEADS

    def vec_spec(n):
        return pl.BlockSpec((1, n), lambda b, h, i: (0, 0))

    return pl.pallas_call(
        functools.partial(_da_kernel, tq=tq, lambda_init=lambda_init),
        grid=(bsz, nh, seq // tq),
        in_specs=[
            vec_spec(DA_HEAD_DIM), vec_spec(DA_HEAD_DIM), vec_spec(DA_HEAD_DIM), vec_spec(DA_HEAD_DIM),
            vec_spec(LANES),
            pl.BlockSpec((None, tq, LANES), lambda b, h, i: (b, i, h)),
            pl.BlockSpec((None, seq, LANES), lambda b, h, i: (b, 0, nh + h)),
            pl.BlockSpec((None, seq, LANES), lambda b, h, i: (b, 0, 2 * nh + h)),
        ],
        out_specs=pl.BlockSpec((None, tq, LANES), lambda b, h, i: (b, i, h)),
        out_shape=jax.ShapeDtypeStruct((bsz, seq, D_MODEL), BF16),
        scratch_shapes=[
            pltpu.VMEM((tq, 2 * LANES), F32), pltpu.VMEM((tq, 2 * LANES), F32),
            pltpu.VMEM((tq, LANES), F32), pltpu.VMEM((tq, LANES), F32),
        ],
        compiler_params=_cparams("parallel", "parallel", "arbitrary"),
        name="diff_attention",
    )(lq1.astype(F32).reshape(1, -1), lk1.astype(F32).reshape(1, -1),
      lq2.astype(F32).reshape(1, -1), lk2.astype(F32).reshape(1, -1),
      subln_g.astype(F32).reshape(1, -1), qkv, qkv, qkv)


def da_layer(h, norm_g_in, w_qkv, lq1, lk1, lq2, lk2, subln_g, w_o, layer_idx):
    bsz, seq, d = h.shape
    h2 = h.reshape(bsz * seq, d)
    qkv = norm_matmul(h2, norm_g_in, w_qkv.astype(BF16), BF16, tm=2048, tn=1024)
    o = diff_attention_core(qkv.reshape(bsz, seq, 3 * d), lq1, lk1, lq2, lk2, subln_g, layer_idx)
    return matmul_residual(o.reshape(bsz * seq, d), w_o.astype(BF16), h2).reshape(bsz, seq, d)


def _xattn_kernel(h_ref, g_ref, wq_ref, kv_ref, wo_ref, o_ref):
    h = h_ref[...]
    hn = _rms(h, g_ref[...]).astype(BF16)
    scale = XA_HEAD_DIM ** -0.5
    q = (_dot(hn, wq_ref[...]) * scale).astype(BF16)
    outs = []
    for hd in range(XA_N_HEADS):
        cs = slice(hd * XA_HEAD_DIM, (hd + 1) * XA_HEAD_DIM)
        vs = slice(D_MODEL + hd * XA_HEAD_DIM, D_MODEL + (hd + 1) * XA_HEAD_DIM)
        s = _dot_nt(q[:, cs], kv_ref[:, cs])
        m = jnp.max(s, axis=-1, keepdims=True)
        p = jnp.exp(s - m)
        l = jnp.sum(p, axis=-1, keepdims=True)
        outs.append((_dot(p.astype(BF16), kv_ref[:, vs]) / l).astype(BF16))
    o = jnp.concatenate(outs, axis=-1)
    o_ref[...] = h + _dot(o, wo_ref[...])


def xattn_layer(h, mem, norm_g, mem_norm_g, w_q, w_kv, w_o, tq=1024):
    bsz, seq, d = h.shape
    mlen = mem.shape[1]
    tq = min(tq, seq)
    kv = norm_matmul(mem.reshape(bsz * mlen, d), mem_norm_g, w_kv.astype(BF16), BF16, tm=512, tn=1024)
    kv = kv.reshape(bsz, mlen, 2 * d)
    return pl.pallas_call(
        _xattn_kernel,
        grid=(bsz, seq // tq),
        in_specs=[
            pl.BlockSpec((None, tq, d), lambda b, i: (b, i, 0)),
            pl.BlockSpec((1, d), lambda b, i: (0, 0)),
            pl.BlockSpec((d, d), lambda b, i: (0, 0)),
            pl.BlockSpec((None, mlen, 2 * d), lambda b, i: (b, 0, 0)),
            pl.BlockSpec((d, d), lambda b, i: (0, 0)),
        ],
        out_specs=pl.BlockSpec((None, tq, d), lambda b, i: (b, i, 0)),
        out_shape=jax.ShapeDtypeStruct((bsz, seq, d), F32),
        compiler_params=_cparams("parallel", "parallel"),
        name="mem_xattn",
    )(h, norm_g.astype(F32).reshape(1, d), w_q.astype(BF16), kv, w_o.astype(BF16))


SLAB = D_MODEL // LANES


def _router_kernel(h_ref, g_ref, wh_ref, wl_ref, br_ref, su_ref,
                   slab_ref, idx_ref, gate_ref, rank_ref, cnt_ref, run_ref, *, tm):
    i = pl.program_id(0)

    @pl.when(i == 0)
    def _():
        run_ref[...] = jnp.zeros_like(run_ref)

    hn = _rms(h_ref[...], g_ref[...])
    for s in range(SLAB):
        slab_ref[pl.ds(s, tm, stride=SLAB), :] = hn[:, s * LANES:(s + 1) * LANES]

    x_hi, x_lo = _split2(hn)
    wh = wh_ref[...]
    logits = _dot_nt(wh, x_hi) + _dot_nt(wh, x_lo) + _dot_nt(wl_ref[...], x_hi) + br_ref[...]

    rows = lax.broadcasted_iota(jnp.int32, (N_EXPERTS, tm), 0).astype(F32)
    tops, idxs, onehots = [], [], []
    cur = logits
    for _ in range(TOP_K):
        m = jnp.max(cur, axis=0, keepdims=True)
        idx = jnp.min(jnp.where(cur == m, rows, float(N_EXPERTS)), axis=0, keepdims=True)
        oh = rows == idx
        cur = jnp.where(oh, -jnp.inf, cur)
        tops.append(m)
        idxs.append(idx)
        onehots.append(oh)
    exps = [jnp.exp(t - tops[0]) for t in tops]
    denom = exps[0] + exps[1] + exps[2] + exps[3]
    gate_ref[...] = jnp.concatenate([e / denom for e in exps], axis=0)
    idx_ref[...] = jnp.concatenate(idxs, axis=0).astype(jnp.int32)

    oh_sum = jnp.zeros((N_EXPERTS, tm), F32)
    for oh in onehots:
        oh_sum = oh_sum + jnp.where(oh, 1.0, 0.0)
    run = run_ref[...]
    prefix = _dot(oh_sum.astype(BF16), su_ref[...]) + run[:, 0:1]
    ranks = [jnp.sum(jnp.where(oh, prefix, 0.0), axis=0, keepdims=True) for oh in onehots]
    rank_ref[...] = jnp.concatenate(ranks, axis=0).astype(jnp.int32)
    run_new = run + jnp.sum(oh_sum, axis=1, keepdims=True)
    run_ref[...] = run_new
    cnt_ref[...] = run_new


def moe_router(h2, norm_g, w_router, b_router, tm=ROUTER_TILE):
    t, d = h2.shape
    tm = min(tm, t)
    wt = w_router.astype(F32).T
    wh = wt.astype(BF16)
    wl = (wt - wh.astype(F32)).astype(BF16)
    su = (jnp.arange(tm)[:, None] < jnp.arange(tm)[None, :]).astype(BF16)
    return pl.pallas_call(
        functools.partial(_router_kernel, tm=tm),
        grid=(t // tm,),
        in_specs=[
            pl.BlockSpec((tm, d), lambda i: (i, 0)),
            pl.BlockSpec((1, d), lambda i: (0, 0)),
            pl.BlockSpec((N_EXPERTS, d), lambda i: (0, 0)),
            pl.BlockSpec((N_EXPERTS, d), lambda i: (0, 0)),
            pl.BlockSpec((N_EXPERTS, 1), lambda i: (0, 0)),
            pl.BlockSpec((tm, tm), lambda i: (0, 0)),
        ],
        out_specs=[
            pl.BlockSpec((tm * SLAB, LANES), lambda i: (i, 0)),
            pl.BlockSpec((TOP_K, tm), lambda i: (0, i)),
            pl.BlockSpec((TOP_K, tm), lambda i: (0, i)),
            pl.BlockSpec((TOP_K, tm), lambda i: (0, i)),
            pl.BlockSpec((N_EXPERTS, LANES), lambda i: (0, 0)),
        ],
        out_shape=[
            jax.ShapeDtypeStruct((t * SLAB, LANES), F32),
            jax.ShapeDtypeStruct((TOP_K, t), jnp.int32),
            jax.ShapeDtypeStruct((TOP_K, t), F32),
            jax.ShapeDtypeStruct((TOP_K, t), jnp.int32),
            jax.ShapeDtypeStruct((N_EXPERTS, LANES), F32),
        ],
        scratch_shapes=[pltpu.VMEM((N_EXPERTS, LANES), F32)],
        compiler_params=_cparams("arbitrary"),
        name="moe_router",
    )(h2, norm_g.astype(F32).reshape(1, d), wh, wl, b_router.astype(F32).reshape(N_EXPERTS, 1), su)


def _slab_rows(r):
    return pl.ds(pl.multiple_of(r * SLAB, SLAB), SLAB)


def _dispatch_kernel(dest_ref, pad_ref, slab_ref, xs_ref, zero_ref, sem, *, tm, bm, nb):
    def zero_fill(action):
        def per_expert(e, carry):
            off = pad_ref[e]
            plen = pad_ref[N_EXPERTS + e]
            for bit in PAD_BITS[PAD_BITS.index(bm // 2):]:
                present = (plen & bit) != 0

                @pl.when(present)
                def _(off=off, bit=bit):
                    action(pltpu.make_async_copy(zero_ref.at[pl.ds(0, bit * SLAB), :],
                                                 xs_ref.at[pl.ds(pl.multiple_of(off * SLAB, SLAB), bit * SLAB), :],
                                                 sem))
                off = off + jnp.where(present, bit, 0)
            return carry

        lax.fori_loop(0, N_EXPERTS, per_expert, 0)

        def per_piece(p, carry):
            row = pl.multiple_of(p * PAD_PIECE * SLAB, PAD_PIECE * SLAB)
            action(pltpu.make_async_copy(zero_ref, xs_ref.at[pl.ds(row, PAD_PIECE * SLAB), :], sem))
            return carry

        lax.fori_loop(pad_ref[2 * N_EXPERTS] * (bm // PAD_PIECE), nb * (bm // PAD_PIECE), per_piece, 0)

    @pl.when(pl.program_id(0) == 0)
    def _():
        zero_ref[...] = jnp.zeros_like(zero_ref)
        zero_fill(lambda copy: copy.start())
        zero_fill(lambda copy: copy.wait())

    def row_copy(t, d):
        return pltpu.make_async_copy(slab_ref.at[_slab_rows(t), :], xs_ref.at[_slab_rows(d), :], sem)

    def issue(g, carry):
        for u in range(DMA_ISSUE_UNROLL):
            t = g * DMA_ISSUE_UNROLL + u
            for k in range(TOP_K):
                row_copy(t, dest_ref[0, 0, k * tm + t]).start(priority=k % 2)
        return carry

    lax.fori_loop(0, tm // DMA_ISSUE_UNROLL, issue, 0)

    for k in range(TOP_K):
        pltpu.make_async_copy(slab_ref, xs_ref.at[pl.ds(0, tm * SLAB), :], sem).wait()


def moe_dispatch(dest_tiles, pad_info, slab, p_rows, tm, bm):
    t = slab.shape[0] // SLAB
    assert bm % PAD_PIECE == 0 and bm // 2 in PAD_BITS
    return pl.pallas_call(
        functools.partial(_dispatch_kernel, tm=tm, bm=bm, nb=p_rows // bm),
        grid=(t // tm,),
        in_specs=[
            pl.BlockSpec((1, 1, TOP_K * tm), lambda i: (i, 0, 0), memory_space=pltpu.SMEM),
            pl.BlockSpec(memory_space=pltpu.SMEM),
            pl.BlockSpec((tm * SLAB, LANES), lambda i: (i, 0)),
        ],
        out_specs=pl.BlockSpec(memory_space=pl.ANY),
        out_shape=jax.ShapeDtypeStruct((p_rows * SLAB, LANES), F32),
        scratch_shapes=[pltpu.VMEM((PAD_PIECE * SLAB, LANES), F32), pltpu.SemaphoreType.DMA(())],
        compiler_params=_cparams("arbitrary"),
        name="moe_dispatch",
    )(dest_tiles, pad_info, slab)


def _expert_kernel(blk_e_ref, xs_ref, wgu_ref, bgu_ref, wd_ref, bd_ref, ys_ref, wgu_b_ref, wd_b_ref, *, bm, nb):
    i = pl.program_id(0)
    prev_e = blk_e_ref[jnp.maximum(i - 1, 0)]
    in_use = i < blk_e_ref[nb]

    @pl.when(in_use & ((i == 0) | (blk_e_ref[i] != prev_e)))
    def _():
        wgu_b_ref[...] = wgu_ref[...].astype(BF16)
        wd_b_ref[...] = wd_ref[...].astype(BF16)

    @pl.when(in_use)
    def _():
        x = jnp.concatenate([xs_ref[pl.ds(s, bm, stride=SLAB), :] for s in range(SLAB)], axis=-1)
        gu = _dot(x.astype(BF16), wgu_b_ref[...]) + bgu_ref[...]
        gate = jnp.minimum(gu[:, :D_FF], SWIGLU_LIMIT)
        up = jnp.clip(gu[:, D_FF:], -SWIGLU_LIMIT, SWIGLU_LIMIT)
        act = (up + 1.0) * (gate * jax.nn.sigmoid(gate * SWIGLU_ALPHA))
        y = _dot(act.astype(BF16), wd_b_ref[...]) + bd_ref[...]
        for s in range(SLAB):
            ys_ref[pl.ds(s, bm, stride=SLAB), :] = y[:, s * LANES:(s + 1) * LANES]

    @pl.when(jnp.logical_not(in_use))
    def _():
        ys_ref[...] = jnp.zeros_like(ys_ref)


def moe_experts(blk_e, xs, w_gate_up, b_gate_up, w_down, b_down, layer, bm):
    nb = blk_e.shape[0] - 1
    d = D_MODEL
    grid_spec = pltpu.PrefetchScalarGridSpec(
        num_scalar_prefetch=1,
        grid=(nb,),
        in_specs=[
            pl.BlockSpec((bm * SLAB, LANES), lambda i, be: (i, 0)),
            pl.BlockSpec((None, None, d, 2 * D_FF), lambda i, be: (layer, be[i], 0, 0)),
            pl.BlockSpec((None, 1, 2 * D_FF), lambda i, be: (be[i], 0, 0)),
            pl.BlockSpec((None, None, D_FF, d), lambda i, be: (layer, be[i], 0, 0)),
            pl.BlockSpec((None, 1, d), lambda i, be: (be[i], 0, 0)),
        ],
        out_specs=pl.BlockSpec((bm * SLAB, LANES), lambda i, be: (i, 0)),
        scratch_shapes=[pltpu.VMEM((d, 2 * D_FF), BF16), pltpu.VMEM((D_FF, d), BF16)],
    )
    return pl.pallas_call(
        functools.partial(_expert_kernel, bm=bm, nb=nb),
        grid_spec=grid_spec,
        out_shape=jax.ShapeDtypeStruct((nb * bm * SLAB, LANES), F32),
        compiler_params=_cparams("arbitrary"),
        name="moe_experts",
    )(blk_e, xs, w_gate_up, b_gate_up.astype(F32).reshape(N_EXPERTS, 1, 2 * D_FF),
      w_down, b_down.astype(F32).reshape(N_EXPERTS, 1, d))


def _combine_kernel(dest_ref, dest_next_ref, gate_ref, h_ref, fg_ref, ys_ref, o_ref, buf_a, buf_b, sem,
                    *, tm, final_norm):
    i = pl.program_id(0)
    n = pl.num_programs(0)
    n_rows = TOP_K * tm

    def row_copy(d_ref, j, buf, s):
        return pltpu.make_async_copy(ys_ref.at[_slab_rows(d_ref[0, 0, j]), :],
                                     buf.at[pl.ds(j * SLAB, SLAB), :], s)

    def wait_tile(buf, s):
        pltpu.make_async_copy(ys_ref.at[pl.ds(0, n_rows * SLAB), :], buf, s).wait()

    @pl.when(i == 0)
    def _():
        def issue(g, carry):
            for u in range(DMA_ISSUE_UNROLL):
                j = g * DMA_ISSUE_UNROLL + u
                pltpu.make_async_copy(ys_ref.at[_slab_rows(dest_ref[0, 0, j]), :],
                                      buf_a.at[_slab_rows(j), :], sem.at[0]).start(priority=u % 2)
            return carry
        lax.fori_loop(0, n_rows // DMA_ISSUE_UNROLL, issue, 0)

    def run(cur, cur_sem, nxt, nxt_sem):
        wait_tile(cur, cur_sem)
        gates = gate_ref[...]
        per_block = n_rows // SLAB
        for s in range(SLAB):
            for j in range(s * per_block, (s + 1) * per_block):
                row_copy(dest_next_ref, j, nxt, nxt_sem).start(priority=j % 2)
            cols = slice(s * LANES, (s + 1) * LANES)
            acc = h_ref[:, cols]
            for k in range(TOP_K):
                acc = acc + gates[:, k:k + 1] * cur[pl.ds(k * tm * SLAB + s, tm, stride=SLAB), :]
            o_ref[:, cols] = acc
        if final_norm:
            o_ref[...] = _rms(o_ref[...], fg_ref[...])

        @pl.when(i == n - 1)
        def _():
            wait_tile(nxt, nxt_sem)

    @pl.when(i % 2 == 0)
    def _():
        run(buf_a, sem.at[0], buf_b, sem.at[1])

    @pl.when(i % 2 == 1)
    def _():
        run(buf_b, sem.at[1], buf_a, sem.at[0])


def moe_combine(dest_tiles, gates_col, h2, ys, tm, final_g=None):
    t, d = h2.shape
    nt = t // tm
    fg = jnp.ones((1, d), F32) if final_g is None else final_g.astype(F32).reshape(1, d)
    return pl.pallas_call(
        functools.partial(_combine_kernel, tm=tm, final_norm=final_g is not None),
        grid=(nt,),
        in_specs=[
            pl.BlockSpec((1, 1, TOP_K * tm), lambda i: (i, 0, 0), memory_space=pltpu.SMEM),
            pl.BlockSpec((1, 1, TOP_K * tm), lambda i: (jnp.minimum(i + 1, nt - 1), 0, 0),
                         memory_space=pltpu.SMEM),
            pl.BlockSpec((tm, TOP_K), lambda i: (i, 0)),
            pl.BlockSpec((tm, d), lambda i: (i, 0)),
            pl.BlockSpec((1, d), lambda i: (0, 0)),
            pl.BlockSpec(memory_space=pl.ANY),
        ],
        out_specs=pl.BlockSpec((tm, d), lambda i: (i, 0)),
        out_shape=jax.ShapeDtypeStruct((t, d), F32),
        scratch_shapes=[pltpu.VMEM((TOP_K * tm * SLAB, LANES), F32), pltpu.VMEM((TOP_K * tm * SLAB, LANES), F32),
                        pltpu.SemaphoreType.DMA((2,))],
        compiler_params=_cparams("arbitrary"),
        name="moe_combine",
    )(dest_tiles, dest_tiles, gates_col, h2, fg, ys)


def _tile_major(a, tm):
    t = a.shape[1]
    return a.reshape(TOP_K, t // tm, tm).transpose(1, 0, 2).reshape(t // tm, 1, TOP_K * tm)


def moe_layer(h, norm_g, w_router, b_router, w_gate_up, b_gate_up, w_down, b_down, layer, final_g=None):
    bsz, seq, d = h.shape
    t = bsz * seq
    h2 = h.reshape(t, d)
    bm = MOE_ROWS_PER_BLOCK
    slab, idx, gates, rank, cnt = moe_router(h2, norm_g, w_router, b_router)

    counts = cnt[:, 0].astype(jnp.int32)
    padded = (counts + bm - 1) // bm * bm
    pend = jnp.cumsum(padded)
    pstart = pend - padded
    experts = jnp.arange(N_EXPERTS, dtype=jnp.int32)
    dest = rank + jnp.sum(jnp.where(idx[..., None] == experts, pstart, 0), axis=-1)
    n = t * TOP_K
    p_rows = -(-(n + N_EXPERTS * bm) // bm) * bm
    nb = p_rows // bm
    blk_start = jnp.arange(nb, dtype=jnp.int32) * bm
    blk_e = jnp.minimum(jnp.sum((pend[None, :] <= blk_start[:, None]).astype(jnp.int32), axis=1),
                        N_EXPERTS - 1)

    n_used = pend[N_EXPERTS - 1:] // bm
    pad_info = jnp.concatenate([pstart + counts, padded - counts, n_used]).astype(jnp.int32)
    blk_meta = jnp.concatenate([blk_e, n_used]).astype(jnp.int32)

    td = min(DISPATCH_TILE, t)
    xs = moe_dispatch(_tile_major(dest, td), pad_info, slab, p_rows, td, bm)
    ys = moe_experts(blk_meta, xs, w_gate_up.astype(F32), b_gate_up, w_down.astype(F32), b_down, layer, bm)
    tc = min(COMBINE_TILE, t)
    out = moe_combine(_tile_major(dest, tc), gates.T, h2, ys, tc, final_g)
    return out.reshape(bsz, seq, d)


def kernel(x, mem, mixer_norm, xattn_norm, mem_norm, ffn_norm, ssd_w_in, ssd_conv_w, ssd_conv_b, ssd_dt_bias, ssd_a_log, ssd_d, ssd_norm, ssd_w_out, da_w_qkv, da_lambda_q1, da_lambda_k1, da_lambda_q2, da_lambda_k2, da_subln, da_w_o, xa_w_q, xa_w_kv, xa_w_o, moe_w_router, moe_b_router, moe_w_gate_up, moe_b_gate_up, moe_w_down, moe_b_down, final_norm):
    depth = mixer_norm.shape[0]
    bsz, seq, d = x.shape
    h = x
    for i in range(depth):
        j = i // N_MIXERS
        if i % N_MIXERS == 0:
            h = ssd_layer(h, mixer_norm[i], ssd_w_in[j], ssd_conv_w[j], ssd_conv_b[j], ssd_dt_bias[j],
                          ssd_a_log[j], ssd_d[j], ssd_norm[j], ssd_w_out[j])
        else:
            h = da_layer(h, mixer_norm[i], da_w_qkv[j], da_lambda_q1[j], da_lambda_k1[j], da_lambda_q2[j],
                         da_lambda_k2[j], da_subln[j], da_w_o[j], i)
        h = xattn_layer(h, mem, xattn_norm[i], mem_norm[i], xa_w_q[i], xa_w_kv[i], xa_w_o[i])
        h = moe_layer(h, ffn_norm[i], moe_w_router[i], moe_b_router[i], moe_w_gate_up,
                      moe_b_gate_up[i], moe_w_down, moe_b_down[i], i,
                      final_g=final_norm if i == depth - 1 else None)
    return h
```

```python
import functools
import math

import jax
import jax.numpy as jnp
from jax import lax
from jax.experimental import pallas as pl
from jax.experimental.pallas import tpu as pltpu

F32 = jnp.float32
BF16 = jnp.bfloat16

D_MODEL = 1024
RMS_EPS = 1e-5
LOG2_E = 1.4426950408889634
N_MIXERS = 2

SSD_D_INNER = 2048
SSD_HEADDIM = 64
SSD_N_HEADS = 32
SSD_N_GROUPS = 4
SSD_HEADS_PER_GROUP = 8
SSD_D_STATE = 128
SSD_D_CONV = 4
SSD_CHUNK = 128
SSD_GN = 512
SSD_CONV_DIM = 3072
SSD_GROUP_WIDTH = SSD_D_INNER // SSD_N_GROUPS
SSD_COL_BLOCK = 1024

DA_HEAD_DIM = 64
DA_N_HEADS = 8
DA_KV_UNROLL = 8

XA_N_HEADS = 4
XA_HEAD_DIM = 256

N_EXPERTS = 32
TOP_K = 4
D_FF = 1024
SWIGLU_LIMIT = 7.0
SWIGLU_ALPHA = 1.702

LANES = 128
SUBLANES = 8
VMEM_LIMIT_BYTES = 56 * 1024 * 1024

MOE_ROWS_PER_BLOCK = 512
ROUTER_TILE = 1024
DISPATCH_TILE = 1024
COMBINE_TILE = 256
DMA_ISSUE_UNROLL = 8
PAD_PIECE = 256
PAD_BITS = (256, 128, 64, 32, 16, 8, 4, 2, 1)


def _cparams(*sem):
    return pltpu.CompilerParams(dimension_semantics=sem, vmem_limit_bytes=VMEM_LIMIT_BYTES)


def _rms(x, g):
    ms = jnp.mean(x * x, axis=-1, keepdims=True)
    return x * lax.rsqrt(ms + RMS_EPS) * g


def _dot(a, b):
    return jnp.dot(a, b, preferred_element_type=F32)


def _dot_nt(a, b):
    return lax.dot_general(a, b, (((1,), (1,)), ((), ())), preferred_element_type=F32)


def _split2(x):
    hi = x.astype(BF16)
    lo = (x - hi.astype(F32)).astype(BF16)
    return hi, lo


def _split3(x):
    hi = x.astype(BF16)
    r = x - hi.astype(F32)
    mid = r.astype(BF16)
    lo = (r - mid.astype(F32)).astype(BF16)
    return hi, mid, lo


def _norm_mm_kernel(x_ref, g_ref, w_ref, o_ref, xn_ref):
    @pl.when(pl.program_id(1) == 0)
    def _():
        xn_ref[...] = _rms(x_ref[...], g_ref[...]).astype(BF16)

    o_ref[...] = _dot(xn_ref[...], w_ref[...]).astype(o_ref.dtype)


def norm_matmul(x, g, w, out_dtype, tm, tn):
    m, k = x.shape
    n = w.shape[1]
    tm = min(tm, m)
    tn = min(tn, n)
    return pl.pallas_call(
        _norm_mm_kernel,
        grid=(m // tm, n // tn),
        in_specs=[
            pl.BlockSpec((tm, k), lambda i, j: (i, 0)),
            pl.BlockSpec((1, k), lambda i, j: (0, 0)),
            pl.BlockSpec((k, tn), lambda i, j: (0, j)),
        ],
        out_specs=pl.BlockSpec((tm, tn), lambda i, j: (i, j)),
        out_shape=jax.ShapeDtypeStruct((m, n), out_dtype),
        scratch_shapes=[pltpu.VMEM((tm, k), BF16)],
        compiler_params=_cparams("parallel", "arbitrary"),
        name="norm_matmul",
    )(x, g.reshape(1, k), w)


def _norm_mm_side_kernel(x_ref, g_ref, w_ref, ws_ref, o_ref, os_ref, xn_ref):
    @pl.when(pl.program_id(1) == 0)
    def _():
        xn = _rms(x_ref[...], g_ref[...]).astype(BF16)
        xn_ref[...] = xn
        os_ref[...] = _dot(xn, ws_ref[...])

    o_ref[...] = _dot(xn_ref[...], w_ref[...]).astype(o_ref.dtype)


def norm_matmul_with_side(x, g, w, w_side, out_dtype, tm, tn):
    m, k = x.shape
    n = w.shape[1]
    ns = w_side.shape[1]
    tm = min(tm, m)
    tn = min(tn, n)
    return pl.pallas_call(
        _norm_mm_side_kernel,
        grid=(m // tm, n // tn),
        in_specs=[
            pl.BlockSpec((tm, k), lambda i, j: (i, 0)),
            pl.BlockSpec((1, k), lambda i, j: (0, 0)),
            pl.BlockSpec((k, tn), lambda i, j: (0, j)),
            pl.BlockSpec((k, ns), lambda i, j: (0, 0)),
        ],
        out_specs=[pl.BlockSpec((tm, tn), lambda i, j: (i, j)), pl.BlockSpec((tm, ns), lambda i, j: (i, 0))],
        out_shape=[jax.ShapeDtypeStruct((m, n), out_dtype), jax.ShapeDtypeStruct((m, ns), F32)],
        scratch_shapes=[pltpu.VMEM((tm, k), BF16)],
        compiler_params=_cparams("parallel", "arbitrary"),
        name="norm_matmul_side",
    )(x, g.reshape(1, k), w, w_side)


def _mm_res_kernel(x_ref, w_ref, r_ref, o_ref):
    o_ref[...] = r_ref[...] + _dot(x_ref[...], w_ref[...])


def matmul_residual(x, w, res, tm=1024):
    m, k = x.shape
    n = w.shape[1]
    tm = min(tm, m)
    return pl.pallas_call(
        _mm_res_kernel,
        grid=(m // tm,),
        in_specs=[
            pl.BlockSpec((tm, k), lambda i: (i, 0)),
            pl.BlockSpec((k, n), lambda i: (0, 0)),
            pl.BlockSpec((tm, n), lambda i: (i, 0)),
        ],
        out_specs=pl.BlockSpec((tm, n), lambda i: (i, 0)),
        out_shape=jax.ShapeDtypeStruct((m, n), F32),
        compiler_params=_cparams("parallel"),
        name="matmul_residual",
    )(x, w, res)


def _ssd_kernel(z0_ref, z1_ref, x0_ref, x1_ref, bc_ref, dtr_ref, h_ref,
                convw_ref, convb_ref, dtb_ref, alog_ref, dexp_ref, ng_ref, expand_ref, wout_ref,
                o_ref, state_ref, ext_ref):
    L = SSD_CHUNK
    GW = SSD_GROUP_WIDTH
    c = pl.program_id(1)

    @pl.when(c == 0)
    def _():
        state_ref[...] = jnp.zeros_like(state_ref)
        ext_ref[0:L, :] = jnp.zeros((L, SSD_CONV_DIM), BF16)

    srow = lax.broadcasted_iota(jnp.int32, (L, 2 * L), 0)
    scol = lax.broadcasted_iota(jnp.int32, (L, 2 * L), 1)
    shifts = [jnp.where(scol == srow + (L - (SSD_D_CONV - 1) + k), 1.0, 0.0).astype(BF16)
              for k in range(SSD_D_CONV - 1)]
    pieces = []
    for blk, ref in enumerate((x0_ref, x1_ref, bc_ref)):
        cols = slice(blk * SSD_COL_BLOCK, (blk + 1) * SSD_COL_BLOCK)
        cur = ref[...]
        ext_ref[L:2 * L, cols] = cur
        both = ext_ref[:, cols]
        acc = convb_ref[:, cols] + convw_ref[SSD_D_CONV - 1:SSD_D_CONV, cols] * cur.astype(F32)
        for k in range(SSD_D_CONV - 1):
            acc = acc + convw_ref[k:k + 1, cols] * _dot(shifts[k], both)
        pieces.append(acc * jax.nn.sigmoid(acc))
        ext_ref[0:L, cols] = cur
    xs = jnp.concatenate(pieces[:2], axis=-1)
    b_all = pieces[2][:, :SSD_GN]
    c_all = pieces[2][:, SSD_GN:]

    dtr = dtr_ref[...] + dtb_ref[...]
    dt = jnp.maximum(dtr, 0.0) + jnp.log1p(jnp.exp(-jnp.abs(dtr)))
    a = -jnp.exp(alog_ref[...])
    da = dt * a
    row = lax.broadcasted_iota(jnp.int32, (L, L), 0)
    col = lax.broadcasted_iota(jnp.int32, (L, L), 1)
    causal = col <= row
    tril = jnp.where(causal, 1.0, 0.0).astype(BF16)
    d_hi, d_mid, d_lo = _split3(da)
    a_cum = _dot(tril, d_hi) + _dot(tril, d_mid) + _dot(tril, d_lo)
    a_cum_t = a_cum.T

    expand = expand_ref[...]
    dt_e = _dot(dt.astype(BF16), expand)
    w_e = _dot((dt * jnp.exp(a_cum[L - 1:L, :] - a_cum)).astype(BF16), expand)
    e_hi, e_lo = _split2(jnp.exp(a_cum))
    exp_acum_e = _dot(e_hi, expand) + _dot(e_lo, expand)
    cd_e = exp_acum_e[L - 1:L, :]

    xd_b = (xs * dt_e).astype(BF16)
    xdw_b = (xs * w_e).astype(BF16)

    lane = lax.broadcasted_iota(jnp.int32, (L, LANES), 1)
    first_half = lane < SSD_HEADDIM

    y_parts = []
    for g in range(SSD_N_GROUPS):
        gs = slice(g * SSD_D_STATE, (g + 1) * SSD_D_STATE)
        gw = slice(g * GW, (g + 1) * GW)
        b_g = b_all[:, gs]
        c_g = c_all[:, gs].astype(BF16)
        cb = _dot_nt(c_g, b_g.astype(BF16))
        y_pairs = []
        for jp in range(SSD_HEADS_PER_GROUP // 2):
            res = []
            pair_col = g * GW + jp * LANES
            xd_pair = xd_b[:, pair_col:pair_col + LANES]
            for sub in range(2):
                hd = g * SSD_HEADS_PER_GROUP + jp * 2 + sub
                diff = a_cum[:, hd:hd + 1] - a_cum_t[hd:hd + 1, :]
                dec = jnp.exp(jnp.where(causal, diff, -jnp.inf))
                res.append(_dot((cb * dec).astype(BF16), xd_pair))
            y_pairs.append(jnp.where(first_half, res[0], res[1]))
        y_diag = jnp.concatenate(y_pairs, axis=-1)
        st = state_ref[g]
        y_off = _dot(c_g, st.astype(BF16)) * exp_acum_e[:, gw]
        state_ref[g] = st * cd_e[:, gw] + _dot(b_g.T.astype(BF16), xdw_b[:, gw])
        y_parts.append(y_diag + y_off)
    y = jnp.concatenate(y_parts, axis=-1) + dexp_ref[...] * xs

    z = jnp.concatenate([z0_ref[...], z1_ref[...]], axis=-1).astype(F32)
    u = y * (z * jax.nn.sigmoid(z))
    u_parts = []
    for g in range(SSD_N_GROUPS):
        gw = slice(g * GW, (g + 1) * GW)
        ug = u[:, gw]
        ms = jnp.mean(ug * ug, axis=-1, keepdims=True)
        u_parts.append(ug * lax.rsqrt(ms + RMS_EPS) * ng_ref[:, gw])
    un = jnp.concatenate(u_parts, axis=-1).astype(BF16)
    o_ref[...] = h_ref[...] + _dot(un, wout_ref[...])


def ssd_core(zxbc, dt_raw, h, conv_w, conv_b, dt_bias, a_log, d_skip, norm_g, w_out_b):
    bsz, seq, _ = h.shape
    L = SSD_CHUNK
    nc = seq // L
    pad_heads = LANES - SSD_N_HEADS
    dtb = jnp.pad(dt_bias.astype(F32), (0, pad_heads)).reshape(1, LANES)
    alog = jnp.pad(a_log.astype(F32), (0, pad_heads)).reshape(1, LANES)
    dexp = jnp.repeat(d_skip.astype(F32), SSD_HEADDIM).reshape(1, SSD_D_INNER)
    head_of_col = jnp.arange(SSD_D_INNER, dtype=jnp.int32) // SSD_HEADDIM
    expand = (jnp.arange(LANES, dtype=jnp.int32)[:, None] == head_of_col[None, :]).astype(BF16)

    def zx_spec(k):
        return pl.BlockSpec((None, L, SSD_COL_BLOCK), lambda b, c, k=k: (b, c, k))

    def const_spec(shape):
        return pl.BlockSpec(shape, lambda b, c: (0,) * len(shape))

    return pl.pallas_call(
        _ssd_kernel,
        grid=(bsz, nc),
        in_specs=[
            zx_spec(0), zx_spec(1), zx_spec(2), zx_spec(3), zx_spec(4),
            pl.BlockSpec((None, L, LANES), lambda b, c: (b, c, 0)),
            pl.BlockSpec((None, L, D_MODEL), lambda b, c: (b, c, 0)),
            const_spec((SSD_D_CONV, SSD_CONV_DIM)),
            const_spec((1, SSD_CONV_DIM)),
            const_spec((1, LANES)),
            const_spec((1, LANES)),
            const_spec((1, SSD_D_INNER)),
            const_spec((1, SSD_D_INNER)),
            const_spec((LANES, SSD_D_INNER)),
            const_spec((SSD_D_INNER, D_MODEL)),
        ],
        out_specs=pl.BlockSpec((None, L, D_MODEL), lambda b, c: (b, c, 0)),
        out_shape=jax.ShapeDtypeStruct((bsz, seq, D_MODEL), F32),
        scratch_shapes=[
            pltpu.VMEM((SSD_N_GROUPS, SSD_D_STATE, SSD_GROUP_WIDTH), F32),
            pltpu.VMEM((2 * L, SSD_CONV_DIM), BF16),
        ],
        compiler_params=_cparams("parallel", "arbitrary"),
        name="ssd_core",
    )(zxbc, zxbc, zxbc, zxbc, zxbc, dt_raw, h,
      conv_w.astype(F32), conv_b.astype(F32).reshape(1, SSD_CONV_DIM), dtb, alog, dexp,
      norm_g.astype(F32).reshape(1, SSD_D_INNER), expand, w_out_b)


def ssd_layer(h, norm_g_in, w_in, conv_w, conv_b, dt_bias, a_log, d_skip, norm_g, w_out):
    bsz, seq, d = h.shape
    h2 = h.reshape(bsz * seq, d)
    n_main = SSD_D_INNER + SSD_CONV_DIM
    w_main = w_in[:, :n_main].astype(BF16)
    w_dt = jnp.pad(w_in[:, n_main:], ((0, 0), (0, LANES - SSD_N_HEADS))).astype(BF16)
    zxbc, dt_raw = norm_matmul_with_side(h2, norm_g_in, w_main, w_dt, BF16, tm=2048, tn=1024)
    return ssd_core(zxbc.reshape(bsz, seq, n_main), dt_raw.reshape(bsz, seq, LANES), h,
                    conv_w, conv_b, dt_bias, a_log, d_skip, norm_g, w_out.astype(BF16))


def _da_kernel(lq1_ref, lk1_ref, lq2_ref, lk2_ref, sub_ref, q_ref, k_ref, v_ref, o_ref,
               acc1_ref, acc2_ref, m1_ref, m2_ref, *, tq, lambda_init):
    i = pl.program_id(2)
    q = q_ref[...]
    lane = lax.broadcasted_iota(jnp.int32, (tq, LANES), 1)
    qs = (q.astype(F32) * (DA_HEAD_DIM ** -0.5 * LOG2_E)).astype(BF16)
    zero = jnp.zeros_like(qs)
    q_maps = (jnp.where(lane < DA_HEAD_DIM, qs, zero), jnp.where(lane >= DA_HEAD_DIM, qs, zero))
    states = ((m1_ref, acc1_ref), (m2_ref, acc2_ref))

    for m_ref, acc_ref in states:
        m_ref[...] = jnp.full((tq, LANES), -jnp.inf, F32)
        acc_ref[...] = jnp.zeros((tq, 2 * LANES), F32)

    ones = jnp.ones((tq, LANES), BF16)

    def step(j, masked):
        start = pl.multiple_of(j * tq, tq)
        kt = k_ref[pl.ds(start, tq), :]
        v_aug = jnp.concatenate([v_ref[pl.ds(start, tq), :], ones], axis=1)
        for qm, (m_ref, acc_ref) in zip(q_maps, states):
            s = _dot_nt(qm, kt)
            if masked:
                r = lax.broadcasted_iota(jnp.int32, (tq, tq), 0)
                cidx = lax.broadcasted_iota(jnp.int32, (tq, tq), 1)
                s = jnp.where(cidx <= r, s, -jnp.inf)
            m_old = m_ref[...]
            m_new = jnp.maximum(m_old, jnp.max(s, axis=-1, keepdims=True))
            alpha = jnp.exp2(m_old - m_new)
            p = jnp.exp2(s - jnp.concatenate([m_new] * (tq // LANES), axis=1))
            acc_ref[...] = (jnp.concatenate([alpha, alpha], axis=1) * acc_ref[...]
                            + _dot(p.astype(BF16), v_aug))
            m_ref[...] = m_new

    def body(jj, carry):
        for u in range(DA_KV_UNROLL):
            step(DA_KV_UNROLL * jj + u, False)
        return carry

    lax.fori_loop(0, i // DA_KV_UNROLL, body, 0)

    for rem in range(DA_KV_UNROLL):
        @pl.when(i % DA_KV_UNROLL == rem)
        def _(rem=rem):
            for u in range(rem):
                step(i - rem + u, False)
            step(i, True)

    lam = (jnp.exp(jnp.sum(lq1_ref[...] * lk1_ref[...], axis=-1, keepdims=True))
           - jnp.exp(jnp.sum(lq2_ref[...] * lk2_ref[...], axis=-1, keepdims=True)) + lambda_init)
    o1 = acc1_ref[:, :LANES] / acc1_ref[:, LANES:]
    o2 = acc2_ref[:, :LANES] / acc2_ref[:, LANES:]
    o = _rms(o1 - lam * o2, sub_ref[...]) * (1.0 - lambda_init)
    o_ref[...] = o.astype(o_ref.dtype)


def diff_attention_core(qkv, lq1, lk1, lq2, lk2, subln_g, layer_idx, tq=512):
    bsz, seq, _ = qkv.shape
    tq = min(tq, seq)
    lambda_init = 0.8 - 0.6 * math.exp(-0.3 * layer_idx)
    nh = DA_N_HEADS

    def vec_spec(n):
        return pl.BlockSpec((1, n), lambda b, h, i: (0, 0))

    return pl.pallas_call(
        functools.partial(_da_kernel, tq=tq, lambda_init=lambda_init),
        grid=(bsz, nh, seq // tq),
        in_specs=[
            vec_spec(DA_HEAD_DIM), vec_spec(DA_HEAD_DIM), vec_spec(DA_HEAD_DIM), vec_spec(DA_HEAD_DIM),
            vec_spec(LANES),
            pl.BlockSpec((None, tq, LANES), lambda b, h, i: (b, i, h)),
            pl.BlockSpec((None, seq, LANES), lambda b, h, i: (b, 0, nh + h)),
            pl.BlockSpec((None, seq, LANES), lambda b, h, i: (b, 0, 2 * nh + h)),
        ],
        out_specs=pl.BlockSpec((None, tq, LANES), lambda b, h, i: (b, i, h)),
        out_shape=jax.ShapeDtypeStruct((bsz, seq, D_MODEL), BF16),
        scratch_shapes=[
            pltpu.VMEM((tq, 2 * LANES), F32), pltpu.VMEM((tq, 2 * LANES), F32),
            pltpu.VMEM((tq, LANES), F32), pltpu.VMEM((tq, LANES), F32),
        ],
        compiler_params=_cparams("parallel", "parallel", "arbitrary"),
        name="diff_attention",
    )(lq1.astype(F32).reshape(1, -1), lk1.astype(F32).reshape(1, -1),
      lq2.astype(F32).reshape(1, -1), lk2.astype(F32).reshape(1, -1),
      subln_g.astype(F32).reshape(1, -1), qkv, qkv, qkv)


def da_layer(h, norm_g_in, w_qkv, lq1, lk1, lq2, lk2, subln_g, w_o, layer_idx):
    bsz, seq, d = h.shape
    h2 = h.reshape(bsz * seq, d)
    qkv = norm_matmul(h2, norm_g_in, w_qkv.astype(BF16), BF16, tm=2048, tn=1024)
    o = diff_attention_core(qkv.reshape(bsz, seq, 3 * d), lq1, lk1, lq2, lk2, subln_g, layer_idx)
    return matmul_residual(o.reshape(bsz * seq, d), w_o.astype(BF16), h2).reshape(bsz, seq, d)


def _xattn_kernel(h_ref, g_ref, wq_ref, kv_ref, wo_ref, o_ref):
    h = h_ref[...]
    hn = _rms(h, g_ref[...]).astype(BF16)
    scale = XA_HEAD_DIM ** -0.5
    q = (_dot(hn, wq_ref[...]) * scale).astype(BF16)
    outs = []
    for hd in range(XA_N_HEADS):
        cs = slice(hd * XA_HEAD_DIM, (hd + 1) * XA_HEAD_DIM)
        vs = slice(D_MODEL + hd * XA_HEAD_DIM, D_MODEL + (hd + 1) * XA_HEAD_DIM)
        s = _dot_nt(q[:, cs], kv_ref[:, cs])
        m = jnp.max(s, axis=-1, keepdims=True)
        p = jnp.exp(s - m)
        l = jnp.sum(p, axis=-1, keepdims=True)
        outs.append((_dot(p.astype(BF16), kv_ref[:, vs]) / l).astype(BF16))
    o = jnp.concatenate(outs, axis=-1)
    o_ref[...] = h + _dot(o, wo_ref[...])


def xattn_layer(h, mem, norm_g, mem_norm_g, w_q, w_kv, w_o, tq=1024):
    bsz, seq, d = h.shape
    mlen = mem.shape[1]
    tq = min(tq, seq)
    kv = norm_matmul(mem.reshape(bsz * mlen, d), mem_norm_g, w_kv.astype(BF16), BF16, tm=512, tn=1024)
    kv = kv.reshape(bsz, mlen, 2 * d)
    return pl.pallas_call(
        _xattn_kernel,
        grid=(bsz, seq // tq),
        in_specs=[
            pl.BlockSpec((None, tq, d), lambda b, i: (b, i, 0)),
            pl.BlockSpec((1, d), lambda b, i: (0, 0)),
            pl.BlockSpec((d, d), lambda b, i: (0, 0)),
            pl.BlockSpec((None, mlen, 2 * d), lambda b, i: (b, 0, 0)),
            pl.BlockSpec((d, d), lambda b, i: (0, 0)),
        ],
        out_specs=pl.BlockSpec((None, tq, d), lambda b, i: (b, i, 0)),
        out_shape=jax.ShapeDtypeStruct((bsz, seq, d), F32),
        compiler_params=_cparams("parallel", "parallel"),
        name="mem_xattn",
    )(h, norm_g.astype(F32).reshape(1, d), w_q.astype(BF16), kv, w_o.astype(BF16))


SLAB = D_MODEL // LANES


def _router_kernel(h_ref, g_ref, wh_ref, wl_ref, br_ref, su_ref,
                   slab_ref, idx_ref, gate_ref, rank_ref, cnt_ref, run_ref, *, tm):
    i = pl.program_id(0)

    @pl.when(i == 0)
    def _():
        run_ref[...] = jnp.zeros_like(run_ref)

    hn = _rms(h_ref[...], g_ref[...])
    for s in range(SLAB):
        slab_ref[pl.ds(s, tm, stride=SLAB), :] = hn[:, s * LANES:(s + 1) * LANES]

    x_hi, x_lo = _split2(hn)
    wh = wh_ref[...]
    logits = _dot_nt(wh, x_hi) + _dot_nt(wh, x_lo) + _dot_nt(wl_ref[...], x_hi) + br_ref[...]

    rows = lax.broadcasted_iota(jnp.int32, (N_EXPERTS, tm), 0).astype(F32)
    tops, idxs, onehots = [], [], []
    cur = logits
    for _ in range(TOP_K):
        m = jnp.max(cur, axis=0, keepdims=True)
        idx = jnp.min(jnp.where(cur == m, rows, float(N_EXPERTS)), axis=0, keepdims=True)
        oh = rows == idx
        cur = jnp.where(oh, -jnp.inf, cur)
        tops.append(m)
        idxs.append(idx)
        onehots.append(oh)
    exps = [jnp.exp(t - tops[0]) for t in tops]
    denom = exps[0] + exps[1] + exps[2] + exps[3]
    gate_ref[...] = jnp.concatenate([e / denom for e in exps], axis=0)
    idx_ref[...] = jnp.concatenate(idxs, axis=0).astype(jnp.int32)

    oh_sum = jnp.zeros((N_EXPERTS, tm), F32)
    for oh in onehots:
        oh_sum = oh_sum + jnp.where(oh, 1.0, 0.0)
    run = run_ref[...]
    prefix = _dot(oh_sum.astype(BF16), su_ref[...]) + run[:, 0:1]
    ranks = [jnp.sum(jnp.where(oh, prefix, 0.0), axis=0, keepdims=True) for oh in onehots]
    rank_ref[...] = jnp.concatenate(ranks, axis=0).astype(jnp.int32)
    run_new = run + jnp.sum(oh_sum, axis=1, keepdims=True)
    run_ref[...] = run_new
    cnt_ref[...] = run_new


def moe_router(h2, norm_g, w_router, b_router, tm=ROUTER_TILE):
    t, d = h2.shape
    tm = min(tm, t)
    wt = w_router.astype(F32).T
    wh = wt.astype(BF16)
    wl = (wt - wh.astype(F32)).astype(BF16)
    su = (jnp.arange(tm)[:, None] < jnp.arange(tm)[None, :]).astype(BF16)
    return pl.pallas_call(
        functools.partial(_router_kernel, tm=tm),
        grid=(t // tm,),
        in_specs=[
            pl.BlockSpec((tm, d), lambda i: (i, 0)),
            pl.BlockSpec((1, d), lambda i: (0, 0)),
            pl.BlockSpec((N_EXPERTS, d), lambda i: (0, 0)),
            pl.BlockSpec((N_EXPERTS, d), lambda i: (0, 0)),
            pl.BlockSpec((N_EXPERTS, 1), lambda i: (0, 0)),
            pl.BlockSpec((tm, tm), lambda i: (0, 0)),
        ],
        out_specs=[
            pl.BlockSpec((tm * SLAB, LANES), lambda i: (i, 0)),
            pl.BlockSpec((TOP_K, tm), lambda i: (0, i)),
            pl.BlockSpec((TOP_K, tm), lambda i: (0, i)),
            pl.BlockSpec((TOP_K, tm), lambda i: (0, i)),
            pl.BlockSpec((N_EXPERTS, LANES), lambda i: (0, 0)),
        ],
        out_shape=[
            jax.ShapeDtypeStruct((t * SLAB, LANES), F32),
            jax.ShapeDtypeStruct((TOP_K, t), jnp.int32),
            jax.ShapeDtypeStruct((TOP_K, t), F32),
            jax.ShapeDtypeStruct((TOP_K, t), jnp.int32),
            jax.ShapeDtypeStruct((N_EXPERTS, LANES), F32),
        ],
        scratch_shapes=[pltpu.VMEM((N_EXPERTS, LANES), F32)],
        compiler_params=_cparams("arbitrary"),
        name="moe_router",
    )(h2, norm_g.astype(F32).reshape(1, d), wh, wl, b_router.astype(F32).reshape(N_EXPERTS, 1), su)


def _slab_rows(r):
    return pl.ds(pl.multiple_of(r * SLAB, SLAB), SLAB)


def _dispatch_kernel(dest_ref, pad_ref, slab_ref, xs_ref, zero_ref, sem, *, tm, bm, nb):
    def zero_fill(action):
        def per_expert(e, carry):
            off = pad_ref[e]
            plen = pad_ref[N_EXPERTS + e]
            for bit in PAD_BITS[PAD_BITS.index(bm // 2):]:
                present = (plen & bit) != 0

                @pl.when(present)
                def _(off=off, bit=bit):
                    action(pltpu.make_async_copy(zero_ref.at[pl.ds(0, bit * SLAB), :],
                                                 xs_ref.at[pl.ds(pl.multiple_of(off * SLAB, SLAB), bit * SLAB), :],
                                                 sem))
                off = off + jnp.where(present, bit, 0)
            return carry

        lax.fori_loop(0, N_EXPERTS, per_expert, 0)

        def per_piece(p, carry):
            row = pl.multiple_of(p * PAD_PIECE * SLAB, PAD_PIECE * SLAB)
            action(pltpu.make_async_copy(zero_ref, xs_ref.at[pl.ds(row, PAD_PIECE * SLAB), :], sem))
            return carry

        lax.fori_loop(pad_ref[2 * N_EXPERTS] * (bm // PAD_PIECE), nb * (bm // PAD_PIECE), per_piece, 0)

    @pl.when(pl.program_id(0) == 0)
    def _():
        zero_ref[...] = jnp.zeros_like(zero_ref)
        zero_fill(lambda copy: copy.start())
        zero_fill(lambda copy: copy.wait())

    def row_copy(t, d):
        return pltpu.make_async_copy(slab_ref.at[_slab_rows(t), :], xs_ref.at[_slab_rows(d), :], sem)

    def issue(g, carry):
        for u in range(DMA_ISSUE_UNROLL):
            t = g * DMA_ISSUE_UNROLL + u
            for k in range(TOP_K):
                row_copy(t, dest_ref[0, 0, k * tm + t]).start(priority=k % 2)
        return carry

    lax.fori_loop(0, tm // DMA_ISSUE_UNROLL, issue, 0)

    for k in range(TOP_K):
        pltpu.make_async_copy(slab_ref, xs_ref.at[pl.ds(0, tm * SLAB), :], sem).wait()


def moe_dispatch(dest_tiles, pad_info, slab, p_rows, tm, bm):
    t = slab.shape[0] // SLAB
    assert bm % PAD_PIECE == 0 and bm // 2 in PAD_BITS
    return pl.pallas_call(
        functools.partial(_dispatch_kernel, tm=tm, bm=bm, nb=p_rows // bm),
        grid=(t // tm,),
        in_specs=[
            pl.BlockSpec((1, 1, TOP_K * tm), lambda i: (i, 0, 0), memory_space=pltpu.SMEM),
            pl.BlockSpec(memory_space=pltpu.SMEM),
            pl.BlockSpec((tm * SLAB, LANES), lambda i: (i, 0)),
        ],
        out_specs=pl.BlockSpec(memory_space=pl.ANY),
        out_shape=jax.ShapeDtypeStruct((p_rows * SLAB, LANES), F32),
        scratch_shapes=[pltpu.VMEM((PAD_PIECE * SLAB, LANES), F32), pltpu.SemaphoreType.DMA(())],
        compiler_params=_cparams("arbitrary"),
        name="moe_dispatch",
    )(dest_tiles, pad_info, slab)


def _expert_kernel(blk_e_ref, xs_ref, wgu_ref, bgu_ref, wd_ref, bd_ref, ys_ref, wgu_b_ref, wd_b_ref, *, bm, nb):
    i = pl.program_id(0)
    prev_e = blk_e_ref[jnp.maximum(i - 1, 0)]
    in_use = i < blk_e_ref[nb]

    @pl.when(in_use & ((i == 0) | (blk_e_ref[i] != prev_e)))
    def _():
        wgu_b_ref[...] = wgu_ref[...].astype(BF16)
        wd_b_ref[...] = wd_ref[...].astype(BF16)

    @pl.when(in_use)
    def _():
        x = jnp.concatenate([xs_ref[pl.ds(s, bm, stride=SLAB), :] for s in range(SLAB)], axis=-1)
        gu = _dot(x.astype(BF16), wgu_b_ref[...]) + bgu_ref[...]
        gate = jnp.minimum(gu[:, :D_FF], SWIGLU_LIMIT)
        up = jnp.clip(gu[:, D_FF:], -SWIGLU_LIMIT, SWIGLU_LIMIT)
        act = (up + 1.0) * (gate * jax.nn.sigmoid(gate * SWIGLU_ALPHA))
        y = _dot(act.astype(BF16), wd_b_ref[...]) + bd_ref[...]
        for s in range(SLAB):
            ys_ref[pl.ds(s, bm, stride=SLAB), :] = y[:, s * LANES:(s + 1) * LANES]

    @pl.when(jnp.logical_not(in_use))
    def _():
        ys_ref[...] = jnp.zeros_like(ys_ref)


def moe_experts(blk_e, xs, w_gate_up, b_gate_up, w_down, b_down, layer, bm):
    nb = blk_e.shape[0] - 1
    d = D_MODEL
    grid_spec = pltpu.PrefetchScalarGridSpec(
        num_scalar_prefetch=1,
        grid=(nb,),
        in_specs=[
            pl.BlockSpec((bm * SLAB, LANES), lambda i, be: (i, 0)),
            pl.BlockSpec((None, None, d, 2 * D_FF), lambda i, be: (layer, be[i], 0, 0)),
            pl.BlockSpec((None, 1, 2 * D_FF), lambda i, be: (be[i], 0, 0)),
            pl.BlockSpec((None, None, D_FF, d), lambda i, be: (layer, be[i], 0, 0)),
            pl.BlockSpec((None, 1, d), lambda i, be: (be[i], 0, 0)),
        ],
        out_specs=pl.BlockSpec((bm * SLAB, LANES), lambda i, be: (i, 0)),
        scratch_shapes=[pltpu.VMEM((d, 2 * D_FF), BF16), pltpu.VMEM((D_FF, d), BF16)],
    )
    return pl.pallas_call(
        functools.partial(_expert_kernel, bm=bm, nb=nb),
        grid_spec=grid_spec,
        out_shape=jax.ShapeDtypeStruct((nb * bm * SLAB, LANES), F32),
        compiler_params=_cparams("arbitrary"),
        name="moe_experts",
    )(blk_e, xs, w_gate_up, b_gate_up.astype(F32).reshape(N_EXPERTS, 1, 2 * D_FF),
      w_down, b_down.astype(F32).reshape(N_EXPERTS, 1, d))


def _combine_kernel(dest_ref, dest_next_ref, gate_ref, h_ref, fg_ref, ys_ref, o_ref, buf_a, buf_b, sem,
                    *, tm, final_norm):
    i = pl.program_id(0)
    n = pl.num_programs(0)
    n_rows = TOP_K * tm

    def row_copy(d_ref, j, buf, s):
        return pltpu.make_async_copy(ys_ref.at[_slab_rows(d_ref[0, 0, j]), :],
                                     buf.at[pl.ds(j * SLAB, SLAB), :], s)

    def wait_tile(buf, s):
        pltpu.make_async_copy(ys_ref.at[pl.ds(0, n_rows * SLAB), :], buf, s).wait()

    @pl.when(i == 0)
    def _():
        def issue(g, carry):
            for u in range(DMA_ISSUE_UNROLL):
                j = g * DMA_ISSUE_UNROLL + u
                pltpu.make_async_copy(ys_ref.at[_slab_rows(dest_ref[0, 0, j]), :],
                                      buf_a.at[_slab_rows(j), :], sem.at[0]).start(priority=u % 2)
            return carry
        lax.fori_loop(0, n_rows // DMA_ISSUE_UNROLL, issue, 0)

    def run(cur, cur_sem, nxt, nxt_sem):
        wait_tile(cur, cur_sem)
        gates = gate_ref[...]
        per_block = n_rows // SLAB
        for s in range(SLAB):
            for j in range(s * per_block, (s + 1) * per_block):
                row_copy(dest_next_ref, j, nxt, nxt_sem).start(priority=j % 2)
            cols = slice(s * LANES, (s + 1) * LANES)
            acc = h_ref[:, cols]
            for k in range(TOP_K):
                acc = acc + gates[:, k:k + 1] * cur[pl.ds(k * tm * SLAB + s, tm, stride=SLAB), :]
            o_ref[:, cols] = acc
        if final_norm:
            o_ref[...] = _rms(o_ref[...], fg_ref[...])

        @pl.when(i == n - 1)
        def _():
            wait_tile(nxt, nxt_sem)

    @pl.when(i % 2 == 0)
    def _():
        run(buf_a, sem.at[0], buf_b, sem.at[1])

    @pl.when(i % 2 == 1)
    def _():
        run(buf_b, sem.at[1], buf_a, sem.at[0])


def moe_combine(dest_tiles, gates_col, h2, ys, tm, final_g=None):
    t, d = h2.shape
    nt = t // tm
    fg = jnp.ones((1, d), F32) if final_g is None else final_g.astype(F32).reshape(1, d)
    return pl.pallas_call(
        functools.partial(_combine_kernel, tm=tm, final_norm=final_g is not None),
        grid=(nt,),
        in_specs=[
            pl.BlockSpec((1, 1, TOP_K * tm), lambda i: (i, 0, 0), memory_space=pltpu.SMEM),
            pl.BlockSpec((1, 1, TOP_K * tm), lambda i: (jnp.minimum(i + 1, nt - 1), 0, 0),
                         memory_space=pltpu.SMEM),
            pl.BlockSpec((tm, TOP_K), lambda i: (i, 0)),
            pl.BlockSpec((tm, d), lambda i: (i, 0)),
            pl.BlockSpec((1, d), lambda i: (0, 0)),
            pl.BlockSpec(memory_space=pl.ANY),
        ],
        out_specs=pl.BlockSpec((tm, d), lambda i: (i, 0)),
        out_shape=jax.ShapeDtypeStruct((t, d), F32),
        scratch_shapes=[pltpu.VMEM((TOP_K * tm * SLAB, LANES), F32), pltpu.VMEM((TOP_K * tm * SLAB, LANES), F32),
                        pltpu.SemaphoreType.DMA((2,))],
        compiler_params=_cparams("arbitrary"),
        name="moe_combine",
    )(dest_tiles, dest_tiles, gates_col, h2, fg, ys)


def _tile_major(a, tm):
    t = a.shape[1]
    return a.reshape(TOP_K, t // tm, tm).transpose(1, 0, 2).reshape(t // tm, 1, TOP_K * tm)


def moe_layer(h, norm_g, w_router, b_router, w_gate_up, b_gate_up, w_down, b_down, layer, final_g=None):
    bsz, seq, d = h.shape
    t = bsz * seq
    h2 = h.reshape(t, d)
    bm = MOE_ROWS_PER_BLOCK
    slab, idx, gates, rank, cnt = moe_router(h2, norm_g, w_router, b_router)

    counts = cnt[:, 0].astype(jnp.int32)
    padded = (counts + bm - 1) // bm * bm
    pend = jnp.cumsum(padded)
    pstart = pend - padded
    experts = jnp.arange(N_EXPERTS, dtype=jnp.int32)
    dest = rank + jnp.sum(jnp.where(idx[..., None] == experts, pstart, 0), axis=-1)
    n = t * TOP_K
    p_rows = -(-(n + N_EXPERTS * bm) // bm) * bm
    nb = p_rows // bm
    blk_start = jnp.arange(nb, dtype=jnp.int32) * bm
    blk_e = jnp.minimum(jnp.sum((pend[None, :] <= blk_start[:, None]).astype(jnp.int32), axis=1),
                        N_EXPERTS - 1)

    n_used = pend[N_EXPERTS - 1:] // bm
    pad_info = jnp.concatenate([pstart + counts, padded - counts, n_used]).astype(jnp.int32)
    blk_meta = jnp.concatenate([blk_e, n_used]).astype(jnp.int32)

    td = min(DISPATCH_TILE, t)
    xs = moe_dispatch(_tile_major(dest, td), pad_info, slab, p_rows, td, bm)
    ys = moe_experts(blk_meta, xs, w_gate_up.astype(F32), b_gate_up, w_down.astype(F32), b_down, layer, bm)
    tc = min(COMBINE_TILE, t)
    out = moe_combine(_tile_major(dest, tc), gates.T, h2, ys, tc, final_g)
    return out.reshape(bsz, seq, d)


def kernel(x, mem, mixer_norm, xattn_norm, mem_norm, ffn_norm, ssd_w_in, ssd_conv_w, ssd_conv_b, ssd_dt_bias, ssd_a_log, ssd_d, ssd_norm, ssd_w_out, da_w_qkv, da_lambda_q1, da_lambda_k1, da_lambda_q2, da_lambda_k2, da_subln, da_w_o, xa_w_q, xa_w_kv, xa_w_o, moe_w_router, moe_b_router, moe_w_gate_up, moe_b_gate_up, moe_w_down, moe_b_down, final_norm):
    depth = mixer_norm.shape[0]
    bsz, seq, d = x.shape
    h = x
    for i in range(depth):
        j = i // N_MIXERS
        if i % N_MIXERS == 0:
            h = ssd_layer(h, mixer_norm[i], ssd_w_in[j], ssd_conv_w[j], ssd_conv_b[j], ssd_dt_bias[j],
                          ssd_a_log[j], ssd_d[j], ssd_norm[j], ssd_w_out[j])
        else:
            h = da_layer(h, mixer_norm[i], da_w_qkv[j], da_lambda_q1[j], da_lambda_k1[j], da_lambda_q2[j],
                         da_lambda_k2[j], da_subln[j], da_w_o[j], i)
        h = xattn_layer(h, mem, xattn_norm[i], mem_norm[i], xa_w_q[i], xa_w_kv[i], xa_w_o[i])
        h = moe_layer(h, ffn_norm[i], moe_w_router[i], moe_b_router[i], moe_w_gate_up,
                      moe_b_gate_up[i], moe_w_down, moe_b_down[i], i,
                      final_g=final_norm if i == depth - 1 else None)
    return h
```

```python
import functools
import math

import jax
import jax.numpy as jnp
from jax import lax
from jax.experimental import pallas as pl
from jax.experimental.pallas import tpu as pltpu

F32 = jnp.float32
BF16 = jnp.bfloat16

D_MODEL = 1024
RMS_EPS = 1e-5
LOG2_E = 1.4426950408889634
N_MIXERS = 2

SSD_D_INNER = 2048
SSD_HEADDIM = 64
SSD_N_HEADS = 32
SSD_N_GROUPS = 4
SSD_HEADS_PER_GROUP = 8
SSD_D_STATE = 128
SSD_D_CONV = 4
SSD_CHUNK = 128
SSD_GN = 512
SSD_CONV_DIM = 3072
SSD_GROUP_WIDTH = SSD_D_INNER // SSD_N_GROUPS
SSD_COL_BLOCK = 1024

DA_HEAD_DIM = 64
DA_N_HEADS = 8
DA_KV_UNROLL = 8

XA_N_HEADS = 4
XA_HEAD_DIM = 256

N_EXPERTS = 32
TOP_K = 4
D_FF = 1024
SWIGLU_LIMIT = 7.0
SWIGLU_ALPHA = 1.702

LANES = 128
SUBLANES = 8
VMEM_LIMIT_BYTES = 56 * 1024 * 1024

MOE_ROWS_PER_BLOCK = 512
ROUTER_TILE = 1024
DISPATCH_TILE = 1024
COMBINE_TILE = 256
DMA_ISSUE_UNROLL = 8
PAD_PIECE = 256
PAD_BITS = (256, 128, 64, 32, 16, 8, 4, 2, 1)


def _cparams(*sem):
    return pltpu.CompilerParams(dimension_semantics=sem, vmem_limit_bytes=VMEM_LIMIT_BYTES)


def _rms(x, g):
    ms = jnp.mean(x * x, axis=-1, keepdims=True)
    return x * lax.rsqrt(ms + RMS_EPS) * g


def _dot(a, b):
    return jnp.dot(a, b, preferred_element_type=F32)


def _dot_nt(a, b):
    return lax.dot_general(a, b, (((1,), (1,)), ((), ())), preferred_element_type=F32)


def _split2(x):
    hi = x.astype(BF16)
    lo = (x - hi.astype(F32)).astype(BF16)
    return hi, lo


def _split3(x):
    hi = x.astype(BF16)
    r = x - hi.astype(F32)
    mid = r.astype(BF16)
    lo = (r - mid.astype(F32)).astype(BF16)
    return hi, mid, lo


def _norm_mm_kernel(x_ref, g_ref, w_ref, o_ref, xn_ref):
    @pl.when(pl.program_id(1) == 0)
    def _():
        xn_ref[...] = _rms(x_ref[...], g_ref[...]).astype(BF16)

    o_ref[...] = _dot(xn_ref[...], w_ref[...]).astype(o_ref.dtype)


def norm_matmul(x, g, w, out_dtype, tm, tn):
    m, k = x.shape
    n = w.shape[1]
    tm = min(tm, m)
    tn = min(tn, n)
    return pl.pallas_call(
        _norm_mm_kernel,
        grid=(m // tm, n // tn),
        in_specs=[
            pl.BlockSpec((tm, k), lambda i, j: (i, 0)),
            pl.BlockSpec((1, k), lambda i, j: (0, 0)),
            pl.BlockSpec((k, tn), lambda i, j: (0, j)),
        ],
        out_specs=pl.BlockSpec((tm, tn), lambda i, j: (i, j)),
        out_shape=jax.ShapeDtypeStruct((m, n), out_dtype),
        scratch_shapes=[pltpu.VMEM((tm, k), BF16)],
        compiler_params=_cparams("parallel", "arbitrary"),
        name="norm_matmul",
    )(x, g.reshape(1, k), w)


def _norm_mm_side_kernel(x_ref, g_ref, w_ref, ws_ref, o_ref, os_ref, xn_ref):
    @pl.when(pl.program_id(1) == 0)
    def _():
        xn = _rms(x_ref[...], g_ref[...]).astype(BF16)
        xn_ref[...] = xn
        os_ref[...] = _dot(xn, ws_ref[...])

    o_ref[...] = _dot(xn_ref[...], w_ref[...]).astype(o_ref.dtype)


def norm_matmul_with_side(x, g, w, w_side, out_dtype, tm, tn):
    m, k = x.shape
    n = w.shape[1]
    ns = w_side.shape[1]
    tm = min(tm, m)
    tn = min(tn, n)
    return pl.pallas_call(
        _norm_mm_side_kernel,
        grid=(m // tm, n // tn),
        in_specs=[
            pl.BlockSpec((tm, k), lambda i, j: (i, 0)),
            pl.BlockSpec((1, k), lambda i, j: (0, 0)),
            pl.BlockSpec((k, tn), lambda i, j: (0, j)),
            pl.BlockSpec((k, ns), lambda i, j: (0, 0)),
        ],
        out_specs=[pl.BlockSpec((tm, tn), lambda i, j: (i, j)), pl.BlockSpec((tm, ns), lambda i, j: (i, 0))],
        out_shape=[jax.ShapeDtypeStruct((m, n), out_dtype), jax.ShapeDtypeStruct((m, ns), F32)],
        scratch_shapes=[pltpu.VMEM((tm, k), BF16)],
        compiler_params=_cparams("parallel", "arbitrary"),
        name="norm_matmul_side",
    )(x, g.reshape(1, k), w, w_side)


def _mm_res_kernel(x_ref, w_ref, r_ref, o_ref):
    o_ref[...] = r_ref[...] + _dot(x_ref[...], w_ref[...])


def matmul_residual(x, w, res, tm=1024):
    m, k = x.shape
    n = w.shape[1]
    tm = min(tm, m)
    return pl.pallas_call(
        _mm_res_kernel,
        grid=(m // tm,),
        in_specs=[
            pl.BlockSpec((tm, k), lambda i: (i, 0)),
            pl.BlockSpec((k, n), lambda i: (0, 0)),
            pl.BlockSpec((tm, n), lambda i: (i, 0)),
        ],
        out_specs=pl.BlockSpec((tm, n), lambda i: (i, 0)),
        out_shape=jax.ShapeDtypeStruct((m, n), F32),
        compiler_params=_cparams("parallel"),
        name="matmul_residual",
    )(x, w, res)


def _ssd_kernel(z0_ref, z1_ref, x0_ref, x1_ref, bc_ref, dtr_ref, h_ref,
                convw_ref, convb_ref, dtb_ref, alog_ref, dexp_ref, ng_ref, expand_ref, wout_ref,
                o_ref, state_ref, ext_ref):
    L = SSD_CHUNK
    GW = SSD_GROUP_WIDTH
    c = pl.program_id(1)

    @pl.when(c == 0)
    def _():
        state_ref[...] = jnp.zeros_like(state_ref)
        ext_ref[0:L, :] = jnp.zeros((L, SSD_CONV_DIM), BF16)

    srow = lax.broadcasted_iota(jnp.int32, (L, 2 * L), 0)
    scol = lax.broadcasted_iota(jnp.int32, (L, 2 * L), 1)
    shifts = [jnp.where(scol == srow + (L - (SSD_D_CONV - 1) + k), 1.0, 0.0).astype(BF16)
              for k in range(SSD_D_CONV - 1)]
    pieces = []
    for blk, ref in enumerate((x0_ref, x1_ref, bc_ref)):
        cols = slice(blk * SSD_COL_BLOCK, (blk + 1) * SSD_COL_BLOCK)
        cur = ref[...]
        ext_ref[L:2 * L, cols] = cur
        both = ext_ref[:, cols]
        acc = convb_ref[:, cols] + convw_ref[SSD_D_CONV - 1:SSD_D_CONV, cols] * cur.astype(F32)
        for k in range(SSD_D_CONV - 1):
            acc = acc + convw_ref[k:k + 1, cols] * _dot(shifts[k], both)
        pieces.append(acc * jax.nn.sigmoid(acc))
        ext_ref[0:L, cols] = cur
    xs = jnp.concatenate(pieces[:2], axis=-1)
    b_all = pieces[2][:, :SSD_GN]
    c_all = pieces[2][:, SSD_GN:]

    dtr = dtr_ref[...] + dtb_ref[...]
    dt = jnp.maximum(dtr, 0.0) + jnp.log1p(jnp.exp(-jnp.abs(dtr)))
    a = -jnp.exp(alog_ref[...])
    da = dt * a
    row = lax.broadcasted_iota(jnp.int32, (L, L), 0)
    col = lax.broadcasted_iota(jnp.int32, (L, L), 1)
    causal = col <= row
    tril = jnp.where(causal, 1.0, 0.0).astype(BF16)
    d_hi, d_mid, d_lo = _split3(da)
    a_cum = _dot(tril, d_hi) + _dot(tril, d_mid) + _dot(tril, d_lo)
    a_cum_t = a_cum.T

    expand = expand_ref[...]
    dt_e = _dot(dt.astype(BF16), expand)
    w_e = _dot((dt * jnp.exp(a_cum[L - 1:L, :] - a_cum)).astype(BF16), expand)
    e_hi, e_lo = _split2(jnp.exp(a_cum))
    exp_acum_e = _dot(e_hi, expand) + _dot(e_lo, expand)
    cd_e = exp_acum_e[L - 1:L, :]

    xd_b = (xs * dt_e).astype(BF16)
    xdw_b = (xs * w_e).astype(BF16)

    lane = lax.broadcasted_iota(jnp.int32, (L, LANES), 1)
    first_half = lane < SSD_HEADDIM

    y_parts = []
    for g in range(SSD_N_GROUPS):
        gs = slice(g * SSD_D_STATE, (g + 1) * SSD_D_STATE)
        gw = slice(g * GW, (g + 1) * GW)
        b_g = b_all[:, gs]
        c_g = c_all[:, gs].astype(BF16)
        cb = _dot_nt(c_g, b_g.astype(BF16))
        y_pairs = []
        for jp in range(SSD_HEADS_PER_GROUP // 2):
            res = []
            pair_col = g * GW + jp * LANES
            xd_pair = xd_b[:, pair_col:pair_col + LANES]
            for sub in range(2):
                hd = g * SSD_HEADS_PER_GROUP + jp * 2 + sub
                diff = a_cum[:, hd:hd + 1] - a_cum_t[hd:hd + 1, :]
                dec = jnp.exp(jnp.where(causal, diff, -jnp.inf))
                res.append(_dot((cb * dec).astype(BF16), xd_pair))
            y_pairs.append(jnp.where(first_half, res[0], res[1]))
        y_diag = jnp.concatenate(y_pairs, axis=-1)
        st = state_ref[g]
        y_off = _dot(c_g, st.astype(BF16)) * exp_acum_e[:, gw]
        state_ref[g] = st * cd_e[:, gw] + _dot(b_g.T.astype(BF16), xdw_b[:, gw])
        y_parts.append(y_diag + y_off)
    y = jnp.concatenate(y_parts, axis=-1) + dexp_ref[...] * xs

    z = jnp.concatenate([z0_ref[...], z1_ref[...]], axis=-1).astype(F32)
    u = y * (z * jax.nn.sigmoid(z))
    u_parts = []
    for g in range(SSD_N_GROUPS):
        gw = slice(g * GW, (g + 1) * GW)
        ug = u[:, gw]
        ms = jnp.mean(ug * ug, axis=-1, keepdims=True)
        u_parts.append(ug * lax.rsqrt(ms + RMS_EPS) * ng_ref[:, gw])
    un = jnp.concatenate(u_parts, axis=-1).astype(BF16)
    o_ref[...] = h_ref[...] + _dot(un, wout_ref[...])


def ssd_core(zxbc, dt_raw, h, conv_w, conv_b, dt_bias, a_log, d_skip, norm_g, w_out_b):
    bsz, seq, _ = h.shape
    L = SSD_CHUNK
    nc = seq // L
    pad_heads = LANES - SSD_N_HEADS
    dtb = jnp.pad(dt_bias.astype(F32), (0, pad_heads)).reshape(1, LANES)
    alog = jnp.pad(a_log.astype(F32), (0, pad_heads)).reshape(1, LANES)
    dexp = jnp.repeat(d_skip.astype(F32), SSD_HEADDIM).reshape(1, SSD_D_INNER)
    head_of_col = jnp.arange(SSD_D_INNER, dtype=jnp.int32) // SSD_HEADDIM
    expand = (jnp.arange(LANES, dtype=jnp.int32)[:, None] == head_of_col[None, :]).astype(BF16)

    def zx_spec(k):
        return pl.BlockSpec((None, L, SSD_COL_BLOCK), lambda b, c, k=k: (b, c, k))

    def const_spec(shape):
        return pl.BlockSpec(shape, lambda b, c: (0,) * len(shape))

    return pl.pallas_call(
        _ssd_kernel,
        grid=(bsz, nc),
        in_specs=[
            zx_spec(0), zx_spec(1), zx_spec(2), zx_spec(3), zx_spec(4),
            pl.BlockSpec((None, L, LANES), lambda b, c: (b, c, 0)),
            pl.BlockSpec((None, L, D_MODEL), lambda b, c: (b, c, 0)),
            const_spec((SSD_D_CONV, SSD_CONV_DIM)),
            const_spec((1, SSD_CONV_DIM)),
            const_spec((1, LANES)),
            const_spec((1, LANES)),
            const_spec((1, SSD_D_INNER)),
            const_spec((1, SSD_D_INNER)),
            const_spec((LANES, SSD_D_INNER)),
            const_spec((SSD_D_INNER, D_MODEL)),
        ],
        out_specs=pl.BlockSpec((None, L, D_MODEL), lambda b, c: (b, c, 0)),
        out_shape=jax.ShapeDtypeStruct((bsz, seq, D_MODEL), F32),
        scratch_shapes=[
            pltpu.VMEM((SSD_N_GROUPS, SSD_D_STATE, SSD_GROUP_WIDTH), F32),
            pltpu.VMEM((2 * L, SSD_CONV_DIM), BF16),
        ],
        compiler_params=_cparams("parallel", "arbitrary"),
        name="ssd_core",
    )(zxbc, zxbc, zxbc, zxbc, zxbc, dt_raw, h,
      conv_w.astype(F32), conv_b.astype(F32).reshape(1, SSD_CONV_DIM), dtb, alog, dexp,
      norm_g.astype(F32).reshape(1, SSD_D_INNER), expand, w_out_b)


def ssd_layer(h, norm_g_in, w_in, conv_w, conv_b, dt_bias, a_log, d_skip, norm_g, w_out):
    bsz, seq, d = h.shape
    h2 = h.reshape(bsz * seq, d)
    n_main = SSD_D_INNER + SSD_CONV_DIM
    w_main = w_in[:, :n_main].astype(BF16)
    w_dt = jnp.pad(w_in[:, n_main:], ((0, 0), (0, LANES - SSD_N_HEADS))).astype(BF16)
    zxbc, dt_raw = norm_matmul_with_side(h2, norm_g_in, w_main, w_dt, BF16, tm=2048, tn=1024)
    return ssd_core(zxbc.reshape(bsz, seq, n_main), dt_raw.reshape(bsz, seq, LANES), h,
                    conv_w, conv_b, dt_bias, a_log, d_skip, norm_g, w_out.astype(BF16))


def _da_kernel(lq1_ref, lk1_ref, lq2_ref, lk2_ref, sub_ref, q_ref, k_ref, v_ref, o_ref,
               acc1_ref, acc2_ref, m1_ref, m2_ref, *, tq, lambda_init):
    i = pl.program_id(2)
    q = q_ref[...]
    lane = lax.broadcasted_iota(jnp.int32, (tq, LANES), 1)
    qs = (q.astype(F32) * (DA_HEAD_DIM ** -0.5 * LOG2_E)).astype(BF16)
    zero = jnp.zeros_like(qs)
    q_maps = (jnp.where(lane < DA_HEAD_DIM, qs, zero), jnp.where(lane >= DA_HEAD_DIM, qs, zero))
    states = ((m1_ref, acc1_ref), (m2_ref, acc2_ref))

    for m_ref, acc_ref in states:
        m_ref[...] = jnp.full((tq, LANES), -jnp.inf, F32)
        acc_ref[...] = jnp.zeros((tq, 2 * LANES), F32)

    ones = jnp.ones((tq, LANES), BF16)

    def step(j, masked):
        start = pl.multiple_of(j * tq, tq)
        kt = k_ref[pl.ds(start, tq), :]
        v_aug = jnp.concatenate([v_ref[pl.ds(start, tq), :], ones], axis=1)
        for qm, (m_ref, acc_ref) in zip(q_maps, states):
            s = _dot_nt(qm, kt)
            if masked:
                r = lax.broadcasted_iota(jnp.int32, (tq, tq), 0)
                cidx = lax.broadcasted_iota(jnp.int32, (tq, tq), 1)
                s = jnp.where(cidx <= r, s, -jnp.inf)
            m_old = m_ref[...]
            m_new = jnp.maximum(m_old, jnp.max(s, axis=-1, keepdims=True))
            alpha = jnp.exp2(m_old - m_new)
            p = jnp.exp2((s - jnp.concatenate([m_new] * (tq // LANES), axis=1)).astype(BF16))
            acc_ref[...] = (jnp.concatenate([alpha, alpha], axis=1) * acc_ref[...]
                            + _dot(p, v_aug))
            m_ref[...] = m_new

    def body(jj, carry):
        for u in range(DA_KV_UNROLL):
            step(DA_KV_UNROLL * jj + u, False)
        return carry

    lax.fori_loop(0, i // DA_KV_UNROLL, body, 0)

    for rem in range(DA_KV_UNROLL):
        @pl.when(i % DA_KV_UNROLL == rem)
        def _(rem=rem):
            for u in range(rem):
                step(i - rem + u, False)
            step(i, True)

    lam = (jnp.exp(jnp.sum(lq1_ref[...] * lk1_ref[...], axis=-1, keepdims=True))
           - jnp.exp(jnp.sum(lq2_ref[...] * lk2_ref[...], axis=-1, keepdims=True)) + lambda_init)
    o1 = acc1_ref[:, :LANES] / acc1_ref[:, LANES:]
    o2 = acc2_ref[:, :LANES] / acc2_ref[:, LANES:]
    o = _rms(o1 - lam * o2, sub_ref[...]) * (1.0 - lambda_init)
    o_ref[...] = o.astype(o_ref.dtype)


def diff_attention_core(qkv, lq1, lk1, lq2, lk2, subln_g, layer_idx, tq=512):
    bsz, seq, _ = qkv.shape
    tq = min(tq, seq)
    lambda_init = 0.8 - 0.6 * math.exp(-0.3 * layer_idx)
    nh = DA_N_HEADS

    def vec_spec(n):
        return pl.BlockSpec((1, n), lambda b, h, i: (0, 0))

    return pl.pallas_call(
        functools.partial(_da_kernel, tq=tq, lambda_init=lambda_init),
        grid=(bsz, nh, seq // tq),
        in_specs=[
            vec_spec(DA_HEAD_DIM), vec_spec(DA_HEAD_DIM), vec_spec(DA_HEAD_DIM), vec_spec(DA_HEAD_DIM),
            vec_spec(LANES),
            pl.BlockSpec((None, tq, LANES), lambda b, h, i: (b, i, h)),
            pl.BlockSpec((None, seq, LANES), lambda b, h, i: (b, 0, nh + h)),
            pl.BlockSpec((None, seq, LANES), lambda b, h, i: (b, 0, 2 * nh + h)),
        ],
        out_specs=pl.BlockSpec((None, tq, LANES), lambda b, h, i: (b, i, h)),
        out_shape=jax.ShapeDtypeStruct((bsz, seq, D_MODEL), BF16),
        scratch_shapes=[
            pltpu.VMEM((tq, 2 * LANES), F32), pltpu.VMEM((tq, 2 * LANES), F32),
            pltpu.VMEM((tq, LANES), F32), pltpu.VMEM((tq, LANES), F32),
        ],
        compiler_params=_cparams("parallel", "parallel", "arbitrary"),
        name="diff_attention",
    )(lq1.astype(F32).reshape(1, -1), lk1.astype(F32).reshape(1, -1),
      lq2.astype(F32).reshape(1, -1), lk2.astype(F32).reshape(1, -1),
      subln_g.astype(F32).reshape(1, -1), qkv, qkv, qkv)


def da_layer(h, norm_g_in, w_qkv, lq1, lk1, lq2, lk2, subln_g, w_o, layer_idx):
    bsz, seq, d = h.shape
    h2 = h.reshape(bsz * seq, d)
    qkv = norm_matmul(h2, norm_g_in, w_qkv.astype(BF16), BF16, tm=2048, tn=1024)
    o = diff_attention_core(qkv.reshape(bsz, seq, 3 * d), lq1, lk1, lq2, lk2, subln_g, layer_idx)
    return matmul_residual(o.reshape(bsz * seq, d), w_o.astype(BF16), h2).reshape(bsz, seq, d)


def _xattn_kernel(h_ref, g_ref, wq_ref, kv_ref, wo_ref, o_ref):
    h = h_ref[...]
    hn = _rms(h, g_ref[...]).astype(BF16)
    scale = XA_HEAD_DIM ** -0.5
    q = (_dot(hn, wq_ref[...]) * scale).astype(BF16)
    outs = []
    for hd in range(XA_N_HEADS):
        cs = slice(hd * XA_HEAD_DIM, (hd + 1) * XA_HEAD_DIM)
        vs = slice(D_MODEL + hd * XA_HEAD_DIM, D_MODEL + (hd + 1) * XA_HEAD_DIM)
        s = _dot_nt(q[:, cs], kv_ref[:, cs])
        m = jnp.max(s, axis=-1, keepdims=True)
        p = jnp.exp(s - m)
        l = jnp.sum(p, axis=-1, keepdims=True)
        outs.append((_dot(p.astype(BF16), kv_ref[:, vs]) / l).astype(BF16))
    o = jnp.concatenate(outs, axis=-1)
    o_ref[...] = h + _dot(o, wo_ref[...])


def xattn_layer(h, mem, norm_g, mem_norm_g, w_q, w_kv, w_o, tq=1024):
    bsz, seq, d = h.shape
    mlen = mem.shape[1]
    tq = min(tq, seq)
    kv = norm_matmul(mem.reshape(bsz * mlen, d), mem_norm_g, w_kv.astype(BF16), BF16, tm=512, tn=1024)
    kv = kv.reshape(bsz, mlen, 2 * d)
    return pl.pallas_call(
        _xattn_kernel,
        grid=(bsz, seq // tq),
        in_specs=[
            pl.BlockSpec((None, tq, d), lambda b, i: (b, i, 0)),
            pl.BlockSpec((1, d), lambda b, i: (0, 0)),
            pl.BlockSpec((d, d), lambda b, i: (0, 0)),
            pl.BlockSpec((None, mlen, 2 * d), lambda b, i: (b, 0, 0)),
            pl.BlockSpec((d, d), lambda b, i: (0, 0)),
        ],
        out_specs=pl.BlockSpec((None, tq, d), lambda b, i: (b, i, 0)),
        out_shape=jax.ShapeDtypeStruct((bsz, seq, d), F32),
        compiler_params=_cparams("parallel", "parallel"),
        name="mem_xattn",
    )(h, norm_g.astype(F32).reshape(1, d), w_q.astype(BF16), kv, w_o.astype(BF16))


SLAB = D_MODEL // LANES


def _router_kernel(h_ref, g_ref, wh_ref, wl_ref, br_ref, su_ref,
                   slab_ref, idx_ref, gate_ref, rank_ref, cnt_ref, run_ref, *, tm):
    i = pl.program_id(0)

    @pl.when(i == 0)
    def _():
        run_ref[...] = jnp.zeros_like(run_ref)

    hn = _rms(h_ref[...], g_ref[...])
    for s in range(SLAB):
        slab_ref[pl.ds(s, tm, stride=SLAB), :] = hn[:, s * LANES:(s + 1) * LANES]

    x_hi, x_lo = _split2(hn)
    wh = wh_ref[...]
    logits = _dot_nt(wh, x_hi) + _dot_nt(wh, x_lo) + _dot_nt(wl_ref[...], x_hi) + br_ref[...]

    rows = lax.broadcasted_iota(jnp.int32, (N_EXPERTS, tm), 0).astype(F32)
    tops, idxs, onehots = [], [], []
    cur = logits
    for _ in range(TOP_K):
        m = jnp.max(cur, axis=0, keepdims=True)
        idx = jnp.min(jnp.where(cur == m, rows, float(N_EXPERTS)), axis=0, keepdims=True)
        oh = rows == idx
        cur = jnp.where(oh, -jnp.inf, cur)
        tops.append(m)
        idxs.append(idx)
        onehots.append(oh)
    exps = [jnp.exp(t - tops[0]) for t in tops]
    denom = exps[0] + exps[1] + exps[2] + exps[3]
    gate_ref[...] = jnp.concatenate([e / denom for e in exps], axis=0)
    idx_ref[...] = jnp.concatenate(idxs, axis=0).astype(jnp.int32)

    oh_sum = jnp.zeros((N_EXPERTS, tm), F32)
    for oh in onehots:
        oh_sum = oh_sum + jnp.where(oh, 1.0, 0.0)
    run = run_ref[...]
    prefix = _dot(oh_sum.astype(BF16), su_ref[...]) + run[:, 0:1]
    ranks = [jnp.sum(jnp.where(oh, prefix, 0.0), axis=0, keepdims=True) for oh in onehots]
    rank_ref[...] = jnp.concatenate(ranks, axis=0).astype(jnp.int32)
    run_new = run + jnp.sum(oh_sum, axis=1, keepdims=True)
    run_ref[...] = run_new
    cnt_ref[...] = run_new


def moe_router(h2, norm_g, w_router, b_router, tm=ROUTER_TILE):
    t, d = h2.shape
    tm = min(tm, t)
    wt = w_router.astype(F32).T
    wh = wt.astype(BF16)
    wl = (wt - wh.astype(F32)).astype(BF16)
    su = (jnp.arange(tm)[:, None] < jnp.arange(tm)[None, :]).astype(BF16)
    return pl.pallas_call(
        functools.partial(_router_kernel, tm=tm),
        grid=(t // tm,),
        in_specs=[
            pl.BlockSpec((tm, d), lambda i: (i, 0)),
            pl.BlockSpec((1, d), lambda i: (0, 0)),
            pl.BlockSpec((N_EXPERTS, d), lambda i: (0, 0)),
            pl.BlockSpec((N_EXPERTS, d), lambda i: (0, 0)),
            pl.BlockSpec((N_EXPERTS, 1), lambda i: (0, 0)),
            pl.BlockSpec((tm, tm), lambda i: (0, 0)),
        ],
        out_specs=[
            pl.BlockSpec((tm * SLAB, LANES), lambda i: (i, 0)),
            pl.BlockSpec((TOP_K, tm), lambda i: (0, i)),
            pl.BlockSpec((TOP_K, tm), lambda i: (0, i)),
            pl.BlockSpec((TOP_K, tm), lambda i: (0, i)),
            pl.BlockSpec((N_EXPERTS, LANES), lambda i: (0, 0)),
        ],
        out_shape=[
            jax.ShapeDtypeStruct((t * SLAB, LANES), F32),
            jax.ShapeDtypeStruct((TOP_K, t), jnp.int32),
            jax.ShapeDtypeStruct((TOP_K, t), F32),
            jax.ShapeDtypeStruct((TOP_K, t), jnp.int32),
            jax.ShapeDtypeStruct((N_EXPERTS, LANES), F32),
        ],
        scratch_shapes=[pltpu.VMEM((N_EXPERTS, LANES), F32)],
        compiler_params=_cparams("arbitrary"),
        name="moe_router",
    )(h2, norm_g.astype(F32).reshape(1, d), wh, wl, b_router.astype(F32).reshape(N_EXPERTS, 1), su)


def _slab_rows(r):
    return pl.ds(pl.multiple_of(r * SLAB, SLAB), SLAB)


def _dispatch_kernel(dest_ref, pad_ref, slab_ref, xs_ref, zero_ref, sem, *, tm, bm, nb):
    def zero_fill(action):
        def per_expert(e, carry):
            off = pad_ref[e]
            plen = pad_ref[N_EXPERTS + e]
            for bit in PAD_BITS[PAD_BITS.index(bm // 2):]:
                present = (plen & bit) != 0

                @pl.when(present)
                def _(off=off, bit=bit):
                    action(pltpu.make_async_copy(zero_ref.at[pl.ds(0, bit * SLAB), :],
                                                 xs_ref.at[pl.ds(pl.multiple_of(off * SLAB, SLAB), bit * SLAB), :],
                                                 sem))
                off = off + jnp.where(present, bit, 0)
            return carry

        lax.fori_loop(0, N_EXPERTS, per_expert, 0)

        def per_piece(p, carry):
            row = pl.multiple_of(p * PAD_PIECE * SLAB, PAD_PIECE * SLAB)
            action(pltpu.make_async_copy(zero_ref, xs_ref.at[pl.ds(row, PAD_PIECE * SLAB), :], sem))
            return carry

        lax.fori_loop(pad_ref[2 * N_EXPERTS] * (bm // PAD_PIECE), nb * (bm // PAD_PIECE), per_piece, 0)

    @pl.when(pl.program_id(0) == 0)
    def _():
        zero_ref[...] = jnp.zeros_like(zero_ref)
        zero_fill(lambda copy: copy.start())
        zero_fill(lambda copy: copy.wait())

    def row_copy(t, d):
        return pltpu.make_async_copy(slab_ref.at[_slab_rows(t), :], xs_ref.at[_slab_rows(d), :], sem)

    def issue(g, carry):
        for u in range(DMA_ISSUE_UNROLL):
            t = g * DMA_ISSUE_UNROLL + u
            for k in range(TOP_K):
                row_copy(t, dest_ref[0, 0, k * tm + t]).start(priority=k % 2)
        return carry

    lax.fori_loop(0, tm // DMA_ISSUE_UNROLL, issue, 0)

    for k in range(TOP_K):
        pltpu.make_async_copy(slab_ref, xs_ref.at[pl.ds(0, tm * SLAB), :], sem).wait()


def moe_dispatch(dest_tiles, pad_info, slab, p_rows, tm, bm):
    t = slab.shape[0] // SLAB
    assert bm % PAD_PIECE == 0 and bm // 2 in PAD_BITS
    return pl.pallas_call(
        functools.partial(_dispatch_kernel, tm=tm, bm=bm, nb=p_rows // bm),
        grid=(t // tm,),
        in_specs=[
            pl.BlockSpec((1, 1, TOP_K * tm), lambda i: (i, 0, 0), memory_space=pltpu.SMEM),
            pl.BlockSpec(memory_space=pltpu.SMEM),
            pl.BlockSpec((tm * SLAB, LANES), lambda i: (i, 0)),
        ],
        out_specs=pl.BlockSpec(memory_space=pl.ANY),
        out_shape=jax.ShapeDtypeStruct((p_rows * SLAB, LANES), F32),
        scratch_shapes=[pltpu.VMEM((PAD_PIECE * SLAB, LANES), F32), pltpu.SemaphoreType.DMA(())],
        compiler_params=_cparams("arbitrary"),
        name="moe_dispatch",
    )(dest_tiles, pad_info, slab)


def _expert_kernel(blk_e_ref, xs_ref, wgu_ref, bgu_ref, wd_ref, bd_ref, ys_ref, wgu_b_ref, wd_b_ref, *, bm, nb):
    i = pl.program_id(0)
    prev_e = blk_e_ref[jnp.maximum(i - 1, 0)]
    in_use = i < blk_e_ref[nb]

    @pl.when(in_use & ((i == 0) | (blk_e_ref[i] != prev_e)))
    def _():
        wgu_b_ref[...] = wgu_ref[...].astype(BF16)
        wd_b_ref[...] = wd_ref[...].astype(BF16)

    @pl.when(in_use)
    def _():
        x = jnp.concatenate([xs_ref[pl.ds(s, bm, stride=SLAB), :] for s in range(SLAB)], axis=-1)
        gu = _dot(x.astype(BF16), wgu_b_ref[...]) + bgu_ref[...]
        gate = jnp.minimum(gu[:, :D_FF], SWIGLU_LIMIT)
        up = jnp.clip(gu[:, D_FF:], -SWIGLU_LIMIT, SWIGLU_LIMIT)
        act = (up + 1.0) * (gate * jax.nn.sigmoid(gate * SWIGLU_ALPHA))
        y = _dot(act.astype(BF16), wd_b_ref[...]) + bd_ref[...]
        for s in range(SLAB):
            ys_ref[pl.ds(s, bm, stride=SLAB), :] = y[:, s * LANES:(s + 1) * LANES]

    @pl.when(jnp.logical_not(in_use))
    def _():
        ys_ref[...] = jnp.zeros_like(ys_ref)


def moe_experts(blk_e, xs, w_gate_up, b_gate_up, w_down, b_down, layer, bm):
    nb = blk_e.shape[0] - 1
    d = D_MODEL
    grid_spec = pltpu.PrefetchScalarGridSpec(
        num_scalar_prefetch=1,
        grid=(nb,),
        in_specs=[
            pl.BlockSpec((bm * SLAB, LANES), lambda i, be: (i, 0)),
            pl.BlockSpec((None, None, d, 2 * D_FF), lambda i, be: (layer, be[i], 0, 0)),
            pl.BlockSpec((None, 1, 2 * D_FF), lambda i, be: (be[i], 0, 0)),
            pl.BlockSpec((None, None, D_FF, d), lambda i, be: (layer, be[i], 0, 0)),
            pl.BlockSpec((None, 1, d), lambda i, be: (be[i], 0, 0)),
        ],
        out_specs=pl.BlockSpec((bm * SLAB, LANES), lambda i, be: (i, 0)),
        scratch_shapes=[pltpu.VMEM((d, 2 * D_FF), BF16), pltpu.VMEM((D_FF, d), BF16)],
    )
    return pl.pallas_call(
        functools.partial(_expert_kernel, bm=bm, nb=nb),
        grid_spec=grid_spec,
        out_shape=jax.ShapeDtypeStruct((nb * bm * SLAB, LANES), F32),
        compiler_params=_cparams("arbitrary"),
        name="moe_experts",
    )(blk_e, xs, w_gate_up, b_gate_up.astype(F32).reshape(N_EXPERTS, 1, 2 * D_FF),
      w_down, b_down.astype(F32).reshape(N_EXPERTS, 1, d))


def _combine_kernel(dest_ref, dest_next_ref, gate_ref, h_ref, fg_ref, ys_ref, o_ref, buf_a, buf_b, sem,
                    *, tm, final_norm):
    i = pl.program_id(0)
    n = pl.num_programs(0)
    n_rows = TOP_K * tm

    def row_copy(d_ref, j, buf, s):
        return pltpu.make_async_copy(ys_ref.at[_slab_rows(d_ref[0, 0, j]), :],
                                     buf.at[pl.ds(j * SLAB, SLAB), :], s)

    def wait_tile(buf, s):
        pltpu.make_async_copy(ys_ref.at[pl.ds(0, n_rows * SLAB), :], buf, s).wait()

    @pl.when(i == 0)
    def _():
        def issue(g, carry):
            for u in range(DMA_ISSUE_UNROLL):
                j = g * DMA_ISSUE_UNROLL + u
                pltpu.make_async_copy(ys_ref.at[_slab_rows(dest_ref[0, 0, j]), :],
                                      buf_a.at[_slab_rows(j), :], sem.at[0]).start(priority=u % 2)
            return carry
        lax.fori_loop(0, n_rows // DMA_ISSUE_UNROLL, issue, 0)

    def run(cur, cur_sem, nxt, nxt_sem):
        wait_tile(cur, cur_sem)
        gates = gate_ref[...]
        per_block = n_rows // SLAB
        for s in range(SLAB):
            for j in range(s * per_block, (s + 1) * per_block):
                row_copy(dest_next_ref, j, nxt, nxt_sem).start(priority=j % 2)
            cols = slice(s * LANES, (s + 1) * LANES)
            acc = h_ref[:, cols]
            for k in range(TOP_K):
                acc = acc + gates[:, k:k + 1] * cur[pl.ds(k * tm * SLAB + s, tm, stride=SLAB), :]
            o_ref[:, cols] = acc
        if final_norm:
            o_ref[...] = _rms(o_ref[...], fg_ref[...])

        @pl.when(i == n - 1)
        def _():
            wait_tile(nxt, nxt_sem)

    @pl.when(i % 2 == 0)
    def _():
        run(buf_a, sem.at[0], buf_b, sem.at[1])

    @pl.when(i % 2 == 1)
    def _():
        run(buf_b, sem.at[1], buf_a, sem.at[0])


def moe_combine(dest_tiles, gates_col, h2, ys, tm, final_g=None):
    t, d = h2.shape
    nt = t // tm
    fg = jnp.ones((1, d), F32) if final_g is None else final_g.astype(F32).reshape(1, d)
    return pl.pallas_call(
        functools.partial(_combine_kernel, tm=tm, final_norm=final_g is not None),
        grid=(nt,),
        in_specs=[
            pl.BlockSpec((1, 1, TOP_K * tm), lambda i: (i, 0, 0), memory_space=pltpu.SMEM),
            pl.BlockSpec((1, 1, TOP_K * tm), lambda i: (jnp.minimum(i + 1, nt - 1), 0, 0),
                         memory_space=pltpu.SMEM),
            pl.BlockSpec((tm, TOP_K), lambda i: (i, 0)),
            pl.BlockSpec((tm, d), lambda i: (i, 0)),
            pl.BlockSpec((1, d), lambda i: (0, 0)),
            pl.BlockSpec(memory_space=pl.ANY),
        ],
        out_specs=pl.BlockSpec((tm, d), lambda i: (i, 0)),
        out_shape=jax.ShapeDtypeStruct((t, d), F32),
        scratch_shapes=[pltpu.VMEM((TOP_K * tm * SLAB, LANES), F32), pltpu.VMEM((TOP_K * tm * SLAB, LANES), F32),
                        pltpu.SemaphoreType.DMA((2,))],
        compiler_params=_cparams("arbitrary"),
        name="moe_combine",
    )(dest_tiles, dest_tiles, gates_col, h2, fg, ys)


def _tile_major(a, tm):
    t = a.shape[1]
    return a.reshape(TOP_K, t // tm, tm).transpose(1, 0, 2).reshape(t // tm, 1, TOP_K * tm)


def moe_layer(h, norm_g, w_router, b_router, w_gate_up, b_gate_up, w_down, b_down, layer, final_g=None):
    bsz, seq, d = h.shape
    t = bsz * seq
    h2 = h.reshape(t, d)
    bm = MOE_ROWS_PER_BLOCK
    slab, idx, gates, rank, cnt = moe_router(h2, norm_g, w_router, b_router)

    counts = cnt[:, 0].astype(jnp.int32)
    padded = (counts + bm - 1) // bm * bm
    pend = jnp.cumsum(padded)
    pstart = pend - padded
    experts = jnp.arange(N_EXPERTS, dtype=jnp.int32)
    dest = rank + jnp.sum(jnp.where(idx[..., None] == experts, pstart, 0), axis=-1)
    n = t * TOP_K
    p_rows = -(-(n + N_EXPERTS * bm) // bm) * bm
    nb = p_rows // bm
    blk_start = jnp.arange(nb, dtype=jnp.int32) * bm
    blk_e = jnp.minimum(jnp.sum((pend[None, :] <= blk_start[:, None]).astype(jnp.int32), axis=1),
                        N_EXPERTS - 1)

    n_used = pend[N_EXPERTS - 1:] // bm
    pad_info = jnp.concatenate([pstart + counts, padded - counts, n_used]).astype(jnp.int32)
    blk_meta = jnp.concatenate([blk_e, n_used]).astype(jnp.int32)

    td = min(DISPATCH_TILE, t)
    xs = moe_dispatch(_tile_major(dest, td), pad_info, slab, p_rows, td, bm)
    ys = moe_experts(blk_meta, xs, w_gate_up.astype(F32), b_gate_up, w_down.astype(F32), b_down, layer, bm)
    tc = min(COMBINE_TILE, t)
    out = moe_combine(_tile_major(dest, tc), gates.T, h2, ys, tc, final_g)
    return out.reshape(bsz, seq, d)


def kernel(x, mem, mixer_norm, xattn_norm, mem_norm, ffn_norm, ssd_w_in, ssd_conv_w, ssd_conv_b, ssd_dt_bias, ssd_a_log, ssd_d, ssd_norm, ssd_w_out, da_w_qkv, da_lambda_q1, da_lambda_k1, da_lambda_q2, da_lambda_k2, da_subln, da_w_o, xa_w_q, xa_w_kv, xa_w_o, moe_w_router, moe_b_router, moe_w_gate_up, moe_b_gate_up, moe_w_down, moe_b_down, final_norm):
    depth = mixer_norm.shape[0]
    bsz, seq, d = x.shape
    h = x
    for i in range(depth):
        j = i // N_MIXERS
        if i % N_MIXERS == 0:
            h = ssd_layer(h, mixer_norm[i], ssd_w_in[j], ssd_conv_w[j], ssd_conv_b[j], ssd_dt_bias[j],
                          ssd_a_log[j], ssd_d[j], ssd_norm[j], ssd_w_out[j])
        else:
            h = da_layer(h, mixer_norm[i], da_w_qkv[j], da_lambda_q1[j], da_lambda_k1[j], da_lambda_q2[j],
                         da_lambda_k2[j], da_subln[j], da_w_o[j], i)
        h = xattn_layer(h, mem, xattn_norm[i], mem_norm[i], xa_w_q[i], xa_w_kv[i], xa_w_o[i])
        h = moe_layer(h, ffn_norm[i], moe_w_router[i], moe_b_router[i], moe_w_gate_up,
                      moe_b_gate_up[i], moe_w_down, moe_b_down[i], i,
                      final_g=final_norm if i == depth - 1 else None)
    return h
```
